```python
import jax, jax.numpy as jnp
from jax import lax
import numpy as np

D_MODEL = 1024
BATCH = 32
SEQ = 256
DEPTH = 4
DEC_BATCH = 4
DEC_SEQ = 2048
PAST_LEN = 512

GRID_W = 64
N_EVEN = (DEPTH + 1) // 2
N_ODD = DEPTH // 2
NA_HEADS = 8
NA_HEAD_DIM = 64
NA_ROWS = 8
NA_COLS = 16
MLA_HEADS = 8
MLA_Q_RANK = 384
MLA_KV_RANK = 256
MLA_NOPE_DIM = 64
MLA_ROPE_DIM = 32
MLA_V_DIM = 64
RET_HEADS = 4
RET_DK = D_MODEL // RET_HEADS
RET_DV = 2 * D_MODEL // RET_HEADS
RET_CHUNK = 128
FFN_DIM = ((8 * D_MODEL // 3 + 127) // 128) * 128
N_MOD = 9
Q_BLOCK = 128
ROPE_BASE = 10000.0
EPS = 1e-6
NA_WIDTH = NA_HEADS * NA_HEAD_DIM
MLA_OUT = MLA_HEADS * MLA_V_DIM
EVEN_IN = 3 * NA_WIDTH + MLA_Q_RANK + MLA_KV_RANK + MLA_ROPE_DIM
RET_IN = 2 * RET_HEADS * RET_DK + 3 * RET_HEADS * RET_DV

kernel_name = 'hybrid_diffusion_na_mla_retention_step'


def _rms_norm(x, g):
    xf = x.astype(jnp.float32)
    y = xf * lax.rsqrt(jnp.mean(xf * xf, axis=-1, keepdims=True) + EPS)
    return (y * g.astype(jnp.float32)).astype(x.dtype)


def _head_norm(x, dtype):
    xf = x.astype(jnp.float32)
    mu = jnp.mean(xf, axis=-1, keepdims=True)
    var = jnp.mean(jnp.square(xf - mu), axis=-1, keepdims=True)
    return ((xf - mu) * lax.rsqrt(var + EPS)).astype(dtype)


def _adaln(cvec, w, b):
    m = jax.nn.silu(cvec) @ w + b
    return m.reshape(cvec.shape[0], 1, N_MOD, D_MODEL)


def _modulate(x, mod, s, g):
    h = _rms_norm(x, g)
    return h * (1 + mod[:, :, 3 * s + 1]) + mod[:, :, 3 * s]


def _swiglu(h, w13, w2):
    a, b = jnp.split(h @ w13, 2, axis=-1)
    return (jax.nn.silu(a) * b) @ w2


def _ffn_sublayer(x, mod, s, g, w13, w2):
    return x + 0.5 * mod[:, :, 3 * s + 2] * _swiglu(_modulate(x, mod, s, g), w13, w2)


def _rope_rotate(x, pos):
    half = x.shape[-1] // 2
    inv = ROPE_BASE ** (-jnp.arange(half, dtype=jnp.float32) / half)
    ang = pos.astype(jnp.float32)[:, None] * inv[None, :]
    cos = jnp.cos(ang)[:, None, :]
    sin = jnp.sin(ang)[:, None, :]
    xf = x.astype(jnp.float32)
    x1, x2 = xf[..., :half], xf[..., half:]
    return jnp.concatenate([x1 * cos - x2 * sin, x2 * cos + x1 * sin], axis=-1).astype(x.dtype)


def _axial_rope(x):
    L = x.shape[1]
    d = x.shape[-1] // 2
    t = jnp.arange(L)
    return jnp.concatenate([_rope_rotate(x[..., :d], t // GRID_W), _rope_rotate(x[..., d:], t % GRID_W)], axis=-1)


def _blocked_attention(q, k, v, scale):
    B, Lq, H, dk = q.shape
    nb = Lq // Q_BLOCK
    qb = q.reshape(B, nb, Q_BLOCK, H, dk).transpose(1, 0, 2, 3, 4)

    def one(qblk):
        s = jnp.einsum('bqhd,bkhd->bhqk', qblk, k).astype(jnp.float32) * scale
        p = jax.nn.softmax(s, axis=-1).astype(v.dtype)
        return jnp.einsum('bhqk,bkhd->bqhd', p, v)

    out = lax.map(one, qb)
    return out.transpose(1, 0, 2, 3, 4).reshape(B, Lq, H, v.shape[-1])


def _neighbourhood_attention(q, k, v, k_ctx, v_ctx, rpb):
    B, L, H, d = q.shape
    rows = L // GRID_W
    kr = min(NA_ROWS, rows)
    kc = NA_COLS
    r = jnp.arange(rows)
    r0 = jnp.clip(r - kr // 2, 0, rows - kr)
    key_rows = r0[:, None] + jnp.arange(kr)[None, :]
    cidx = jnp.arange(GRID_W)
    c0 = jnp.clip(cidx - kc // 2, 0, GRID_W - kc)
    in_win = (cidx[None, :] >= c0[:, None]) & (cidx[None, :] < c0[:, None] + kc)
    dr = key_rows - r[:, None] + NA_ROWS - 1
    dc = jnp.clip(cidx[None, :] - cidx[:, None], -(kc - 1), kc - 1) + kc - 1
    bias = rpb[:, dr[:, None, :, None], dc[None, :, None, :]].astype(jnp.float32)
    qg = q.reshape(B, rows, GRID_W, H, d)
    kg = k.reshape(B, rows, GRID_W, H, d)[:, key_rows]
    vg = v.reshape(B, rows, GRID_W, H, d)[:, key_rows]
    scale = d ** -0.5
    s_win = jnp.einsum('brchd,brmnhd->bhrcmn', qg, kg).astype(jnp.float32) * scale + bias[None]
    s_win = jnp.where(in_win[None, None, None, :, None, :], s_win, -jnp.inf)
    s_ctx = jnp.einsum('brchd,bkhd->bhrck', qg, k_ctx).astype(jnp.float32) * scale
    nwin = kr * GRID_W
    s = jnp.concatenate([s_win.reshape(B, H, rows, GRID_W, nwin), s_ctx], axis=-1)
    p = jax.nn.softmax(s, axis=-1).astype(v.dtype)
    p_win = p[..., :nwin].reshape(B, H, rows, GRID_W, kr, GRID_W)
    out = jnp.einsum('bhrcmn,brmnhd->brchd', p_win, vg) + jnp.einsum('bhrck,bkhd->brchd', p[..., nwin:], v_ctx)
    return out.reshape(B, L, H, d)


def _even_project(h, w_in, q_norm, kv_norm, w_uq):
    B, L, _ = h.shape
    p = h @ w_in
    qa = p[..., :NA_WIDTH].reshape(B, L, NA_HEADS, NA_HEAD_DIM)
    ka = p[..., NA_WIDTH:2 * NA_WIDTH].reshape(B, L, NA_HEADS, NA_HEAD_DIM)
    va = p[..., 2 * NA_WIDTH:3 * NA_WIDTH].reshape(B, L, NA_HEADS, NA_HEAD_DIM)
    o = 3 * NA_WIDTH
    cq = p[..., o:o + MLA_Q_RANK]
    o += MLA_Q_RANK
    ckv = p[..., o:o + MLA_KV_RANK]
    o += MLA_KV_RANK
    kr = p[..., o:o + MLA_ROPE_DIM][:, :, None, :]
    qb = (_rms_norm(cq, q_norm) @ w_uq).reshape(B, L, MLA_HEADS, MLA_NOPE_DIM + MLA_ROPE_DIM)
    return qa, ka, va, qb, _rms_norm(ckv, kv_norm), kr


def _mla_keys(ckv, kr, w_ukv):
    B, L, _ = ckv.shape
    kv = (ckv @ w_ukv).reshape(B, L, MLA_HEADS, MLA_NOPE_DIM + MLA_V_DIM)
    k = jnp.concatenate([kv[..., :MLA_NOPE_DIM], jnp.broadcast_to(kr, (B, L, MLA_HEADS, MLA_ROPE_DIM)).astype(kv.dtype)], axis=-1)
    return k, kv[..., MLA_NOPE_DIM:]


def _even_mixer_context(h, w_in, q_norm, kv_norm, w_uq, w_ukv, w_out):
    B, L, _ = h.shape
    qa, ka, va, qb, ckv, kr = _even_project(h, w_in, q_norm, kv_norm, w_uq)
    oa = _blocked_attention(qa, ka, va, NA_HEAD_DIM ** -0.5)
    kb, vb = _mla_keys(ckv, kr, w_ukv)
    ob = _blocked_attention(qb, kb, vb, (MLA_NOPE_DIM + MLA_ROPE_DIM) ** -0.5)
    y = jnp.concatenate([oa.reshape(B, L, NA_WIDTH), ob.reshape(B, L, MLA_OUT)], axis=-1) @ w_out
    return y, ka, va, ckv, kr[:, :, 0]


def _even_mixer_latent(h, k_ctx, v_ctx, ckv_ctx, kr_ctx, w_in, q_norm, kv_norm, w_uq, w_ukv, rpb, w_out):
    B, L, _ = h.shape
    qa, ka, va, qb, ckv, kr = _even_project(h, w_in, q_norm, kv_norm, w_uq)
    oa = _neighbourhood_attention(qa, ka, va, k_ctx, v_ctx, rpb)
    qb = jnp.concatenate([qb[..., :MLA_NOPE_DIM], _axial_rope(qb[..., MLA_NOPE_DIM:])], axis=-1)
    kb_lat, vb_lat = _mla_keys(ckv, _axial_rope(kr), w_ukv)
    kb_ctx, vb_ctx = _mla_keys(ckv_ctx, kr_ctx[:, :, None, :], w_ukv)
    kb = jnp.concatenate([kb_ctx, kb_lat], axis=1)
    vb = jnp.concatenate([vb_ctx, vb_lat], axis=1)
    ob = _blocked_attention(qb, kb, vb, (MLA_NOPE_DIM + MLA_ROPE_DIM) ** -0.5)
    return jnp.concatenate([oa.reshape(B, L, NA_WIDTH), ob.reshape(B, L, MLA_OUT)], axis=-1) @ w_out


def _retention_chunkwise(q, k, v, log_gamma, s0):
    B, L, H, dk = q.shape
    dv = v.shape[-1]
    n = L // RET_CHUNK

    def chunks(t):
        return t.reshape(B, n, RET_CHUNK, H, t.shape[-1]).transpose(1, 0, 3, 2, 4).astype(jnp.float32)

    pos = jnp.arange(RET_CHUNK, dtype=jnp.float32)
    diff = pos[:, None] - pos[None, :]
    intra = jnp.exp(jnp.where(diff[None] >= 0, log_gamma[:, None, None] * diff[None], -jnp.inf))
    q_dec = jnp.exp(log_gamma[:, None] * (pos[None, :] + 1.0))[None, :, :, None]
    k_dec = jnp.exp(log_gamma[:, None] * (RET_CHUNK - 1.0 - pos[None, :]))[None, :, :, None]
    c_dec = jnp.exp(log_gamma * RET_CHUNK)[None, :, None, None]

    def step(S, inp):
        qi, ki, vi = inp
        a = jnp.einsum('bhid,bhjd->bhij', qi, ki) * intra
        o = jnp.einsum('bhij,bhjv->bhiv', a, vi) + jnp.einsum('bhid,bhdv->bhiv', qi, S) * q_dec
        S = S * c_dec + jnp.einsum('bhjd,bhjv->bhdv', ki * k_dec, vi)
        return S, o

    S, o = lax.scan(step, s0.astype(jnp.float32), (chunks(q), chunks(k), chunks(v)))
    return o.transpose(1, 0, 3, 2, 4).reshape(B, L, H, dv), S


def _retention_mixer(h, w_in, log_decay, w_out, s0_f, s0_b, latent):
    B, L, _ = h.shape
    nk = RET_HEADS * RET_DK
    nv = RET_HEADS * RET_DV
    p = h @ w_in
    q = p[..., :nk].reshape(B, L, RET_HEADS, RET_DK)
    k = p[..., nk:2 * nk].reshape(B, L, RET_HEADS, RET_DK) * (RET_DK ** -0.5)
    v = p[..., 2 * nk:2 * nk + nv].reshape(B, L, RET_HEADS, RET_DV)
    g_f = p[..., 2 * nk + nv:2 * nk + 2 * nv]
    g_b = p[..., 2 * nk + 2 * nv:]
    if latent:
        q = _axial_rope(q)
        k = _axial_rope(k)
    lg = -jnp.abs(log_decay.astype(jnp.float32))
    o_f, s_f = _retention_chunkwise(q, k, v, lg[0], s0_f)
    o_b, s_b = _retention_chunkwise(q[:, ::-1], k[:, ::-1], v[:, ::-1], lg[1], s0_b)
    o_b = o_b[:, ::-1]
    y = jax.nn.silu(g_f) * _head_norm(o_f, h.dtype).reshape(B, L, nv) + jax.nn.silu(g_b) * _head_norm(o_b, h.dtype).reshape(B, L, nv)
    return y @ w_out, s_f, s_b


def setup_inputs(seed: int = 0) -> dict:
    key = jax.random.key(seed)
    ks = iter(jax.random.split(key, 32))
    f32 = jnp.float32
    D = D_MODEL

    def nrm(shape, scale):
        return jax.random.normal(next(ks), shape, f32) * scale

    base_decay = jnp.log(1.0 - 2.0 ** (-5.0 - jnp.arange(RET_HEADS, dtype=f32)))
    return {
        'x_prompt': nrm((BATCH, SEQ, D), 1.0),
        'x_sample': nrm((DEC_BATCH, DEC_SEQ, D), 1.0),
        'c': nrm((DEC_BATCH, D), 1.0),
        'cache_na_k': nrm((DEC_BATCH, N_EVEN, PAST_LEN, NA_HEADS, NA_HEAD_DIM), 1.0),
        'cache_na_v': nrm((DEC_BATCH, N_EVEN, PAST_LEN, NA_HEADS, NA_HEAD_DIM), 1.0),
        'cache_mla_ckv': nrm((DEC_BATCH, N_EVEN, PAST_LEN, MLA_KV_RANK), 1.0),
        'cache_mla_krope': nrm((DEC_BATCH, N_EVEN, PAST_LEN, MLA_ROPE_DIM), 1.0),
        'state_ret': nrm((DEC_BATCH, N_ODD, 2, RET_HEADS, RET_DK, RET_DV), 0.5),
        'c_ctx': nrm((D,), 1.0),
        'norm_g': 1.0 + nrm((DEPTH, 3, D), 0.05),
        'ada_w': nrm((DEPTH, D, N_MOD * D), 0.5 * D ** -0.5),
        'ada_b': nrm((DEPTH, N_MOD * D), 0.02),
        'ffn_w13': nrm((DEPTH, 2, D, 2 * FFN_DIM), D ** -0.5),
        'ffn_w2': nrm((DEPTH, 2, FFN_DIM, D), FFN_DIM ** -0.5),
        'mix_w_in': nrm((N_EVEN, D, EVEN_IN), D ** -0.5),
        'mla_q_norm': 1.0 + nrm((N_EVEN, MLA_Q_RANK), 0.05),
        'mla_kv_norm': 1.0 + nrm((N_EVEN, MLA_KV_RANK), 0.05),
        'mla_w_uq': nrm((N_EVEN, MLA_Q_RANK, MLA_HEADS * (MLA_NOPE_DIM + MLA_ROPE_DIM)), MLA_Q_RANK ** -0.5),
        'mla_w_ukv': nrm((N_EVEN, MLA_KV_RANK, MLA_HEADS * (MLA_NOPE_DIM + MLA_V_DIM)), MLA_KV_RANK ** -0.5),
        'na_rpb': nrm((N_EVEN, NA_HEADS, 2 * NA_ROWS - 1, 2 * NA_COLS - 1), 0.05),
        'mix_w_out': nrm((N_EVEN, NA_WIDTH + MLA_OUT, D), (NA_WIDTH + MLA_OUT) ** -0.5),
        'ret_w_in': nrm((N_ODD, D, RET_IN), D ** -0.5),
        'ret_log_decay': base_decay[None, None, :] * (1.0 + nrm((N_ODD, 2, RET_HEADS), 0.1)),
        'ret_w_out': nrm((N_ODD, RET_HEADS * RET_DV, D), (RET_HEADS * RET_DV) ** -0.5),
        'final_norm_g': 1.0 + nrm((D,), 0.05),
    }


def reference(x_prompt, x_sample, c, cache_na_k, cache_na_v, cache_mla_ckv, cache_mla_krope, state_ret,
              c_ctx, norm_g, ada_w, ada_b, ffn_w13, ffn_w2, mix_w_in, mla_q_norm, mla_kv_norm, mla_w_uq,
              mla_w_ukv, na_rpb, mix_w_out, ret_w_in, ret_log_decay, ret_w_out, final_norm_g):
    xp, xs = x_prompt, x_sample
    new_k, new_v, new_ckv, new_kr, new_ret = [], [], [], [], []
    for l in range(DEPTH):
        mp = _adaln(c_ctx[None, :], ada_w[l], ada_b[l])
        ms = _adaln(c, ada_w[l], ada_b[l])
        xp = _ffn_sublayer(xp, mp, 0, norm_g[l, 0], ffn_w13[l, 0], ffn_w2[l, 0])
        xs = _ffn_sublayer(xs, ms, 0, norm_g[l, 0], ffn_w13[l, 0], ffn_w2[l, 0])
        hp = _modulate(xp, mp, 1, norm_g[l, 1])
        hs = _modulate(xs, ms, 1, norm_g[l, 1])
        i = l // 2
        if l % 2 == 0:
            yp, ka, va, ckv, kr = _even_mixer_context(hp, mix_w_in[i], mla_q_norm[i], mla_kv_norm[i],
                                                      mla_w_uq[i], mla_w_ukv[i], mix_w_out[i])
            new_k.append(ka)
            new_v.append(va)
            new_ckv.append(ckv)
            new_kr.append(kr)
            ys = _even_mixer_latent(hs, cache_na_k[:, i], cache_na_v[:, i], cache_mla_ckv[:, i],
                                    cache_mla_krope[:, i], mix_w_in[i], mla_q_norm[i], mla_kv_norm[i],
                                    mla_w_uq[i], mla_w_ukv[i], na_rpb[i], mix_w_out[i])
        else:
            zeros = jnp.zeros((xp.shape[0], RET_HEADS, RET_DK, RET_DV), jnp.float32)
            yp, s_f, s_b = _retention_mixer(hp, ret_w_in[i], ret_log_decay[i], ret_w_out[i], zeros, zeros, False)
            new_ret.append(jnp.stack([s_f, s_b], axis=1).astype(x_prompt.dtype))
            ys, _, _ = _retention_mixer(hs, ret_w_in[i], ret_log_decay[i], ret_w_out[i],
                                        state_ret[:, i, 0], state_ret[:, i, 1], True)
        xp = xp + mp[:, :, 5] * yp
        xs = xs + ms[:, :, 5] * ys
        xp = _ffn_sublayer(xp, mp, 2, norm_g[l, 2], ffn_w13[l, 1], ffn_w2[l, 1])
        xs = _ffn_sublayer(xs, ms, 2, norm_g[l, 2], ffn_w13[l, 1], ffn_w2[l, 1])
    y_prompt = _rms_norm(xp, final_norm_g)
    y_sample = _rms_norm(xs, final_norm_g)
    new_na_k = jnp.stack(new_k, axis=1)
    new_na_v = jnp.stack(new_v, axis=1)
    new_mla_ckv = jnp.stack(new_ckv, axis=1)
    new_mla_krope = jnp.stack(new_kr, axis=1)
    new_state_ret = jnp.stack(new_ret, axis=1)
    return (y_prompt, y_sample, new_na_k, new_na_v, new_mla_ckv, new_mla_krope, new_state_ret)
```

```python
import functools

import jax
import jax.numpy as jnp
import numpy as np
from jax import lax
from jax.experimental import pallas as pl
from jax.experimental.pallas import tpu as pltpu

F32 = jnp.float32
BF16 = jnp.bfloat16

D_MODEL = 1024
DEPTH = 4
GRID_W = 64
NA_HEADS = 8
NA_HEAD_DIM = 64
NA_ROWS = 8
NA_COLS = 16
MLA_HEADS = 8
MLA_Q_RANK = 384
MLA_KV_RANK = 256
MLA_NOPE_DIM = 64
MLA_ROPE_DIM = 32
MLA_V_DIM = 64
RET_HEADS = 4
RET_DK = D_MODEL // RET_HEADS
RET_DV = 2 * D_MODEL // RET_HEADS
FFN_DIM = ((8 * D_MODEL // 3 + 127) // 128) * 128
N_MOD = 9
ROPE_BASE = 10000.0
EPS = 1e-6
NA_WIDTH = NA_HEADS * NA_HEAD_DIM
MLA_OUT = MLA_HEADS * MLA_V_DIM

LANES = 128
MOD_ROWS = 8
NEG_BIG = -1e30
VMEM_LIMIT = 52 * 1024 * 1024

FFN_TF = 256
RET_CHUNK = 256
NA_QROWS = 8
NA_KROWS = 16


def _cparams(n_axes):
    return pltpu.CompilerParams(dimension_semantics=("arbitrary",) * n_axes,
                                vmem_limit_bytes=VMEM_LIMIT)


def _resident(block, index_map):
    return pl.BlockSpec(block, index_map, pipeline_mode=pl.Buffered(1))


def _dot(a, b):
    return jnp.dot(a, b, preferred_element_type=F32)


def _dot_nt(a, b):
    return lax.dot_general(a, b, (((1,), (1,)), ((), ())), preferred_element_type=F32)


def _silu(x):
    return x * jax.nn.sigmoid(x)


def _rms(x, g):
    return x * lax.rsqrt(jnp.mean(x * x, axis=-1, keepdims=True) + EPS) * g


def _modulated(x, mod_ref, g_ref, s):
    shift = mod_ref[0, :, 3 * s * D_MODEL:(3 * s + 1) * D_MODEL]
    scale = mod_ref[0, :, (3 * s + 1) * D_MODEL:(3 * s + 2) * D_MODEL]
    return _rms(x, g_ref[0]) * (1.0 + scale) + shift


def _gate(mod_ref, s):
    return mod_ref[0, :, (3 * s + 2) * D_MODEL:(3 * s + 3) * D_MODEL]


def _mod_index(layer, latent, tile_rows, seq):
    if not latent:
        return lambda i: (layer * MOD_ROWS, 0, 0)
    return lambda i: (layer * MOD_ROWS + 1 + (i * tile_rows) // seq, 0, 0)


def _ada_kernel(c_ref, w_ref, b_ref, o_ref):
    s = _silu(c_ref[...]).astype(BF16)
    o_ref[0] = _dot(s, w_ref[0].astype(BF16)) + b_ref[0]


def _ada_all(cvec, ada_w, ada_b):
    tn = 1024
    n = N_MOD * D_MODEL
    return pl.pallas_call(
        _ada_kernel,
        grid=(DEPTH, n // tn),
        in_specs=[pl.BlockSpec((MOD_ROWS, D_MODEL), lambda l, j: (0, 0)),
                  pl.BlockSpec((1, D_MODEL, tn), lambda l, j: (l, 0, j)),
                  pl.BlockSpec((1, 1, tn), lambda l, j: (l, 0, j))],
        out_specs=pl.BlockSpec((1, MOD_ROWS, tn), lambda l, j: (l, 0, j)),
        out_shape=jax.ShapeDtypeStruct((DEPTH, MOD_ROWS, n), F32),
        compiler_params=_cparams(2),
        name="adaln",
    )(cvec, ada_w, ada_b.reshape(DEPTH, 1, n))


def _ffn_kernel(x_ref, mod_ref, g_ref, w13_ref, w2_ref, o_ref, *, s):
    x = x_ref[...]
    h = _modulated(x, mod_ref, g_ref, s).astype(BF16)
    acc = jnp.zeros(x.shape, F32)
    for j in range(FFN_DIM // FFN_TF):
        a = _dot(h, w13_ref[0, 0, :, j * FFN_TF:(j + 1) * FFN_TF])
        b = _dot(h, w13_ref[0, 0, :, FFN_DIM + j * FFN_TF:FFN_DIM + (j + 1) * FFN_TF])
        u = (_silu(a) * b).astype(BF16)
        acc = acc + _dot(u, w2_ref[0, 0, j * FFN_TF:(j + 1) * FFN_TF, :])
    o_ref[...] = x + 0.5 * _gate(mod_ref, s) * acc


def _ffn(x, mods, norm_g, w13, w2, layer, which, latent, seq):
    tm = 512
    m = x.shape[0]
    s = 0 if which == 0 else 2
    return pl.pallas_call(
        functools.partial(_ffn_kernel, s=s),
        grid=(m // tm,),
        in_specs=[pl.BlockSpec((tm, D_MODEL), lambda i: (i, 0)),
                  pl.BlockSpec((1, 1, N_MOD * D_MODEL), _mod_index(layer, latent, tm, seq)),
                  pl.BlockSpec((1, 1, D_MODEL), lambda i: (layer * 3 + s, 0, 0)),
                  _resident((1, 1, D_MODEL, 2 * FFN_DIM), lambda i: (layer, which, 0, 0)),
                  _resident((1, 1, FFN_DIM, D_MODEL), lambda i: (layer, which, 0, 0))],
        out_specs=pl.BlockSpec((tm, D_MODEL), lambda i: (i, 0)),
        out_shape=jax.ShapeDtypeStruct((m, D_MODEL), F32),
        compiler_params=_cparams(1),
        name="ffn",
    )(x, mods, norm_g, w13, w2)


def _out_proj_kernel(*refs, n_in):
    x_ref, mod_ref = refs[0], refs[1]
    o_ref = refs[-1]
    acc = None
    for t in range(n_in):
        y = _dot(refs[2 + t][...], refs[2 + n_in + t][0])
        acc = y if acc is None else acc + y
    o_ref[...] = x_ref[...] + _gate(mod_ref, 1) * acc


def _out_proj(x, mods, ys, ws, w_layer, layer, latent, seq):
    tm = 512
    m = x.shape[0]
    n_in = len(ys)
    in_specs = [pl.BlockSpec((tm, D_MODEL), lambda i: (i, 0)),
                pl.BlockSpec((1, 1, N_MOD * D_MODEL), _mod_index(layer, latent, tm, seq))]
    for y in ys:
        in_specs.append(pl.BlockSpec((tm, y.shape[1]), lambda i: (i, 0)))
    w_args = []
    for w, rows, blk in ws:
        in_specs.append(_resident((1, rows, D_MODEL), lambda i, blk=blk: (w_layer, blk, 0)))
        w_args.append(w)
    return pl.pallas_call(
        functools.partial(_out_proj_kernel, n_in=n_in),
        grid=(m // tm,),
        in_specs=in_specs,
        out_specs=pl.BlockSpec((tm, D_MODEL), lambda i: (i, 0)),
        out_shape=jax.ShapeDtypeStruct((m, D_MODEL), F32),
        compiler_params=_cparams(1),
        name="out_proj",
    )(x, mods, *ys, *w_args)


def _final_norm_kernel(x_ref, g_ref, o_ref):
    o_ref[...] = _rms(x_ref[...], g_ref[...])


def _final_norm(x, g):
    tm = 512
    m = x.shape[0]
    return pl.pallas_call(
        _final_norm_kernel,
        grid=(m // tm,),
        in_specs=[pl.BlockSpec((tm, D_MODEL), lambda i: (i, 0)),
                  pl.BlockSpec((1, D_MODEL), lambda i: (0, 0))],
        out_specs=pl.BlockSpec((tm, D_MODEL), lambda i: (i, 0)),
        out_shape=jax.ShapeDtypeStruct((m, D_MODEL), F32),
        compiler_params=_cparams(1),
        name="final_norm",
    )(x, g.reshape(1, D_MODEL))


_EV_QA = 0
_EV_KA = NA_WIDTH
_EV_VA = 2 * NA_WIDTH
_EV_CQ = 3 * NA_WIDTH
_EV_CKV = _EV_CQ + MLA_Q_RANK
_EV_KR = _EV_CKV + MLA_KV_RANK
_EV_KRS = _EV_KR + LANES
_EV_COLS = _EV_KRS + LANES
MLA_QK_W = MLA_HEADS * LANES


def _rope_swap_perm(width, half):
    j = np.arange(width)
    return np.where((j % (2 * half)) < half, j + half, j - half)


def _pack_even_weights(mix_w_in, mla_w_uq, mla_w_ukv):
    n_even = mix_w_in.shape[0]
    o = 3 * NA_WIDTH + MLA_Q_RANK + MLA_KV_RANK
    w_kr = mix_w_in[:, :, o:o + MLA_ROPE_DIM]
    w_krs = w_kr[:, :, _rope_swap_perm(MLA_ROPE_DIM, MLA_ROPE_DIM // 4)]
    pad = ((0, 0), (0, 0), (MLA_NOPE_DIM, LANES - MLA_NOPE_DIM - MLA_ROPE_DIM))
    w_in = jnp.concatenate([mix_w_in[:, :, :o], jnp.pad(w_kr, pad), jnp.pad(w_krs, pad)], axis=-1)

    uq = mla_w_uq.reshape(n_even, MLA_Q_RANK, MLA_HEADS, MLA_NOPE_DIM + MLA_ROPE_DIM)
    uq_rope = uq[..., MLA_NOPE_DIM:]
    uq_rope_s = uq_rope[..., _rope_swap_perm(MLA_ROPE_DIM, MLA_ROPE_DIM // 4)]
    zpad = LANES - MLA_NOPE_DIM - MLA_ROPE_DIM
    w_uq = jnp.pad(uq, ((0, 0), (0, 0), (0, 0), (0, zpad))).reshape(n_even, MLA_Q_RANK, MLA_QK_W)
    w_uqs = jnp.pad(uq_rope_s, ((0, 0), (0, 0), (0, 0), (MLA_NOPE_DIM, zpad)))
    w_uqs = w_uqs.reshape(n_even, MLA_Q_RANK, MLA_QK_W)

    ukv = mla_w_ukv.reshape(n_even, MLA_KV_RANK, MLA_HEADS, MLA_NOPE_DIM + MLA_V_DIM)
    w_uk = jnp.pad(ukv[..., :MLA_NOPE_DIM], ((0, 0), (0, 0), (0, 0), (0, LANES - MLA_NOPE_DIM)))
    w_uk = w_uk.reshape(n_even, MLA_KV_RANK, MLA_QK_W)
    w_uv = ukv[..., MLA_NOPE_DIM:].reshape(n_even, MLA_KV_RANK, MLA_OUT)
    return (w_in.astype(BF16), w_uq.astype(BF16), w_uqs.astype(BF16), w_uk.astype(BF16),
            w_uv.astype(BF16))


def _mla_rope_tables(seq):
    half = MLA_ROPE_DIM // 4
    t = jnp.arange(seq)
    j = np.arange(MLA_ROPE_DIM)
    pos = jnp.where((j // (2 * half))[None, :] == 0, (t // GRID_W)[:, None], (t % GRID_W)[:, None])
    inv = ROPE_BASE ** (-jnp.arange(half, dtype=F32) / half)
    ang = pos.astype(F32) * inv[j % half][None, :]
    sign = np.where((j % (2 * half)) < half, -1.0, 1.0).astype(np.float32)
    cos = jnp.concatenate([jnp.ones((seq, MLA_NOPE_DIM), F32), jnp.cos(ang),
                           jnp.ones((seq, LANES - MLA_NOPE_DIM - MLA_ROPE_DIM), F32)], axis=1)
    sin = jnp.concatenate([jnp.zeros((seq, MLA_NOPE_DIM), F32), jnp.sin(ang) * sign[None, :],
                           jnp.zeros((seq, LANES - MLA_NOPE_DIM - MLA_ROPE_DIM), F32)], axis=1)
    return cos, sin


def _even_proj_kernel(*refs, latent):
    (x_ref, mod_ref, g_ref, win_ref, qn_ref, kvn_ref, wuq_ref, wuqs_ref, wuk_ref, wuv_ref) = refs[:10]
    if latent:
        cos_ref, sin_ref = refs[10:12]
        qa_ref, ka_ref, va_ref, qm_ref, km_ref, vm_ref = refs[12:]
    else:
        qa_ref, ka_ref, va_ref, qm_ref, km_ref, vm_ref, ckv_ref, kr_ref = refs[10:]

    def w_in(lo, width):
        return win_ref[0, :, lo:lo + width]

    h = _modulated(x_ref[...], mod_ref, g_ref, 1).astype(BF16)
    qa_ref[...] = (_dot(h, w_in(_EV_QA, NA_WIDTH)) * (NA_HEAD_DIM ** -0.5)).astype(qa_ref.dtype)
    ka = _dot(h, w_in(_EV_KA, NA_WIDTH))
    va = _dot(h, w_in(_EV_VA, NA_WIDTH))
    if latent:
        ka_ref[...] = ka.astype(BF16)
        va_ref[...] = va.astype(BF16)
    else:
        ka_ref[:, 0] = ka.reshape(ka_ref.shape[0], ka_ref.shape[2], NA_WIDTH)
        va_ref[:, 0] = va.reshape(va_ref.shape[0], va_ref.shape[2], NA_WIDTH)

    cqn = _rms(_dot(h, w_in(_EV_CQ, MLA_Q_RANK)), qn_ref[0]).astype(BF16)
    ckvn = _rms(_dot(h, w_in(_EV_CKV, MLA_KV_RANK)), kvn_ref[0])
    kr = _dot(h, w_in(_EV_KR, LANES))
    if latent:
        cos = cos_ref[...]
        sin = sin_ref[...]
        kr = kr * cos + _dot(h, w_in(_EV_KRS, LANES)) * sin
    else:
        ckv_ref[:, 0] = ckvn.reshape(ckv_ref.shape[0], ckv_ref.shape[2], MLA_KV_RANK)
        kr_ref[:, 0] = kr[:, MLA_NOPE_DIM:MLA_NOPE_DIM + MLA_ROPE_DIM].reshape(
            kr_ref.shape[0], kr_ref.shape[2], MLA_ROPE_DIM)
    ckvb = ckvn.astype(BF16)

    qm = _dot(cqn, wuq_ref[0])
    km = _dot(ckvb, wuk_ref[0])
    if latent:
        qms = _dot(cqn, wuqs_ref[0])
    for hd in range(MLA_HEADS):
        sl = slice(hd * LANES, (hd + 1) * LANES)
        q_h = qm[:, sl]
        if latent:
            q_h = q_h * cos + qms[:, sl] * sin
        qm_ref[:, sl] = q_h.astype(BF16)
        km_ref[:, sl] = (km[:, sl] + kr).astype(BF16)
    vm_ref[...] = _dot(ckvb, wuv_ref[0]).astype(BF16)


def _even_proj(x, mods, norm_g, packed, q_norm, kv_norm, layer, latent, seq, rope=None):
    w_in, w_uq, w_uqs, w_uk, w_uv = packed
    i_even = layer // 2
    m = x.shape[0]
    tm = 512
    bt = tm // seq if not latent else 0
    tok = lambda i: (i, 0)
    in_specs = [pl.BlockSpec((tm, D_MODEL), tok),
                pl.BlockSpec((1, 1, N_MOD * D_MODEL), _mod_index(layer, latent, tm, seq)),
                pl.BlockSpec((1, 1, D_MODEL), lambda i: (layer * 3 + 1, 0, 0)),
                _resident((1, D_MODEL, _EV_COLS), lambda i: (i_even, 0, 0)),
                pl.BlockSpec((1, 1, MLA_Q_RANK), lambda i: (i_even, 0, 0)),
                pl.BlockSpec((1, 1, MLA_KV_RANK), lambda i: (i_even, 0, 0)),
                _resident((1, MLA_Q_RANK, MLA_QK_W), lambda i: (i_even, 0, 0)),
                _resident((1, MLA_Q_RANK, MLA_QK_W), lambda i: (i_even, 0, 0)),
                _resident((1, MLA_KV_RANK, MLA_QK_W), lambda i: (i_even, 0, 0)),
                _resident((1, MLA_KV_RANK, MLA_OUT), lambda i: (i_even, 0, 0))]
    args = [x, mods, norm_g, w_in, q_norm, kv_norm, w_uq, w_uqs, w_uk, w_uv]
    bf = lambda w: jax.ShapeDtypeStruct((m, w), BF16)
    if latent:
        nt = seq // tm
        in_specs += [pl.BlockSpec((tm, LANES), lambda i: (i % nt, 0))] * 2
        args += list(rope)
        out_shape = [bf(NA_WIDTH), bf(NA_WIDTH), bf(NA_WIDTH), bf(MLA_QK_W), bf(MLA_QK_W), bf(MLA_OUT)]
        out_specs = [pl.BlockSpec((tm, s.shape[1]), tok) for s in out_shape]
    else:
        nb = m // seq
        cache = lambda w: jax.ShapeDtypeStruct((nb, 1, seq, w), F32)
        cspec = lambda w: pl.BlockSpec((bt, 1, seq, w), lambda i: (i, 0, 0, 0))
        out_shape = [bf(NA_WIDTH), cache(NA_WIDTH), cache(NA_WIDTH), bf(MLA_QK_W), bf(MLA_QK_W),
                     bf(MLA_OUT), cache(MLA_KV_RANK), cache(MLA_ROPE_DIM)]
        out_specs = [pl.BlockSpec((tm, NA_WIDTH), tok), cspec(NA_WIDTH), cspec(NA_WIDTH),
                     pl.BlockSpec((tm, MLA_QK_W), tok), pl.BlockSpec((tm, MLA_QK_W), tok),
                     pl.BlockSpec((tm, MLA_OUT), tok), cspec(MLA_KV_RANK), cspec(MLA_ROPE_DIM)]
    return pl.pallas_call(
        functools.partial(_even_proj_kernel, latent=latent),
        grid=(m // tm,),
        in_specs=in_specs,
        out_specs=out_specs,
        out_shape=out_shape,
        compiler_params=_cparams(1),
        name="even_proj_latent" if latent else "even_proj_context",
    )(*args)


def _mla_cache_kernel(ckv_ref, kr_ref, wuk_ref, wuv_ref, km_ref, vm_ref):
    ckvb = ckv_ref[0, 0].astype(BF16)
    km = _dot(ckvb, wuk_ref[0])
    kr = kr_ref[...]
    for hd in range(MLA_HEADS):
        sl = slice(hd * LANES, (hd + 1) * LANES)
        km_ref[:, sl] = (km[:, sl] + kr).astype(BF16)
    vm_ref[...] = _dot(ckvb, wuv_ref[0]).astype(BF16)


def _mla_cache_keys(cache_ckv, kr_padded, w_uk, w_uv, i_even):
    nb, _, past, _ = cache_ckv.shape
    return pl.pallas_call(
        _mla_cache_kernel,
        grid=(nb,),
        in_specs=[pl.BlockSpec((1, 1, past, MLA_KV_RANK), lambda b: (b, i_even, 0, 0)),
                  pl.BlockSpec((past, LANES), lambda b: (b, 0)),
                  _resident((1, MLA_KV_RANK, MLA_QK_W), lambda b: (i_even, 0, 0)),
                  _resident((1, MLA_KV_RANK, MLA_OUT), lambda b: (i_even, 0, 0))],
        out_specs=[pl.BlockSpec((past, MLA_QK_W), lambda b: (b, 0)),
                   pl.BlockSpec((past, MLA_OUT), lambda b: (b, 0))],
        out_shape=[jax.ShapeDtypeStruct((nb * past, MLA_QK_W), BF16),
                   jax.ShapeDtypeStruct((nb * past, MLA_OUT), BF16)],
        compiler_params=_cparams(1),
        name="mla_cache_keys",
    )(cache_ckv, kr_padded, w_uk, w_uv)


def _half_masks():
    lane = lax.broadcasted_iota(jnp.int32, (1, LANES), 1)
    lo = lane < (LANES // 2)
    return lo, jnp.logical_not(lo)


def _softmax_pv(score_blocks, value_blocks):
    m = None
    for s in score_blocks:
        sm = jnp.max(s, axis=-1, keepdims=True)
        m = sm if m is None else jnp.maximum(m, sm)
    l = None
    acc = None
    for s, v in zip(score_blocks, value_blocks):
        p = jnp.exp(s - m)
        ps = jnp.sum(p, axis=-1, keepdims=True)
        l = ps if l is None else l + ps
        o = _dot(p.astype(BF16), v)
        acc = o if acc is None else acc + o
    return acc / l


def _ctx_attn_kernel(qa_ref, ka_ref, va_ref, qm_ref, km_ref, vm_ref, o_ref):
    lo, hi = _half_masks()
    zero = jnp.zeros((), BF16)
    mla_scale = (MLA_NOPE_DIM + MLA_ROPE_DIM) ** -0.5
    for g in range(NA_WIDTH // LANES):
        sl = slice(g * LANES, (g + 1) * LANES)
        q = qa_ref[:, sl]
        k = ka_ref[0, 0, :, sl].astype(BF16)
        v = va_ref[0, 0, :, sl].astype(BF16)
        acc = None
        for msk in (lo, hi):
            s = _dot_nt(jnp.where(msk, q, zero), k)
            o = _softmax_pv([s], [jnp.where(msk, v, zero)])
            acc = o if acc is None else acc + o
        o_ref[:, sl] = acc.astype(BF16)
    for g in range(MLA_OUT // LANES):
        v = vm_ref[:, g * LANES:(g + 1) * LANES]
        acc = None
        for t, msk in enumerate((lo, hi)):
            hd = 2 * g + t
            sl = slice(hd * LANES, (hd + 1) * LANES)
            s = _dot_nt(qm_ref[:, sl], km_ref[:, sl]) * mla_scale
            o = _softmax_pv([s], [jnp.where(msk, v, zero)])
            acc = o if acc is None else acc + o
        o_ref[:, NA_WIDTH + g * LANES:NA_WIDTH + (g + 1) * LANES] = acc.astype(BF16)


def _ctx_attention(qa, ka, va, qm, km, vm, seq):
    m = qa.shape[0]
    nb = m // seq
    tok = lambda b: (b, 0)
    cache = lambda b: (b, 0, 0, 0)
    return pl.pallas_call(
        _ctx_attn_kernel,
        grid=(nb,),
        in_specs=[pl.BlockSpec((seq, NA_WIDTH), tok),
                  pl.BlockSpec((1, 1, seq, NA_WIDTH), cache),
                  pl.BlockSpec((1, 1, seq, NA_WIDTH), cache),
                  pl.BlockSpec((seq, MLA_QK_W), tok),
                  pl.BlockSpec((seq, MLA_QK_W), tok),
                  pl.BlockSpec((seq, MLA_OUT), tok)],
        out_specs=pl.BlockSpec((seq, NA_WIDTH + MLA_OUT), tok),
        out_shape=jax.ShapeDtypeStruct((m, NA_WIDTH + MLA_OUT), BF16),
        compiler_params=_cparams(1),
        name="context_attention",
    )(qa, ka, va, qm, km, vm)


def _lat_mla_kernel(q_ref, kc_ref, kl_ref, vc_ref, vl_ref, o_ref):
    lo, hi = _half_masks()
    zero = jnp.zeros((), BF16)
    scale = (MLA_NOPE_DIM + MLA_ROPE_DIM) ** -0.5
    vc = vc_ref[...]
    vl = vl_ref[...]
    acc = None
    for t, msk in enumerate((lo, hi)):
        sl = slice(t * LANES, (t + 1) * LANES)
        q = q_ref[:, sl]
        s_c = _dot_nt(q, kc_ref[:, sl]) * scale
        s_l = _dot_nt(q, kl_ref[:, sl]) * scale
        o = _softmax_pv([s_c, s_l], [jnp.where(msk, vc, zero), jnp.where(msk, vl, zero)])
        acc = o if acc is None else acc + o
    o_ref[...] = acc.astype(BF16)


def _latent_mla_attention(qm, km_ctx, km_lat, vm_ctx, vm_lat, seq, past):
    m = qm.shape[0]
    nb = m // seq
    tq = 512
    nq = seq // tq
    pair = 2 * LANES
    return pl.pallas_call(
        _lat_mla_kernel,
        grid=(nb, MLA_HEADS // 2, nq),
        in_specs=[pl.BlockSpec((tq, pair), lambda b, g, i: (b * nq + i, g)),
                  pl.BlockSpec((past, pair), lambda b, g, i: (b, g)),
                  pl.BlockSpec((seq, pair), lambda b, g, i: (b, g)),
                  pl.BlockSpec((past, LANES), lambda b, g, i: (b, g)),
                  pl.BlockSpec((seq, LANES), lambda b, g, i: (b, g))],
        out_specs=pl.BlockSpec((tq, LANES), lambda b, g, i: (b * nq + i, g)),
        out_shape=jax.ShapeDtypeStruct((m, MLA_OUT), BF16),
        compiler_params=_cparams(3),
        name="latent_mla_attention",
    )(qm, km_ctx, km_lat, vm_ctx, vm_lat)


def _na_key_start(row_block, rows):
    return np.clip(row_block * NA_QROWS - NA_ROWS // 2, 0, rows - NA_KROWS)


def _na_bias_kernel(rpb_ref, o_ref, *, rows):
    hd = pl.program_id(0)
    kr = min(NA_ROWS, rows)
    n_dc = 2 * NA_COLS - 1
    c = lax.broadcasted_iota(jnp.int32, (GRID_W, LANES), 0)
    lane = lax.broadcasted_iota(jnp.int32, (GRID_W, LANES), 1)
    n = lane % GRID_W
    c0 = jnp.clip(c - NA_COLS // 2, 0, GRID_W - NA_COLS)
    in_win = (n >= c0) & (n < c0 + NA_COLS)
    dc = n - c + NA_COLS - 1
    left = lane < GRID_W
    neg = jnp.full((GRID_W, LANES), NEG_BIG, F32)
    col_tables = []
    for dr in range(2 * NA_ROWS - 1):
        t = neg
        for j in range(n_dc):
            t = jnp.where(dc == j, rpb_ref[hd * (2 * NA_ROWS - 1) * n_dc + dr * n_dc + j], t)
        col_tables.append(jnp.where(in_win, t, neg))
    for rb in range(rows // NA_QROWS):
        start = int(_na_key_start(rb, rows))
        for rl in range(NA_QROWS):
            r = rb * NA_QROWS + rl
            r0 = min(max(r - kr // 2, 0), rows - kr)
            for pr in range(NA_KROWS // 2):
                halves = []
                for key_row in (start + 2 * pr, start + 2 * pr + 1):
                    ok = r0 <= key_row < r0 + kr
                    halves.append(col_tables[key_row - r + NA_ROWS - 1] if ok else neg)
                o_ref[rb, 0, rl * GRID_W:(rl + 1) * GRID_W, pr * LANES:(pr + 1) * LANES] = (
                    jnp.where(left, halves[0], halves[1]))


def _na_bias(rpb, rows):
    n_rb = rows // NA_QROWS
    flat = rpb.reshape(-1)
    return pl.pallas_call(
        functools.partial(_na_bias_kernel, rows=rows),
        grid=(NA_HEADS,),
        in_specs=[pl.BlockSpec(memory_space=pltpu.SMEM)],
        out_specs=pl.BlockSpec((n_rb, 1, NA_QROWS * GRID_W, NA_KROWS * GRID_W), lambda h: (0, h, 0, 0)),
        out_shape=jax.ShapeDtypeStruct((n_rb, NA_HEADS, NA_QROWS * GRID_W, NA_KROWS * GRID_W), F32),
        compiler_params=_cparams(1),
        name="na_bias",
    )(flat)


def _lat_na_kernel(q_ref, k_ref, v_ref, kc_ref, vc_ref, bias_ref, o_ref, *, rows):
    lo, hi = _half_masks()
    zero = jnp.zeros((), BF16)
    rb = pl.program_id(0)
    start = jnp.clip(rb * NA_QROWS - NA_ROWS // 2, 0, rows - NA_KROWS) * GRID_W
    start = pl.multiple_of(start, GRID_W * (NA_ROWS // 2))
    nk = NA_KROWS * GRID_W
    q = q_ref[...]
    kw = k_ref[pl.ds(start, nk), :]
    vw = v_ref[pl.ds(start, nk), :]
    kc = kc_ref[0, 0].astype(BF16)
    vc = vc_ref[0, 0].astype(BF16)
    acc = None
    for t, msk in enumerate((lo, hi)):
        qh = jnp.where(msk, q, zero)
        s_w = _dot_nt(qh, kw) + bias_ref[0, t]
        s_c = _dot_nt(qh, kc)
        o = _softmax_pv([s_w, s_c], [jnp.where(msk, vw, zero), jnp.where(msk, vc, zero)])
        acc = o if acc is None else acc + o
    o_ref[...] = acc.astype(BF16)


def _latent_na_attention(qa, ka, va, cache_k, cache_v, bias, i_even, seq):
    m = qa.shape[0]
    nb = m // seq
    rows = seq // GRID_W
    n_rb = rows // NA_QROWS
    tq = NA_QROWS * GRID_W
    past = cache_k.shape[2]
    return pl.pallas_call(
        functools.partial(_lat_na_kernel, rows=rows),
        grid=(n_rb, NA_HEADS // 2, nb),
        in_specs=[pl.BlockSpec((tq, LANES), lambda r, g, b: (b * n_rb + r, g)),
                  pl.BlockSpec((seq, LANES), lambda r, g, b: (b, g)),
                  pl.BlockSpec((seq, LANES), lambda r, g, b: (b, g)),
                  pl.BlockSpec((1, 1, past, LANES), lambda r, g, b: (b, i_even, 0, g)),
                  pl.BlockSpec((1, 1, past, LANES), lambda r, g, b: (b, i_even, 0, g)),
                  pl.BlockSpec((1, 2, tq, NA_KROWS * GRID_W), lambda r, g, b: (r, g, 0, 0))],
        out_specs=pl.BlockSpec((tq, LANES), lambda r, g, b: (b * n_rb + r, g)),
        out_shape=jax.ShapeDtypeStruct((m, NA_WIDTH), BF16),
        compiler_params=_cparams(3),
        name="latent_na_attention",
    )(qa, ka, va, cache_k, cache_v, bias)


_RET_NK = RET_HEADS * RET_DK
_RET_NV = RET_HEADS * RET_DV


def _ret_proj_kernel(x_ref, mod_ref, g_ref, w_ref, qk_ref, v_ref, gate_ref):
    h = _modulated(x_ref[...], mod_ref, g_ref, 1).astype(BF16)
    step = 512
    for j in range(2 * _RET_NK // step):
        y = _dot(h, w_ref[0, :, j * step:(j + 1) * step])
        if j * step >= _RET_NK:
            y = y * (RET_DK ** -0.5)
        qk_ref[:, j * step:(j + 1) * step] = y
    for j in range(_RET_NV // step):
        o = 2 * _RET_NK + j * step
        v_ref[:, j * step:(j + 1) * step] = _dot(h, w_ref[0, :, o:o + step]).astype(BF16)
    for j in range(2 * _RET_NV // step):
        o = 2 * _RET_NK + _RET_NV + j * step
        gate_ref[:, j * step:(j + 1) * step] = _dot(h, w_ref[0, :, o:o + step])


def _ret_proj(x, mods, norm_g, w_in, layer, latent, seq):
    m = x.shape[0]
    tm = 256
    i_odd = layer // 2
    tok = lambda i: (i, 0)
    n_in = 2 * _RET_NK + 3 * _RET_NV
    return pl.pallas_call(
        _ret_proj_kernel,
        grid=(m // tm,),
        in_specs=[pl.BlockSpec((tm, D_MODEL), tok),
                  pl.BlockSpec((1, 1, N_MOD * D_MODEL), _mod_index(layer, latent, tm, seq)),
                  pl.BlockSpec((1, 1, D_MODEL), lambda i: (layer * 3 + 1, 0, 0)),
                  _resident((1, D_MODEL, n_in), lambda i: (i_odd, 0, 0))],
        out_specs=[pl.BlockSpec((tm, 2 * _RET_NK), tok),
                   pl.BlockSpec((tm, _RET_NV), tok),
                   pl.BlockSpec((tm, 2 * _RET_NV), tok)],
        out_shape=[jax.ShapeDtypeStruct((m, 2 * _RET_NK), F32),
                   jax.ShapeDtypeStruct((m, _RET_NV), BF16),
                   jax.ShapeDtypeStruct((m, 2 * _RET_NV), F32)],
        compiler_params=_cparams(1),
        name="ret_proj",
    )(x, mods, norm_g, w_in)


def _ret_rope_tables(seq):
    d = RET_DK // 2
    half = d // 2
    t = jnp.arange(seq)
    j = np.arange(RET_DK)
    pos = jnp.where((j // d)[None, :] == 0, (t // GRID_W)[:, None], (t % GRID_W)[:, None])
    inv = ROPE_BASE ** (-jnp.arange(half, dtype=F32) / half)
    ang = pos.astype(F32) * inv[j % half][None, :]
    sign = np.where((j % d) < half, -1.0, 1.0).astype(np.float32)
    return jnp.cos(ang), jnp.sin(ang) * sign[None, :]


def _head_norm(o):
    mu = jnp.mean(o, axis=-1, keepdims=True)
    d = o - mu
    return d * lax.rsqrt(jnp.mean(d * d, axis=-1, keepdims=True) + EPS)


def _ret_kernel(*refs, seq, latent):
    ld_ref, q_ref, k_ref, v_ref, gf_ref, gb_ref = refs[:6]
    if latent:
        cos_ref, sin_ref, s0_ref, y_ref, s_scr, of_scr = refs[6:]
    else:
        y_ref, st_ref, of_scr = refs[6:]
    hd = pl.program_id(1)
    c = RET_CHUNK
    n_chunks = seq // c
    assert latent or n_chunks == 1
    ii = lax.broadcasted_iota(jnp.int32, (c, c), 0)
    jj = lax.broadcasted_iota(jnp.int32, (c, c), 1)
    row = lax.broadcasted_iota(jnp.int32, (c, 1), 0).astype(F32)

    def rope(x, off):
        if not latent:
            return x
        cos = cos_ref[pl.ds(off, c), :]
        sin = sin_ref[pl.ds(off, c), :]
        parts = [pltpu.roll(x[:, g * LANES:(g + 1) * LANES], LANES // 2, 1) for g in range(RET_DK // LANES)]
        return x * cos + jnp.concatenate(parts, axis=1) * sin

    for dirn in range(2):
        lg = -jnp.abs(ld_ref[dirn, hd])
        diff = (ii - jj if dirn == 0 else jj - ii).astype(F32)
        intra = jnp.where(diff >= 0, jnp.exp(lg * jnp.maximum(diff, 0.0)), 0.0)
        if dirn == 0:
            q_dec = jnp.exp(lg * (row + 1.0))
            k_dec = jnp.exp(lg * (c - 1.0 - row))
        else:
            q_dec = jnp.exp(lg * (c - row))
            k_dec = jnp.exp(lg * row)
        c_dec = jnp.exp(lg * float(c))
        g_ref = gf_ref if dirn == 0 else gb_ref

        def chunk(off, dirn=dirn, intra=intra, q_dec=q_dec, k_dec=k_dec, c_dec=c_dec, g_ref=g_ref):
            q = rope(q_ref[pl.ds(off, c), :], off)
            k = rope(k_ref[pl.ds(off, c), :], off)
            v = v_ref[pl.ds(off, c), :]
            qb = q.astype(BF16)
            a = _dot_nt(qb, k.astype(BF16)) * intra
            o = _dot(a.astype(BF16), v)
            kd_t = (k * k_dec).T.astype(BF16)
            if latent:
                s_prev = s_scr[...]
                o = o + _dot(qb, s_prev.astype(BF16)) * q_dec
                s_scr[...] = s_prev * c_dec + _dot(kd_t, v)
            else:
                st_ref[0, dirn, 0] = _dot(kd_t, v)
            y = _silu(g_ref[pl.ds(off, c), :]) * _head_norm(o)
            if dirn == 0:
                of_scr[pl.ds(off, c), :] = y
            else:
                y_ref[pl.ds(off, c), :] = (of_scr[pl.ds(off, c), :] + y).astype(BF16)

        if latent:
            s_scr[...] = s0_ref[0, 0, dirn, 0]

            def body(t, carry, dirn=dirn, chunk=chunk):
                ci = t if dirn == 0 else n_chunks - 1 - t
                chunk(pl.multiple_of(ci * c, c))
                return carry

            lax.fori_loop(0, n_chunks, body, 0)
        else:
            chunk(0)


def _retention(qk, v, gates, log_decay, i_odd, seq, latent, rope=None, state=None):
    m = qk.shape[0]
    nb = m // seq
    in_specs = [pl.BlockSpec(memory_space=pltpu.SMEM),
                pl.BlockSpec((seq, RET_DK), lambda b, h: (b, h)),
                pl.BlockSpec((seq, RET_DK), lambda b, h: (b, RET_HEADS + h)),
                pl.BlockSpec((seq, RET_DV), lambda b, h: (b, h)),
                pl.BlockSpec((seq, RET_DV), lambda b, h: (b, h)),
                pl.BlockSpec((seq, RET_DV), lambda b, h: (b, RET_HEADS + h))]
    args = [log_decay[i_odd], qk, qk, v, gates, gates]
    y_shape = jax.ShapeDtypeStruct((m, _RET_NV), BF16)
    y_spec = pl.BlockSpec((seq, RET_DV), lambda b, h: (b, h))
    scratch = [pltpu.VMEM((seq, RET_DV), F32)]
    if latent:
        in_specs += [_resident((seq, RET_DK), lambda b, h: (0, 0)),
                     _resident((seq, RET_DK), lambda b, h: (0, 0)),
                     pl.BlockSpec((1, 1, 2, 1, RET_DK, RET_DV), lambda b, h: (b, i_odd, 0, h, 0, 0))]
        args += [rope[0], rope[1], state]
        out_shape, out_specs = y_shape, y_spec
        scratch = [pltpu.VMEM((RET_DK, RET_DV), F32)] + scratch
    else:
        out_shape = [y_shape, jax.ShapeDtypeStruct((nb, 2, RET_HEADS, RET_DK, RET_DV), F32)]
        out_specs = [y_spec, pl.BlockSpec((1, 2, 1, RET_DK, RET_DV), lambda b, h: (b, 0, h, 0, 0))]
    return pl.pallas_call(
        functools.partial(_ret_kernel, seq=seq, latent=latent),
        grid=(nb, RET_HEADS),
        in_specs=in_specs,
        out_specs=out_specs,
        out_shape=out_shape,
        scratch_shapes=scratch,
        compiler_params=_cparams(2),
        name="retention_latent" if latent else "retention_context",
    )(*args)


def kernel(x_prompt, x_sample, c, cache_na_k, cache_na_v, cache_mla_ckv, cache_mla_krope, state_ret,
           c_ctx, norm_g, ada_w, ada_b, ffn_w13, ffn_w2, mix_w_in, mla_q_norm, mla_kv_norm, mla_w_uq,
           mla_w_ukv, na_rpb, mix_w_out, ret_w_in, ret_log_decay, ret_w_out, final_norm_g):
    batch, seq, _ = x_prompt.shape
    dec_batch, dec_seq, _ = x_sample.shape
    past = cache_na_k.shape[2]
    n_even = mix_w_in.shape[0]
    assert dec_batch + 1 <= MOD_ROWS and seq == RET_CHUNK and dec_seq % RET_CHUNK == 0
    assert (dec_seq // GRID_W) % NA_QROWS == 0 and dec_seq // GRID_W >= NA_KROWS

    w13 = ffn_w13.astype(BF16)
    w2 = ffn_w2.astype(BF16)
    w_out_even = mix_w_out.astype(BF16)
    w_in_ret = ret_w_in.astype(BF16)
    w_out_ret = ret_w_out.astype(BF16)
    packed = _pack_even_weights(mix_w_in, mla_w_uq, mla_w_ukv)
    q_norm = mla_q_norm.reshape(n_even, 1, MLA_Q_RANK)
    kv_norm = mla_kv_norm.reshape(n_even, 1, MLA_KV_RANK)
    norm_g3 = norm_g.reshape(DEPTH * 3, 1, D_MODEL)
    mla_rope = _mla_rope_tables(dec_seq)
    ret_rope = _ret_rope_tables(dec_seq)
    cache_k = cache_na_k.reshape(dec_batch, n_even, past, NA_WIDTH)
    cache_v = cache_na_v.reshape(dec_batch, n_even, past, NA_WIDTH)

    cvec = jnp.concatenate([c_ctx[None, :], c, jnp.zeros((MOD_ROWS - 1 - dec_batch, D_MODEL), F32)], axis=0)
    mods = _ada_all(cvec, ada_w, ada_b).reshape(DEPTH * MOD_ROWS, 1, N_MOD * D_MODEL)

    xp = x_prompt.reshape(batch * seq, D_MODEL)
    xs = x_sample.reshape(dec_batch * dec_seq, D_MODEL)
    new_k, new_v, new_ckv, new_kr, new_ret = [], [], [], [], []
    for layer in range(DEPTH):
        i = layer // 2
        xp = _ffn(xp, mods, norm_g3, w13, w2, layer, 0, False, seq)
        xs = _ffn(xs, mods, norm_g3, w13, w2, layer, 0, True, dec_seq)
        if layer % 2 == 0:
            qa, ka, va, qm, km, vm, ckv, kr = _even_proj(
                xp, mods, norm_g3, packed, q_norm, kv_norm, layer, False, seq)
            new_k.append(ka)
            new_v.append(va)
            new_ckv.append(ckv)
            new_kr.append(kr)
            op = _ctx_attention(qa, ka, va, qm, km, vm, seq)
            xp = _out_proj(xp, mods, [op], [(w_out_even, NA_WIDTH + MLA_OUT, 0)], i, layer, False, seq)

            qa, ka, va, qm, km, vm = _even_proj(
                xs, mods, norm_g3, packed, q_norm, kv_norm, layer, True, dec_seq, rope=mla_rope)
            kr_pad = jnp.pad(cache_mla_krope[:, i].reshape(dec_batch * past, MLA_ROPE_DIM),
                             ((0, 0), (MLA_NOPE_DIM, LANES - MLA_NOPE_DIM - MLA_ROPE_DIM)))
            km_ctx, vm_ctx = _mla_cache_keys(cache_mla_ckv, kr_pad, packed[3], packed[4], i)
            bias = _na_bias(na_rpb[i], dec_seq // GRID_W)
            oa = _latent_na_attention(qa, ka, va, cache_k, cache_v, bias, i, dec_seq)
            ob = _latent_mla_attention(qm, km_ctx, km, vm_ctx, vm, dec_seq, past)
            xs = _out_proj(xs, mods, [oa, ob],
                           [(w_out_even, NA_WIDTH, 0), (w_out_even, MLA_OUT, NA_WIDTH // MLA_OUT)],
                           i, layer, True, dec_seq)
        else:
            qk, v, gates = _ret_proj(xp, mods, norm_g3, w_in_ret, layer, False, seq)
            yp, st = _retention(qk, v, gates, ret_log_decay, i, seq, False)
            new_ret.append(st)
            xp = _out_proj(xp, mods, [yp], [(w_out_ret, _RET_NV, 0)], i, layer, False, seq)

            qk, v, gates = _ret_proj(xs, mods, norm_g3, w_in_ret, layer, True, dec_seq)
            ys = _retention(qk, v, gates, ret_log_decay, i, dec_seq, True, rope=ret_rope, state=state_ret)
            xs = _out_proj(xs, mods, [ys], [(w_out_ret, _RET_NV, 0)], i, layer, True, dec_seq)
        xp = _ffn(xp, mods, norm_g3, w13, w2, layer, 1, False, seq)
        xs = _ffn(xs, mods, norm_g3, w13, w2, layer, 1, True, dec_seq)

    y_prompt = _final_norm(xp, final_norm_g).reshape(batch, seq, D_MODEL)
    y_sample = _final_norm(xs, final_norm_g).reshape(dec_batch, dec_seq, D_MODEL)
    new_na_k = jnp.concatenate(new_k, axis=1).reshape(batch, n_even, seq, NA_HEADS, NA_HEAD_DIM)
    new_na_v = jnp.concatenate(new_v, axis=1).reshape(batch, n_even, seq, NA_HEADS, NA_HEAD_DIM)
    new_mla_ckv = jnp.concatenate(new_ckv, axis=1)
    new_mla_krope = jnp.concatenate(new_kr, axis=1)
    new_state_ret = jnp.stack(new_ret, axis=1)
    return (y_prompt, y_sample, new_na_k, new_na_v, new_mla_ckv, new_mla_krope, new_state_ret)
```

```python
import functools

import jax
import jax.numpy as jnp
import numpy as np
from jax import lax
from jax.experimental import pallas as pl
from jax.experimental.pallas import tpu as pltpu

F32 = jnp.float32
BF16 = jnp.bfloat16

D_MODEL = 1024
DEPTH = 4
GRID_W = 64
NA_HEADS = 8
NA_HEAD_DIM = 64
NA_ROWS = 8
NA_COLS = 16
MLA_HEADS = 8
MLA_Q_RANK = 384
MLA_KV_RANK = 256
MLA_NOPE_DIM = 64
MLA_ROPE_DIM = 32
MLA_V_DIM = 64
RET_HEADS = 4
RET_DK = D_MODEL // RET_HEADS
RET_DV = 2 * D_MODEL // RET_HEADS
FFN_DIM = ((8 * D_MODEL // 3 + 127) // 128) * 128
N_MOD = 9
ROPE_BASE = 10000.0
EPS = 1e-6
NA_WIDTH = NA_HEADS * NA_HEAD_DIM
MLA_OUT = MLA_HEADS * MLA_V_DIM

LANES = 128
MOD_ROWS = 8
NEG_BIG = -1e30
VMEM_LIMIT = 52 * 1024 * 1024

FFN_TF = 256
RET_CHUNK = 256
NA_QROWS = 8
NA_KROWS = 16
ATT_QS = 128
ATT_KB = 256


def _cparams(n_axes):
    return pltpu.CompilerParams(dimension_semantics=("arbitrary",) * n_axes,
                                vmem_limit_bytes=VMEM_LIMIT)


def _resident(block, index_map):
    return pl.BlockSpec(block, index_map, pipeline_mode=pl.Buffered(1))


def _dot(a, b):
    return jnp.dot(a, b, preferred_element_type=F32)


def _dot_nt(a, b):
    return lax.dot_general(a, b, (((1,), (1,)), ((), ())), preferred_element_type=F32)


def _silu(x):
    return x * jax.nn.sigmoid(x)


def _rms(x, g):
    return x * lax.rsqrt(jnp.mean(x * x, axis=-1, keepdims=True) + EPS) * g


def _modulated(x, mod_ref, g_ref, s):
    shift = mod_ref[0, :, 3 * s * D_MODEL:(3 * s + 1) * D_MODEL]
    scale = mod_ref[0, :, (3 * s + 1) * D_MODEL:(3 * s + 2) * D_MODEL]
    return _rms(x, g_ref[0]) * (1.0 + scale) + shift


def _gate(mod_ref, s):
    return mod_ref[0, :, (3 * s + 2) * D_MODEL:(3 * s + 3) * D_MODEL]


def _mod_index(layer, latent, tile_rows, seq):
    if not latent:
        return lambda i: (layer * MOD_ROWS, 0, 0)
    return lambda i: (layer * MOD_ROWS + 1 + (i * tile_rows) // seq, 0, 0)


def _ada_kernel(c_ref, w_ref, b_ref, o_ref):
    s = _silu(c_ref[...]).astype(BF16)
    o_ref[0] = _dot(s, w_ref[0].astype(BF16)) + b_ref[0]


def _ada_all(cvec, ada_w, ada_b):
    tn = 1024
    n = N_MOD * D_MODEL
    return pl.pallas_call(
        _ada_kernel,
        grid=(DEPTH, n // tn),
        in_specs=[pl.BlockSpec((MOD_ROWS, D_MODEL), lambda l, j: (0, 0)),
                  pl.BlockSpec((1, D_MODEL, tn), lambda l, j: (l, 0, j)),
                  pl.BlockSpec((1, 1, tn), lambda l, j: (l, 0, j))],
        out_specs=pl.BlockSpec((1, MOD_ROWS, tn), lambda l, j: (l, 0, j)),
        out_shape=jax.ShapeDtypeStruct((DEPTH, MOD_ROWS, n), F32),
        compiler_params=_cparams(2),
        name="adaln",
    )(cvec, ada_w, ada_b.reshape(DEPTH, 1, n))


def _ffn_kernel(*refs, s, n_pre, final):
    x_ref, mod_ref, g_ref, w13_ref, w2_ref = refs[:5]
    pre_y = refs[5:5 + n_pre]
    pre_w = refs[5 + n_pre:5 + 2 * n_pre]
    fg_ref = refs[5 + 2 * n_pre] if final else None
    o_ref = refs[-1]
    x = x_ref[...]
    if n_pre:
        mix = None
        for y_ref, w_ref in zip(pre_y, pre_w):
            y = _dot(y_ref[...], w_ref[0])
            mix = y if mix is None else mix + y
        x = x + _gate(mod_ref, 1) * mix
    h = _modulated(x, mod_ref, g_ref, s).astype(BF16)
    acc = jnp.zeros(x.shape, F32)
    for j in range(FFN_DIM // FFN_TF):
        a = _dot(h, w13_ref[0, 0, :, j * FFN_TF:(j + 1) * FFN_TF])
        b = _dot(h, w13_ref[0, 0, :, FFN_DIM + j * FFN_TF:FFN_DIM + (j + 1) * FFN_TF])
        u = (_silu(a) * b).astype(BF16)
        acc = acc + _dot(u, w2_ref[0, 0, j * FFN_TF:(j + 1) * FFN_TF, :])
    out = x + 0.5 * _gate(mod_ref, s) * acc
    if final:
        out = _rms(out, fg_ref[...])
    o_ref[...] = out


def _ffn(x, mods, norm_g, w13, w2, layer, which, latent, seq, pre=None, final_g=None):
    tm = 512
    m = x.shape[0]
    s = 0 if which == 0 else 2
    in_specs = [pl.BlockSpec((tm, D_MODEL), lambda i: (i, 0)),
                pl.BlockSpec((1, 1, N_MOD * D_MODEL), _mod_index(layer, latent, tm, seq)),
                pl.BlockSpec((1, 1, D_MODEL), lambda i: (layer * 3 + s, 0, 0)),
                _resident((1, 1, D_MODEL, 2 * FFN_DIM), lambda i: (layer, which, 0, 0)),
                _resident((1, 1, FFN_DIM, D_MODEL), lambda i: (layer, which, 0, 0))]
    args = [x, mods, norm_g, w13, w2]
    n_pre = 0
    if pre is not None:
        ys, ws, w_layer = pre
        n_pre = len(ys)
        for y in ys:
            in_specs.append(pl.BlockSpec((tm, y.shape[1]), lambda i: (i, 0)))
        for w, rows, blk in ws:
            in_specs.append(_resident((1, rows, D_MODEL), lambda i, blk=blk: (w_layer, blk, 0)))
        args += list(ys) + [w for w, _, _ in ws]
    if final_g is not None:
        in_specs.append(pl.BlockSpec((1, D_MODEL), lambda i: (0, 0)))
        args.append(final_g.reshape(1, D_MODEL))
    return pl.pallas_call(
        functools.partial(_ffn_kernel, s=s, n_pre=n_pre, final=final_g is not None),
        grid=(m // tm,),
        in_specs=in_specs,
        out_specs=pl.BlockSpec((tm, D_MODEL), lambda i: (i, 0)),
        out_shape=jax.ShapeDtypeStruct((m, D_MODEL), F32),
        compiler_params=_cparams(1),
        name="ffn",
    )(*args)


_EV_QA = 0
_EV_KA = NA_WIDTH
_EV_VA = 2 * NA_WIDTH
_EV_CQ = 3 * NA_WIDTH
_EV_CKV = _EV_CQ + MLA_Q_RANK
_EV_KR = _EV_CKV + MLA_KV_RANK
_EV_KRS = _EV_KR + LANES
_EV_COLS = _EV_KRS + LANES
MLA_QK_W = MLA_HEADS * LANES
MLA_SCALE = (MLA_NOPE_DIM + MLA_ROPE_DIM) ** -0.5


def _rope_swap_perm(width, half):
    j = np.arange(width)
    return np.where((j % (2 * half)) < half, j + half, j - half)


def _pack_even_weights(mix_w_in, mla_w_uq, mla_w_ukv):
    n_even = mix_w_in.shape[0]
    o = 3 * NA_WIDTH + MLA_Q_RANK + MLA_KV_RANK
    w_kr = mix_w_in[:, :, o:o + MLA_ROPE_DIM]
    w_krs = w_kr[:, :, _rope_swap_perm(MLA_ROPE_DIM, MLA_ROPE_DIM // 4)]
    pad = ((0, 0), (0, 0), (MLA_NOPE_DIM, LANES - MLA_NOPE_DIM - MLA_ROPE_DIM))
    w_in = jnp.concatenate([mix_w_in[:, :, :o], jnp.pad(w_kr, pad), jnp.pad(w_krs, pad)], axis=-1)

    uq = mla_w_uq.reshape(n_even, MLA_Q_RANK, MLA_HEADS, MLA_NOPE_DIM + MLA_ROPE_DIM)
    uq_rope = uq[..., MLA_NOPE_DIM:]
    uq_rope_s = uq_rope[..., _rope_swap_perm(MLA_ROPE_DIM, MLA_ROPE_DIM // 4)]
    zpad = LANES - MLA_NOPE_DIM - MLA_ROPE_DIM
    w_uq = jnp.pad(uq, ((0, 0), (0, 0), (0, 0), (0, zpad))).reshape(n_even, MLA_Q_RANK, MLA_QK_W)
    w_uqs = jnp.pad(uq_rope_s, ((0, 0), (0, 0), (0, 0), (MLA_NOPE_DIM, zpad)))
    w_uqs = w_uqs.reshape(n_even, MLA_Q_RANK, MLA_QK_W)

    ukv = mla_w_ukv.reshape(n_even, MLA_KV_RANK, MLA_HEADS, MLA_NOPE_DIM + MLA_V_DIM)
    w_uk = jnp.pad(ukv[..., :MLA_NOPE_DIM], ((0, 0), (0, 0), (0, 0), (0, LANES - MLA_NOPE_DIM)))
    w_uk = w_uk.reshape(n_even, MLA_KV_RANK, MLA_QK_W)
    w_uv = ukv[..., MLA_NOPE_DIM:].reshape(n_even, MLA_KV_RANK, MLA_OUT)
    return (w_in.astype(BF16), w_uq.astype(BF16), w_uqs.astype(BF16), w_uk.astype(BF16),
            w_uv.astype(BF16))


def _mla_rope_tables(seq):
    half = MLA_ROPE_DIM // 4
    t = jnp.arange(seq)
    j = np.arange(MLA_ROPE_DIM)
    pos = jnp.where((j // (2 * half))[None, :] == 0, (t // GRID_W)[:, None], (t % GRID_W)[:, None])
    inv = ROPE_BASE ** (-jnp.arange(half, dtype=F32) / half)
    ang = pos.astype(F32) * inv[j % half][None, :]
    sign = np.where((j % (2 * half)) < half, -1.0, 1.0).astype(np.float32)
    cos = jnp.concatenate([jnp.ones((seq, MLA_NOPE_DIM), F32), jnp.cos(ang),
                           jnp.ones((seq, LANES - MLA_NOPE_DIM - MLA_ROPE_DIM), F32)], axis=1)
    sin = jnp.concatenate([jnp.zeros((seq, MLA_NOPE_DIM), F32), jnp.sin(ang) * sign[None, :],
                           jnp.zeros((seq, LANES - MLA_NOPE_DIM - MLA_ROPE_DIM), F32)], axis=1)
    return cos, sin


def _even_proj_kernel(*refs, latent, n_alias):
    (x_ref, mod_ref, g_ref, win_ref, qn_ref, kvn_ref, wuq_ref, wuqs_ref, wuk_ref, wuv_ref) = refs[:10]
    if latent:
        cos_ref, sin_ref = refs[10:12]
        qa_ref, ka_ref, va_ref, qm_ref, km_ref, vm_ref = refs[12:]
    else:
        qa_ref, ka_ref, va_ref, qm_ref, km_ref, vm_ref, ckv_ref, kr_ref = refs[10 + n_alias:]

    def w_in(lo, width):
        return win_ref[0, :, lo:lo + width]

    h = _modulated(x_ref[...], mod_ref, g_ref, 1).astype(BF16)
    qa_ref[...] = (_dot(h, w_in(_EV_QA, NA_WIDTH)) * (NA_HEAD_DIM ** -0.5)).astype(qa_ref.dtype)
    ka = _dot(h, w_in(_EV_KA, NA_WIDTH))
    va = _dot(h, w_in(_EV_VA, NA_WIDTH))
    if latent:
        ka_ref[...] = ka.astype(BF16)
        va_ref[...] = va.astype(BF16)
    else:
        ka_ref[:, 0] = ka.reshape(ka_ref.shape[0], ka_ref.shape[2], NA_WIDTH)
        va_ref[:, 0] = va.reshape(va_ref.shape[0], va_ref.shape[2], NA_WIDTH)

    cqn = _rms(_dot(h, w_in(_EV_CQ, MLA_Q_RANK)), qn_ref[0]).astype(BF16)
    ckvn = _rms(_dot(h, w_in(_EV_CKV, MLA_KV_RANK)), kvn_ref[0])
    kr = _dot(h, w_in(_EV_KR, LANES))
    if latent:
        cos = cos_ref[...]
        sin = sin_ref[...]
        kr = kr * cos + _dot(h, w_in(_EV_KRS, LANES)) * sin
    else:
        ckv_ref[:, 0] = ckvn.reshape(ckv_ref.shape[0], ckv_ref.shape[2], MLA_KV_RANK)
        kr_ref[:, 0] = kr[:, MLA_NOPE_DIM:MLA_NOPE_DIM + MLA_ROPE_DIM].reshape(
            kr_ref.shape[0], kr_ref.shape[2], MLA_ROPE_DIM)
    ckvb = ckvn.astype(BF16)

    qm = _dot(cqn, wuq_ref[0])
    km = _dot(ckvb, wuk_ref[0])
    if latent:
        qms = _dot(cqn, wuqs_ref[0])
    for hd in range(MLA_HEADS):
        sl = slice(hd * LANES, (hd + 1) * LANES)
        q_h = qm[:, sl]
        if latent:
            q_h = q_h * cos + qms[:, sl] * sin
        qm_ref[:, sl] = (q_h * MLA_SCALE).astype(BF16)
        km_ref[:, sl] = (km[:, sl] + kr).astype(BF16)
    vm_ref[...] = _dot(ckvb, wuv_ref[0]).astype(BF16)


def _even_proj(x, mods, norm_g, packed, q_norm, kv_norm, layer, latent, seq, n_even, rope=None, caches=None):
    w_in, w_uq, w_uqs, w_uk, w_uv = packed
    i_even = layer // 2
    m = x.shape[0]
    tm = 512
    bt = tm // seq if not latent else 0
    tok = lambda i: (i, 0)
    in_specs = [pl.BlockSpec((tm, D_MODEL), tok),
                pl.BlockSpec((1, 1, N_MOD * D_MODEL), _mod_index(layer, latent, tm, seq)),
                pl.BlockSpec((1, 1, D_MODEL), lambda i: (layer * 3 + 1, 0, 0)),
                _resident((1, D_MODEL, _EV_COLS), lambda i: (i_even, 0, 0)),
                pl.BlockSpec((1, 1, MLA_Q_RANK), lambda i: (i_even, 0, 0)),
                pl.BlockSpec((1, 1, MLA_KV_RANK), lambda i: (i_even, 0, 0)),
                _resident((1, MLA_Q_RANK, MLA_QK_W), lambda i: (i_even, 0, 0)),
                _resident((1, MLA_Q_RANK, MLA_QK_W), lambda i: (i_even, 0, 0)),
                _resident((1, MLA_KV_RANK, MLA_QK_W), lambda i: (i_even, 0, 0)),
                _resident((1, MLA_KV_RANK, MLA_OUT), lambda i: (i_even, 0, 0))]
    args = [x, mods, norm_g, w_in, q_norm, kv_norm, w_uq, w_uqs, w_uk, w_uv]
    bf = lambda w: jax.ShapeDtypeStruct((m, w), BF16)
    aliases = {}
    if latent:
        nt = seq // tm
        in_specs += [pl.BlockSpec((tm, LANES), lambda i: (i % nt, 0))] * 2
        args += list(rope)
        out_shape = [bf(NA_WIDTH), bf(NA_WIDTH), bf(NA_WIDTH), bf(MLA_QK_W), bf(MLA_QK_W), bf(MLA_OUT)]
        out_specs = [pl.BlockSpec((tm, s.shape[1]), tok) for s in out_shape]
    else:
        nb = m // seq
        cache = lambda w: jax.ShapeDtypeStruct((nb, n_even, seq, w), F32)
        cspec = lambda w: pl.BlockSpec((bt, 1, seq, w), lambda i: (i, i_even, 0, 0))
        out_shape = [bf(NA_WIDTH), cache(NA_WIDTH), cache(NA_WIDTH), bf(MLA_QK_W), bf(MLA_QK_W),
                     bf(MLA_OUT), cache(MLA_KV_RANK), cache(MLA_ROPE_DIM)]
        out_specs = [pl.BlockSpec((tm, NA_WIDTH), tok), cspec(NA_WIDTH), cspec(NA_WIDTH),
                     pl.BlockSpec((tm, MLA_QK_W), tok), pl.BlockSpec((tm, MLA_QK_W), tok),
                     pl.BlockSpec((tm, MLA_OUT), tok), cspec(MLA_KV_RANK), cspec(MLA_ROPE_DIM)]
        if caches is not None:
            for arr, out_idx in zip(caches, (1, 2, 6, 7)):
                aliases[len(args)] = out_idx
                in_specs.append(pl.BlockSpec(memory_space=pl.ANY))
                args.append(arr)
    return pl.pallas_call(
        functools.partial(_even_proj_kernel, latent=latent, n_alias=len(aliases)),
        grid=(m // tm,),
        in_specs=in_specs,
        out_specs=out_specs,
        out_shape=out_shape,
        input_output_aliases=aliases,
        compiler_params=_cparams(1),
        name="even_proj_latent" if latent else "even_proj_context",
    )(*args)


def _mla_cache_kernel(ckv_ref, kr_ref, wuk_ref, wuv_ref, km_ref, vm_ref):
    ckvb = ckv_ref[0, 0].astype(BF16)
    km = _dot(ckvb, wuk_ref[0])
    kr = kr_ref[...]
    for hd in range(MLA_HEADS):
        sl = slice(hd * LANES, (hd + 1) * LANES)
        km_ref[:, sl] = (km[:, sl] + kr).astype(BF16)
    vm_ref[...] = _dot(ckvb, wuv_ref[0]).astype(BF16)


def _mla_cache_keys(cache_ckv, kr_padded, w_uk, w_uv, i_even):
    nb, _, past, _ = cache_ckv.shape
    return pl.pallas_call(
        _mla_cache_kernel,
        grid=(nb,),
        in_specs=[pl.BlockSpec((1, 1, past, MLA_KV_RANK), lambda b: (b, i_even, 0, 0)),
                  pl.BlockSpec((past, LANES), lambda b: (b, 0)),
                  _resident((1, MLA_KV_RANK, MLA_QK_W), lambda b: (i_even, 0, 0)),
                  _resident((1, MLA_KV_RANK, MLA_OUT), lambda b: (i_even, 0, 0))],
        out_specs=[pl.BlockSpec((past, MLA_QK_W), lambda b: (b, 0)),
                   pl.BlockSpec((past, MLA_OUT), lambda b: (b, 0))],
        out_shape=[jax.ShapeDtypeStruct((nb * past, MLA_QK_W), BF16),
                   jax.ShapeDtypeStruct((nb * past, MLA_OUT), BF16)],
        compiler_params=_cparams(1),
        name="mla_cache_keys",
    )(cache_ckv, kr_padded, w_uk, w_uv)


def _half_masks():
    lane = lax.broadcasted_iota(jnp.int32, (1, LANES), 1)
    lo = lane < (LANES // 2)
    return lo, jnp.logical_not(lo)


def _blocked_attention(q, key_blocks, value_blocks, bias_blocks, s_ref):
    mx = None
    for j, (k_blk, bias) in enumerate(zip(key_blocks, bias_blocks)):
        sc = _dot_nt(q, k_blk())
        if bias is not None:
            sc = sc + bias()
        s_ref[:, j * ATT_KB:(j + 1) * ATT_KB] = sc
        for g in range(ATT_KB // LANES):
            part = sc[:, g * LANES:(g + 1) * LANES]
            mx = part if mx is None else jnp.maximum(mx, part)
    m = jnp.max(mx, axis=-1, keepdims=True)
    ls = None
    acc = None
    for j, v_blk in enumerate(value_blocks):
        p = jnp.exp(s_ref[:, j * ATT_KB:(j + 1) * ATT_KB] - m)
        for g in range(ATT_KB // LANES):
            part = p[:, g * LANES:(g + 1) * LANES]
            ls = part if ls is None else ls + part
        o = _dot(p.astype(BF16), v_blk())
        acc = o if acc is None else acc + o
    return acc / jnp.sum(ls, axis=-1, keepdims=True)


def _rows(ref, start, size, lanes):
    return lambda: ref[start:start + size, lanes]


def _ctx_attn_kernel(qa_ref, ka_ref, va_ref, qm_ref, km_ref, vm_ref, o_ref, s_scr, kv_scr):
    lo, hi = _half_masks()
    zero = jnp.zeros((), BF16)
    seq = qa_ref.shape[0]
    kv_scr[0] = ka_ref[0, 0].astype(BF16)
    kv_scr[1] = va_ref[0, 0].astype(BF16)
    slot = 0
    for g in range((NA_WIDTH + MLA_OUT) // LANES):
        mla = g >= NA_WIDTH // LANES
        gl = g - NA_WIDTH // LANES if mla else g
        sl = slice(gl * LANES, (gl + 1) * LANES)
        for qs in range(seq // ATT_QS):
            rows = slice(qs * ATT_QS, (qs + 1) * ATT_QS)
            outs = []
            for t, msk in enumerate((lo, hi)):
                if mla:
                    hsl = slice((2 * gl + t) * LANES, (2 * gl + t + 1) * LANES)
                    q = qm_ref[rows, hsl]
                    kb = [_rows(km_ref, j * ATT_KB, ATT_KB, hsl) for j in range(seq // ATT_KB)]
                    vb = [_rows(vm_ref, j * ATT_KB, ATT_KB, sl) for j in range(seq // ATT_KB)]
                else:
                    q = jnp.where(msk, qa_ref[rows, sl], zero)
                    kb = [_rows(kv_scr.at[0], j * ATT_KB, ATT_KB, sl) for j in range(seq // ATT_KB)]
                    vb = [_rows(kv_scr.at[1], j * ATT_KB, ATT_KB, sl) for j in range(seq // ATT_KB)]
                outs.append(_blocked_attention(q, kb, vb, [None] * len(kb), s_scr.at[slot]))
                slot = 1 - slot
            o_ref[rows, g * LANES:(g + 1) * LANES] = jnp.where(lo, outs[0], outs[1]).astype(BF16)


def _ctx_attention(qa, ka, va, qm, km, vm, seq, i_even):
    m = qa.shape[0]
    nb = m // seq
    tok = lambda b: (b, 0)
    cache = lambda b: (b, i_even, 0, 0)
    return pl.pallas_call(
        _ctx_attn_kernel,
        grid=(nb,),
        in_specs=[pl.BlockSpec((seq, NA_WIDTH), tok),
                  pl.BlockSpec((1, 1, seq, NA_WIDTH), cache),
                  pl.BlockSpec((1, 1, seq, NA_WIDTH), cache),
                  pl.BlockSpec((seq, MLA_QK_W), tok),
                  pl.BlockSpec((seq, MLA_QK_W), tok),
                  pl.BlockSpec((seq, MLA_OUT), tok)],
        out_specs=pl.BlockSpec((seq, NA_WIDTH + MLA_OUT), tok),
        out_shape=jax.ShapeDtypeStruct((m, NA_WIDTH + MLA_OUT), BF16),
        scratch_shapes=[pltpu.VMEM((2, ATT_QS, seq), F32), pltpu.VMEM((2, seq, NA_WIDTH), BF16)],
        compiler_params=_cparams(1),
        name="context_attention",
    )(qa, ka, va, qm, km, vm)


def _lat_mla_kernel(q_ref, kc_ref, kl_ref, vc_ref, vl_ref, o_ref, s_scr):
    lo, _ = _half_masks()
    past, seq = kc_ref.shape[0], kl_ref.shape[0]
    all_lanes = slice(0, LANES)
    slot = 0
    for qs in range(q_ref.shape[0] // ATT_QS):
        rows = slice(qs * ATT_QS, (qs + 1) * ATT_QS)
        outs = []
        for t in range(2):
            sl = slice(t * LANES, (t + 1) * LANES)
            kb = ([_rows(kc_ref, j * ATT_KB, ATT_KB, sl) for j in range(past // ATT_KB)]
                  + [_rows(kl_ref, j * ATT_KB, ATT_KB, sl) for j in range(seq // ATT_KB)])
            vb = ([_rows(vc_ref, j * ATT_KB, ATT_KB, all_lanes) for j in range(past // ATT_KB)]
                  + [_rows(vl_ref, j * ATT_KB, ATT_KB, all_lanes) for j in range(seq // ATT_KB)])
            outs.append(_blocked_attention(q_ref[rows, sl], kb, vb, [None] * len(kb), s_scr.at[slot]))
            slot = 1 - slot
        o_ref[rows, :] = jnp.where(lo, outs[0], outs[1]).astype(BF16)


def _latent_mla_attention(qm, km_ctx, km_lat, vm_ctx, vm_lat, seq, past):
    m = qm.shape[0]
    nb = m // seq
    tq = 512
    nq = seq // tq
    pair = 2 * LANES
    return pl.pallas_call(
        _lat_mla_kernel,
        grid=(nb, MLA_HEADS // 2, nq),
        in_specs=[pl.BlockSpec((tq, pair), lambda b, g, i: (b * nq + i, g)),
                  pl.BlockSpec((past, pair), lambda b, g, i: (b, g)),
                  pl.BlockSpec((seq, pair), lambda b, g, i: (b, g)),
                  pl.BlockSpec((past, LANES), lambda b, g, i: (b, g)),
                  pl.BlockSpec((seq, LANES), lambda b, g, i: (b, g))],
        out_specs=pl.BlockSpec((tq, LANES), lambda b, g, i: (b * nq + i, g)),
        out_shape=jax.ShapeDtypeStruct((m, MLA_OUT), BF16),
        scratch_shapes=[pltpu.VMEM((2, ATT_QS, past + seq), F32)],
        compiler_params=_cparams(3),
        name="latent_mla_attention",
    )(qm, km_ctx, km_lat, vm_ctx, vm_lat)


def _na_key_start(row_block, rows):
    return np.clip(row_block * NA_QROWS - NA_ROWS // 2, 0, rows - NA_KROWS)


def _na_bias_kernel(rpb_ref, o_ref, *, rows):
    hd = pl.program_id(0)
    kr = min(NA_ROWS, rows)
    n_dc = 2 * NA_COLS - 1
    c = lax.broadcasted_iota(jnp.int32, (GRID_W, LANES), 0)
    lane = lax.broadcasted_iota(jnp.int32, (GRID_W, LANES), 1)
    n = lane % GRID_W
    c0 = jnp.clip(c - NA_COLS // 2, 0, GRID_W - NA_COLS)
    in_win = (n >= c0) & (n < c0 + NA_COLS)
    dc = n - c + NA_COLS - 1
    left = lane < GRID_W
    neg = jnp.full((GRID_W, LANES), NEG_BIG, F32)
    col_tables = []
    for dr in range(2 * NA_ROWS - 1):
        t = neg
        for j in range(n_dc):
            t = jnp.where(dc == j, rpb_ref[hd * (2 * NA_ROWS - 1) * n_dc + dr * n_dc + j], t)
        col_tables.append(jnp.where(in_win, t, neg))
    for rb in range(rows // NA_QROWS):
        start = int(_na_key_start(rb, rows))
        for rl in range(NA_QROWS):
            r = rb * NA_QROWS + rl
            r0 = min(max(r - kr // 2, 0), rows - kr)
            for pr in range(NA_KROWS // 2):
                halves = []
                for key_row in (start + 2 * pr, start + 2 * pr + 1):
                    ok = r0 <= key_row < r0 + kr
                    halves.append(col_tables[key_row - r + NA_ROWS - 1] if ok else neg)
                o_ref[rb, 0, rl * GRID_W:(rl + 1) * GRID_W, pr * LANES:(pr + 1) * LANES] = (
                    jnp.where(left, halves[0], halves[1]))


def _na_bias(rpb, rows):
    n_rb = rows // NA_QROWS
    flat = rpb.reshape(-1)
    return pl.pallas_call(
        functools.partial(_na_bias_kernel, rows=rows),
        grid=(NA_HEADS,),
        in_specs=[pl.BlockSpec(memory_space=pltpu.SMEM)],
        out_specs=pl.BlockSpec((n_rb, 1, NA_QROWS * GRID_W, NA_KROWS * GRID_W), lambda h: (0, h, 0, 0)),
        out_shape=jax.ShapeDtypeStruct((n_rb, NA_HEADS, NA_QROWS * GRID_W, NA_KROWS * GRID_W), F32),
        compiler_params=_cparams(1),
        name="na_bias",
    )(flat)


def _lat_na_kernel(q_ref, k_ref, v_ref, kc_ref, vc_ref, bias_ref, o_ref, s_scr, kv_scr, *, rows):
    lo, hi = _half_masks()
    zero = jnp.zeros((), BF16)
    rb = pl.program_id(0)
    start = jnp.clip(rb * NA_QROWS - NA_ROWS // 2, 0, rows - NA_KROWS) * GRID_W
    start = pl.multiple_of(start, GRID_W * (NA_ROWS // 2))
    nk = NA_KROWS * GRID_W
    past = kc_ref.shape[2]
    all_lanes = slice(0, LANES)
    kv_scr[0] = kc_ref[0, 0].astype(BF16)
    kv_scr[1] = vc_ref[0, 0].astype(BF16)

    def win(ref, j):
        return lambda: ref[pl.ds(start + j * ATT_KB, ATT_KB), :]

    kb = ([win(k_ref, j) for j in range(nk // ATT_KB)]
          + [_rows(kv_scr.at[0], j * ATT_KB, ATT_KB, all_lanes) for j in range(past // ATT_KB)])
    vb = ([win(v_ref, j) for j in range(nk // ATT_KB)]
          + [_rows(kv_scr.at[1], j * ATT_KB, ATT_KB, all_lanes) for j in range(past // ATT_KB)])
    slot = 0
    for qs in range(q_ref.shape[0] // ATT_QS):
        qrows = slice(qs * ATT_QS, (qs + 1) * ATT_QS)
        outs = []
        for t, msk in enumerate((lo, hi)):
            bias = ([(lambda t=t, j=j: bias_ref[0, t, qrows, j * ATT_KB:(j + 1) * ATT_KB])
                     for j in range(nk // ATT_KB)] + [None] * (past // ATT_KB))
            q = jnp.where(msk, q_ref[qrows, :], zero)
            outs.append(_blocked_attention(q, kb, vb, bias, s_scr.at[slot]))
            slot = 1 - slot
        o_ref[qrows, :] = jnp.where(lo, outs[0], outs[1]).astype(BF16)


def _latent_na_attention(qa, ka, va, cache_k, cache_v, bias, i_even, seq):
    m = qa.shape[0]
    nb = m // seq
    rows = seq // GRID_W
    n_rb = rows // NA_QROWS
    tq = NA_QROWS * GRID_W
    past = cache_k.shape[2]
    return pl.pallas_call(
        functools.partial(_lat_na_kernel, rows=rows),
        grid=(n_rb, NA_HEADS // 2, nb),
        in_specs=[pl.BlockSpec((tq, LANES), lambda r, g, b: (b * n_rb + r, g)),
                  pl.BlockSpec((seq, LANES), lambda r, g, b: (b, g)),
                  pl.BlockSpec((seq, LANES), lambda r, g, b: (b, g)),
                  pl.BlockSpec((1, 1, past, LANES), lambda r, g, b: (b, i_even, 0, g)),
                  pl.BlockSpec((1, 1, past, LANES), lambda r, g, b: (b, i_even, 0, g)),
                  pl.BlockSpec((1, 2, tq, NA_KROWS * GRID_W), lambda r, g, b: (r, g, 0, 0))],
        out_specs=pl.BlockSpec((tq, LANES), lambda r, g, b: (b * n_rb + r, g)),
        out_shape=jax.ShapeDtypeStruct((m, NA_WIDTH), BF16),
        scratch_shapes=[pltpu.VMEM((2, ATT_QS, NA_KROWS * GRID_W + past), F32),
                        pltpu.VMEM((2, past, LANES), BF16)],
        compiler_params=_cparams(3),
        name="latent_na_attention",
    )(qa, ka, va, cache_k, cache_v, bias)


_RET_NK = RET_HEADS * RET_DK
_RET_NV = RET_HEADS * RET_DV


def _ret_proj_kernel(*refs, latent):
    x_ref, mod_ref, g_ref, w_ref = refs[:4]
    if latent:
        cos_ref, sin_ref = refs[4:6]
    q_ref, k_ref, v_ref, gate_ref = refs[-4:]
    h = _modulated(x_ref[...], mod_ref, g_ref, 1).astype(BF16)

    def rope(y):
        if not latent:
            return y
        parts = [pltpu.roll(y[:, g * LANES:(g + 1) * LANES], LANES // 2, 1) for g in range(RET_DK // LANES)]
        return y * cos_ref[...] + jnp.concatenate(parts, axis=1) * sin_ref[...]

    for hd in range(RET_HEADS):
        sl = slice(hd * RET_DK, (hd + 1) * RET_DK)
        q_ref[:, sl] = rope(_dot(h, w_ref[0, :, sl])).astype(BF16)
        k = _dot(h, w_ref[0, :, _RET_NK + hd * RET_DK:_RET_NK + (hd + 1) * RET_DK]) * (RET_DK ** -0.5)
        k_ref[:, sl] = rope(k)
    step = 512
    for j in range(_RET_NV // step):
        o = 2 * _RET_NK + j * step
        v_ref[:, j * step:(j + 1) * step] = _dot(h, w_ref[0, :, o:o + step]).astype(BF16)
    for j in range(2 * _RET_NV // step):
        o = 2 * _RET_NK + _RET_NV + j * step
        gate_ref[:, j * step:(j + 1) * step] = _dot(h, w_ref[0, :, o:o + step])


def _ret_proj(x, mods, norm_g, w_in, layer, latent, seq, rope=None):
    m = x.shape[0]
    tm = 256
    i_odd = layer // 2
    tok = lambda i: (i, 0)
    n_in = 2 * _RET_NK + 3 * _RET_NV
    in_specs = [pl.BlockSpec((tm, D_MODEL), tok),
                pl.BlockSpec((1, 1, N_MOD * D_MODEL), _mod_index(layer, latent, tm, seq)),
                pl.BlockSpec((1, 1, D_MODEL), lambda i: (layer * 3 + 1, 0, 0)),
                _resident((1, D_MODEL, n_in), lambda i: (i_odd, 0, 0))]
    args = [x, mods, norm_g, w_in]
    if latent:
        nt = seq // tm
        in_specs += [pl.BlockSpec((tm, RET_DK), lambda i: (i % nt, 0))] * 2
        args += list(rope)
    return pl.pallas_call(
        functools.partial(_ret_proj_kernel, latent=latent),
        grid=(m // tm,),
        in_specs=in_specs,
        out_specs=[pl.BlockSpec((tm, _RET_NK), tok),
                   pl.BlockSpec((tm, _RET_NK), tok),
                   pl.BlockSpec((tm, _RET_NV), tok),
                   pl.BlockSpec((tm, 2 * _RET_NV), tok)],
        out_shape=[jax.ShapeDtypeStruct((m, _RET_NK), BF16),
                   jax.ShapeDtypeStruct((m, _RET_NK), F32),
                   jax.ShapeDtypeStruct((m, _RET_NV), BF16),
                   jax.ShapeDtypeStruct((m, 2 * _RET_NV), F32)],
        compiler_params=_cparams(1),
        name="ret_proj_latent" if latent else "ret_proj_context",
    )(*args)


def _ret_rope_tables(seq):
    d = RET_DK // 2
    half = d // 2
    t = jnp.arange(seq)
    j = np.arange(RET_DK)
    pos = jnp.where((j // d)[None, :] == 0, (t // GRID_W)[:, None], (t % GRID_W)[:, None])
    inv = ROPE_BASE ** (-jnp.arange(half, dtype=F32) / half)
    ang = pos.astype(F32) * inv[j % half][None, :]
    sign = np.where((j % d) < half, -1.0, 1.0).astype(np.float32)
    return jnp.cos(ang), jnp.sin(ang) * sign[None, :]


def _head_norm(o):
    mu = jnp.mean(o, axis=-1, keepdims=True)
    d = o - mu
    return d * lax.rsqrt(jnp.mean(d * d, axis=-1, keepdims=True) + EPS)


def _ret_kernel(*refs, seq, latent, n_alias):
    ld_ref, q_ref, k_ref, v_ref, gf_ref, gb_ref = refs[:6]
    if latent:
        s0_ref, y_ref, s_scr, of_scr = refs[6:]
    else:
        y_ref, st_ref, of_scr = refs[6 + n_alias:]
    c = RET_CHUNK
    n_chunks = seq // c
    assert latent or n_chunks == 1
    ii = lax.broadcasted_iota(jnp.int32, (c, c), 0)
    jj = lax.broadcasted_iota(jnp.int32, (c, c), 1)
    row = lax.broadcasted_iota(jnp.int32, (c, 1), 0).astype(F32)

    for hh in range(1 if latent else RET_HEADS):
        hd = pl.program_id(1) if latent else hh
        ksl = slice(hh * RET_DK, (hh + 1) * RET_DK)
        vsl = slice(hh * RET_DV, (hh + 1) * RET_DV)
        for dirn in range(2):
            lg = -jnp.abs(ld_ref[dirn, hd])
            diff = (ii - jj if dirn == 0 else jj - ii).astype(F32)
            intra = jnp.where(diff >= 0, jnp.exp(lg * jnp.maximum(diff, 0.0)), 0.0)
            if dirn == 0:
                q_dec = jnp.exp(lg * (row + 1.0))
                k_dec = jnp.exp(lg * (c - 1.0 - row))
            else:
                q_dec = jnp.exp(lg * (c - row))
                k_dec = jnp.exp(lg * row)
            c_dec = jnp.exp(lg * float(c))
            g_ref = gf_ref if dirn == 0 else gb_ref

            def chunk(off, hh=hh, dirn=dirn, ksl=ksl, vsl=vsl, intra=intra, q_dec=q_dec, k_dec=k_dec,
                      c_dec=c_dec, g_ref=g_ref):
                rows = pl.ds(off, c)
                qb = q_ref[rows, ksl]
                k = k_ref[rows, ksl]
                v = v_ref[rows, vsl]
                a = _dot_nt(qb, k.astype(BF16)) * intra
                o = _dot(a.astype(BF16), v)
                kd_t = (k * k_dec).T.astype(BF16)
                if latent:
                    s_prev = s_scr[...]
                    o = o + _dot(qb, s_prev.astype(BF16)) * q_dec
                    s_scr[...] = s_prev * c_dec + _dot(kd_t, v)
                else:
                    st_ref[0, 0, dirn, hh] = _dot(kd_t, v)
                y = _silu(g_ref[rows, vsl]) * _head_norm(o)
                if dirn == 0:
                    of_scr[rows, vsl] = y
                else:
                    y_ref[rows, vsl] = (of_scr[rows, vsl] + y).astype(BF16)

            if latent:
                s_scr[...] = s0_ref[0, 0, dirn, 0]

                def body(t, carry, dirn=dirn, chunk=chunk):
                    ci = t if dirn == 0 else n_chunks - 1 - t
                    chunk(pl.multiple_of(ci * c, c))
                    return carry

                lax.fori_loop(0, n_chunks, body, 0)
            else:
                chunk(0)


def _retention(q, k, v, gates, log_decay, i_odd, seq, latent, n_odd, state=None, prev_states=None):
    m = q.shape[0]
    nb = m // seq
    y_shape = jax.ShapeDtypeStruct((m, _RET_NV), BF16)
    aliases = {}
    if latent:
        grid = (nb, RET_HEADS)
        in_specs = [pl.BlockSpec(memory_space=pltpu.SMEM),
                    pl.BlockSpec((seq, RET_DK), lambda b, h: (b, h)),
                    pl.BlockSpec((seq, RET_DK), lambda b, h: (b, h)),
                    pl.BlockSpec((seq, RET_DV), lambda b, h: (b, h)),
                    pl.BlockSpec((seq, RET_DV), lambda b, h: (b, h)),
                    pl.BlockSpec((seq, RET_DV), lambda b, h: (b, RET_HEADS + h)),
                    pl.BlockSpec((1, 1, 2, 1, RET_DK, RET_DV), lambda b, h: (b, i_odd, 0, h, 0, 0))]
        args = [log_decay[i_odd], q, k, v, gates, gates, state]
        out_shape = y_shape
        out_specs = pl.BlockSpec((seq, RET_DV), lambda b, h: (b, h))
        scratch = [pltpu.VMEM((RET_DK, RET_DV), F32), pltpu.VMEM((seq, RET_DV), F32)]
    else:
        grid = (nb,)
        in_specs = [pl.BlockSpec(memory_space=pltpu.SMEM),
                    pl.BlockSpec((seq, _RET_NK), lambda b: (b, 0)),
                    pl.BlockSpec((seq, _RET_NK), lambda b: (b, 0)),
                    pl.BlockSpec((seq, _RET_NV), lambda b: (b, 0)),
                    pl.BlockSpec((seq, _RET_NV), lambda b: (b, 0)),
                    pl.BlockSpec((seq, _RET_NV), lambda b: (b, 1))]
        args = [log_decay[i_odd], q, k, v, gates, gates]
        if prev_states is not None:
            aliases[len(args)] = 1
            in_specs.append(pl.BlockSpec(memory_space=pl.ANY))
            args.append(prev_states)
        out_shape = [y_shape, jax.ShapeDtypeStruct((nb, n_odd, 2, RET_HEADS, RET_DK, RET_DV), F32)]
        out_specs = [pl.BlockSpec((seq, _RET_NV), lambda b: (b, 0)),
                     pl.BlockSpec((1, 1, 2, RET_HEADS, RET_DK, RET_DV), lambda b: (b, i_odd, 0, 0, 0, 0))]
        scratch = [pltpu.VMEM((seq, _RET_NV), F32)]
    return pl.pallas_call(
        functools.partial(_ret_kernel, seq=seq, latent=latent, n_alias=len(aliases)),
        grid=grid,
        in_specs=in_specs,
        out_specs=out_specs,
        out_shape=out_shape,
        scratch_shapes=scratch,
        input_output_aliases=aliases,
        compiler_params=_cparams(len(grid)),
        name="retention_latent" if latent else "retention_context",
    )(*args)


def kernel(x_prompt, x_sample, c, cache_na_k, cache_na_v, cache_mla_ckv, cache_mla_krope, state_ret,
           c_ctx, norm_g, ada_w, ada_b, ffn_w13, ffn_w2, mix_w_in, mla_q_norm, mla_kv_norm, mla_w_uq,
           mla_w_ukv, na_rpb, mix_w_out, ret_w_in, ret_log_decay, ret_w_out, final_norm_g):
    batch, seq, _ = x_prompt.shape
    dec_batch, dec_seq, _ = x_sample.shape
    past = cache_na_k.shape[2]
    n_even = mix_w_in.shape[0]
    assert dec_batch + 1 <= MOD_ROWS and seq == RET_CHUNK and dec_seq % RET_CHUNK == 0
    assert (dec_seq // GRID_W) % NA_QROWS == 0 and dec_seq // GRID_W >= NA_KROWS

    w13 = ffn_w13.astype(BF16)
    w2 = ffn_w2.astype(BF16)
    w_out_even = mix_w_out.astype(BF16)
    w_in_ret = ret_w_in.astype(BF16)
    w_out_ret = ret_w_out.astype(BF16)
    packed = _pack_even_weights(mix_w_in, mla_w_uq, mla_w_ukv)
    q_norm = mla_q_norm.reshape(n_even, 1, MLA_Q_RANK)
    kv_norm = mla_kv_norm.reshape(n_even, 1, MLA_KV_RANK)
    norm_g3 = norm_g.reshape(DEPTH * 3, 1, D_MODEL)
    mla_rope = _mla_rope_tables(dec_seq)
    ret_rope = _ret_rope_tables(dec_seq)
    cache_k = cache_na_k.reshape(dec_batch, n_even, past, NA_WIDTH)
    cache_v = cache_na_v.reshape(dec_batch, n_even, past, NA_WIDTH)

    cvec = jnp.concatenate([c_ctx[None, :], c, jnp.zeros((MOD_ROWS - 1 - dec_batch, D_MODEL), F32)], axis=0)
    mods = _ada_all(cvec, ada_w, ada_b).reshape(DEPTH * MOD_ROWS, 1, N_MOD * D_MODEL)

    xp = x_prompt.reshape(batch * seq, D_MODEL)
    xs = x_sample.reshape(dec_batch * dec_seq, D_MODEL)
    n_odd = ret_w_in.shape[0]
    caches = None
    states = None
    for layer in range(DEPTH):
        i = layer // 2
        xp = _ffn(xp, mods, norm_g3, w13, w2, layer, 0, False, seq)
        xs = _ffn(xs, mods, norm_g3, w13, w2, layer, 0, True, dec_seq)
        if layer % 2 == 0:
            qa, ka, va, qm, km, vm, ckv, kr = _even_proj(
                xp, mods, norm_g3, packed, q_norm, kv_norm, layer, False, seq, n_even, caches=caches)
            caches = (ka, va, ckv, kr)
            op = _ctx_attention(qa, ka, va, qm, km, vm, seq, i)
            pre_p = ([op], [(w_out_even, NA_WIDTH + MLA_OUT, 0)], i)

            qa, ka, va, qm, km, vm = _even_proj(
                xs, mods, norm_g3, packed, q_norm, kv_norm, layer, True, dec_seq, n_even, rope=mla_rope)
            kr_pad = jnp.pad(cache_mla_krope[:, i].reshape(dec_batch * past, MLA_ROPE_DIM),
                             ((0, 0), (MLA_NOPE_DIM, LANES - MLA_NOPE_DIM - MLA_ROPE_DIM)))
            km_ctx, vm_ctx = _mla_cache_keys(cache_mla_ckv, kr_pad, packed[3], packed[4], i)
            bias = _na_bias(na_rpb[i], dec_seq // GRID_W)
            oa = _latent_na_attention(qa, ka, va, cache_k, cache_v, bias, i, dec_seq)
            ob = _latent_mla_attention(qm, km_ctx, km, vm_ctx, vm, dec_seq, past)
            pre_s = ([oa, ob], [(w_out_even, NA_WIDTH, 0), (w_out_even, MLA_OUT, NA_WIDTH // MLA_OUT)], i)
        else:
            q, k, v, gates = _ret_proj(xp, mods, norm_g3, w_in_ret, layer, False, seq)
            yp, states = _retention(q, k, v, gates, ret_log_decay, i, seq, False, n_odd, prev_states=states)
            pre_p = ([yp], [(w_out_ret, _RET_NV, 0)], i)

            q, k, v, gates = _ret_proj(xs, mods, norm_g3, w_in_ret, layer, True, dec_seq, rope=ret_rope)
            ys = _retention(q, k, v, gates, ret_log_decay, i, dec_seq, True, n_odd, state=state_ret)
            pre_s = ([ys], [(w_out_ret, _RET_NV, 0)], i)
        last = final_norm_g if layer == DEPTH - 1 else None
        xp = _ffn(xp, mods, norm_g3, w13, w2, layer, 1, False, seq, pre=pre_p, final_g=last)
        xs = _ffn(xs, mods, norm_g3, w13, w2, layer, 1, True, dec_seq, pre=pre_s, final_g=last)

    y_prompt = xp.reshape(batch, seq, D_MODEL)
    y_sample = xs.reshape(dec_batch, dec_seq, D_MODEL)
    ka, va, ckv, kr = caches
    new_na_k = ka.reshape(batch, n_even, seq, NA_HEADS, NA_HEAD_DIM)
    new_na_v = va.reshape(batch, n_even, seq, NA_HEADS, NA_HEAD_DIM)
    return (y_prompt, y_sample, new_na_k, new_na_v, ckv, kr, states)
```

```python
import functools

import jax
import jax.numpy as jnp
import numpy as np
from jax import lax
from jax.experimental import pallas as pl
from jax.experimental.pallas import tpu as pltpu

F32 = jnp.float32
BF16 = jnp.bfloat16

D_MODEL = 1024
DEPTH = 4
GRID_W = 64
NA_HEADS = 8
NA_HEAD_DIM = 64
NA_ROWS = 8
NA_COLS = 16
MLA_HEADS = 8
MLA_Q_RANK = 384
MLA_KV_RANK = 256
MLA_NOPE_DIM = 64
MLA_ROPE_DIM = 32
MLA_V_DIM = 64
RET_HEADS = 4
RET_DK = D_MODEL // RET_HEADS
RET_DV = 2 * D_MODEL // RET_HEADS
FFN_DIM = ((8 * D_MODEL // 3 + 127) // 128) * 128
N_MOD = 9
ROPE_BASE = 10000.0
EPS = 1e-6
NA_WIDTH = NA_HEADS * NA_HEAD_DIM
MLA_OUT = MLA_HEADS * MLA_V_DIM

LANES = 128
MOD_ROWS = 8
NEG_BIG = -1e30
VMEM_LIMIT = 52 * 1024 * 1024

FFN_TF = 256
RET_CHUNK = 256
NA_QROWS = 8
NA_KROWS = 16
ATT_QS = 128
ATT_KB = 256
ATT_TQ = 512


def _cparams(n_axes):
    return pltpu.CompilerParams(dimension_semantics=("arbitrary",) * n_axes,
                                vmem_limit_bytes=VMEM_LIMIT)


def _resident(block, index_map):
    return pl.BlockSpec(block, index_map, pipeline_mode=pl.Buffered(1))


def _dot(a, b):
    return jnp.dot(a, b, preferred_element_type=F32)


def _dot_nt(a, b):
    return lax.dot_general(a, b, (((1,), (1,)), ((), ())), preferred_element_type=F32)


def _silu(x):
    return x * jax.nn.sigmoid(x)


def _rms(x, g):
    return x * lax.rsqrt(jnp.mean(x * x, axis=-1, keepdims=True) + EPS) * g


def _modulated(x, mod_ref, g_ref, s):
    shift = mod_ref[0, :, 3 * s * D_MODEL:(3 * s + 1) * D_MODEL]
    scale = mod_ref[0, :, (3 * s + 1) * D_MODEL:(3 * s + 2) * D_MODEL]
    return _rms(x, g_ref[0]) * (1.0 + scale) + shift


def _gate(mod_ref, s):
    return mod_ref[0, :, (3 * s + 2) * D_MODEL:(3 * s + 3) * D_MODEL]


def _mod_index(layer, latent, tile_rows, seq):
    if not latent:
        return lambda i: (layer * MOD_ROWS, 0, 0)
    return lambda i: (layer * MOD_ROWS + 1 + (i * tile_rows) // seq, 0, 0)


def _ada_kernel(c_ref, w_ref, b_ref, o_ref):
    s = _silu(c_ref[...]).astype(BF16)
    o_ref[0] = _dot(s, w_ref[0].astype(BF16)) + b_ref[0]


def _ada_all(cvec, ada_w, ada_b):
    tn = 1024
    n = N_MOD * D_MODEL
    return pl.pallas_call(
        _ada_kernel,
        grid=(DEPTH, n // tn),
        in_specs=[pl.BlockSpec((MOD_ROWS, D_MODEL), lambda l, j: (0, 0)),
                  pl.BlockSpec((1, D_MODEL, tn), lambda l, j: (l, 0, j)),
                  pl.BlockSpec((1, 1, tn), lambda l, j: (l, 0, j))],
        out_specs=pl.BlockSpec((1, MOD_ROWS, tn), lambda l, j: (l, 0, j)),
        out_shape=jax.ShapeDtypeStruct((DEPTH, MOD_ROWS, n), F32),
        compiler_params=_cparams(2),
        name="adaln",
    )(cvec, ada_w, ada_b.reshape(DEPTH, 1, n))


def _ffn_kernel(*refs, s, n_pre, final):
    x_ref, mod_ref, g_ref, w13_ref, w2_ref = refs[:5]
    pre_y = refs[5:5 + n_pre]
    pre_w = refs[5 + n_pre:5 + 2 * n_pre]
    fg_ref = refs[5 + 2 * n_pre] if final else None
    o_ref = refs[-1]
    x = x_ref[...]
    if n_pre:
        mix = None
        for y_ref, w_ref in zip(pre_y, pre_w):
            y = _dot(y_ref[...], w_ref[0])
            mix = y if mix is None else mix + y
        x = x + _gate(mod_ref, 1) * mix
    h = _modulated(x, mod_ref, g_ref, s).astype(BF16)
    acc = jnp.zeros(x.shape, F32)
    for j in range(FFN_DIM // FFN_TF):
        a = _dot(h, w13_ref[0, 0, :, j * FFN_TF:(j + 1) * FFN_TF])
        b = _dot(h, w13_ref[0, 0, :, FFN_DIM + j * FFN_TF:FFN_DIM + (j + 1) * FFN_TF])
        u = (_silu(a) * b).astype(BF16)
        acc = acc + _dot(u, w2_ref[0, 0, j * FFN_TF:(j + 1) * FFN_TF, :])
    out = x + 0.5 * _gate(mod_ref, s) * acc
    if final:
        out = _rms(out, fg_ref[...])
    o_ref[...] = out


def _ffn(x, mods, norm_g, w13, w2, layer, which, latent, seq, pre=None, final_g=None):
    tm = 512
    m = x.shape[0]
    s = 0 if which == 0 else 2
    in_specs = [pl.BlockSpec((tm, D_MODEL), lambda i: (i, 0)),
                pl.BlockSpec((1, 1, N_MOD * D_MODEL), _mod_index(layer, latent, tm, seq)),
                pl.BlockSpec((1, 1, D_MODEL), lambda i: (layer * 3 + s, 0, 0)),
                _resident((1, 1, D_MODEL, 2 * FFN_DIM), lambda i: (layer, which, 0, 0)),
                _resident((1, 1, FFN_DIM, D_MODEL), lambda i: (layer, which, 0, 0))]
    args = [x, mods, norm_g, w13, w2]
    n_pre = 0
    if pre is not None:
        ys, ws, w_layer = pre
        n_pre = len(ys)
        for y in ys:
            in_specs.append(pl.BlockSpec((tm, y.shape[1]), lambda i: (i, 0)))
        for w, rows, blk in ws:
            in_specs.append(_resident((1, rows, D_MODEL), lambda i, blk=blk: (w_layer, blk, 0)))
        args += list(ys) + [w for w, _, _ in ws]
    if final_g is not None:
        in_specs.append(pl.BlockSpec((1, D_MODEL), lambda i: (0, 0)))
        args.append(final_g.reshape(1, D_MODEL))
    return pl.pallas_call(
        functools.partial(_ffn_kernel, s=s, n_pre=n_pre, final=final_g is not None),
        grid=(m // tm,),
        in_specs=in_specs,
        out_specs=pl.BlockSpec((tm, D_MODEL), lambda i: (i, 0)),
        out_shape=jax.ShapeDtypeStruct((m, D_MODEL), F32),
        compiler_params=_cparams(1),
        name="ffn",
    )(*args)


_EV_QA = 0
_EV_KA = NA_WIDTH
_EV_VA = 2 * NA_WIDTH
_EV_CQ = 3 * NA_WIDTH
_EV_CKV = _EV_CQ + MLA_Q_RANK
_EV_KR = _EV_CKV + MLA_KV_RANK
_EV_KRS = _EV_KR + LANES
_EV_COLS = _EV_KRS + LANES
MLA_QK_W = MLA_HEADS * LANES
MLA_SCALE = (MLA_NOPE_DIM + MLA_ROPE_DIM) ** -0.5


def _rope_swap_perm(width, half):
    j = np.arange(width)
    return np.where((j % (2 * half)) < half, j + half, j - half)


def _pack_even_weights(mix_w_in, mla_w_uq, mla_w_ukv):
    n_even = mix_w_in.shape[0]
    o = 3 * NA_WIDTH + MLA_Q_RANK + MLA_KV_RANK
    w_kr = mix_w_in[:, :, o:o + MLA_ROPE_DIM]
    w_krs = w_kr[:, :, _rope_swap_perm(MLA_ROPE_DIM, MLA_ROPE_DIM // 4)]
    pad = ((0, 0), (0, 0), (MLA_NOPE_DIM, LANES - MLA_NOPE_DIM - MLA_ROPE_DIM))
    w_in = jnp.concatenate([mix_w_in[:, :, :o], jnp.pad(w_kr, pad), jnp.pad(w_krs, pad)], axis=-1)

    uq = mla_w_uq.reshape(n_even, MLA_Q_RANK, MLA_HEADS, MLA_NOPE_DIM + MLA_ROPE_DIM)
    uq_rope = uq[..., MLA_NOPE_DIM:]
    uq_rope_s = uq_rope[..., _rope_swap_perm(MLA_ROPE_DIM, MLA_ROPE_DIM // 4)]
    zpad = LANES - MLA_NOPE_DIM - MLA_ROPE_DIM
    w_uq = jnp.pad(uq, ((0, 0), (0, 0), (0, 0), (0, zpad))).reshape(n_even, MLA_Q_RANK, MLA_QK_W)
    w_uqs = jnp.pad(uq_rope_s, ((0, 0), (0, 0), (0, 0), (MLA_NOPE_DIM, zpad)))
    w_uqs = w_uqs.reshape(n_even, MLA_Q_RANK, MLA_QK_W)

    ukv = mla_w_ukv.reshape(n_even, MLA_KV_RANK, MLA_HEADS, MLA_NOPE_DIM + MLA_V_DIM)
    w_uk = jnp.pad(ukv[..., :MLA_NOPE_DIM], ((0, 0), (0, 0), (0, 0), (0, LANES - MLA_NOPE_DIM)))
    w_uk = w_uk.reshape(n_even, MLA_KV_RANK, MLA_QK_W)
    w_uv = ukv[..., MLA_NOPE_DIM:].reshape(n_even, MLA_KV_RANK, MLA_OUT)
    return (w_in.astype(BF16), w_uq.astype(BF16), w_uqs.astype(BF16), w_uk.astype(BF16),
            w_uv.astype(BF16))


def _mla_rope_tables(seq):
    half = MLA_ROPE_DIM // 4
    t = jnp.arange(seq)
    j = np.arange(MLA_ROPE_DIM)
    pos = jnp.where((j // (2 * half))[None, :] == 0, (t // GRID_W)[:, None], (t % GRID_W)[:, None])
    inv = ROPE_BASE ** (-jnp.arange(half, dtype=F32) / half)
    ang = pos.astype(F32) * inv[j % half][None, :]
    sign = np.where((j % (2 * half)) < half, -1.0, 1.0).astype(np.float32)
    cos = jnp.concatenate([jnp.ones((seq, MLA_NOPE_DIM), F32), jnp.cos(ang),
                           jnp.ones((seq, LANES - MLA_NOPE_DIM - MLA_ROPE_DIM), F32)], axis=1)
    sin = jnp.concatenate([jnp.zeros((seq, MLA_NOPE_DIM), F32), jnp.sin(ang) * sign[None, :],
                           jnp.zeros((seq, LANES - MLA_NOPE_DIM - MLA_ROPE_DIM), F32)], axis=1)
    return cos, sin


def _even_proj_kernel(*refs, latent, n_alias):
    (x_ref, mod_ref, g_ref, win_ref, qn_ref, kvn_ref, wuq_ref, wuqs_ref, wuk_ref, wuv_ref) = refs[:10]
    if latent:
        cos_ref, sin_ref = refs[10:12]
        qa_ref, ka_ref, va_ref, qm_ref, km_ref, vm_ref = refs[12:]
    else:
        qa_ref, ka_ref, va_ref, qm_ref, km_ref, vm_ref, ckv_ref, kr_ref = refs[10 + n_alias:]

    def w_in(lo, width):
        return win_ref[0, :, lo:lo + width]

    h = _modulated(x_ref[...], mod_ref, g_ref, 1).astype(BF16)
    qa_ref[...] = (_dot(h, w_in(_EV_QA, NA_WIDTH)) * (NA_HEAD_DIM ** -0.5)).astype(qa_ref.dtype)
    ka = _dot(h, w_in(_EV_KA, NA_WIDTH))
    va = _dot(h, w_in(_EV_VA, NA_WIDTH))
    if latent:
        ka_ref[...] = ka.astype(BF16)
        va_ref[...] = va.astype(BF16)
    else:
        ka_ref[:, 0] = ka.reshape(ka_ref.shape[0], ka_ref.shape[2], NA_WIDTH)
        va_ref[:, 0] = va.reshape(va_ref.shape[0], va_ref.shape[2], NA_WIDTH)

    cqn = _rms(_dot(h, w_in(_EV_CQ, MLA_Q_RANK)), qn_ref[0]).astype(BF16)
    ckvn = _rms(_dot(h, w_in(_EV_CKV, MLA_KV_RANK)), kvn_ref[0])
    kr = _dot(h, w_in(_EV_KR, LANES))
    if latent:
        cos = cos_ref[...]
        sin = sin_ref[...]
        kr = kr * cos + _dot(h, w_in(_EV_KRS, LANES)) * sin
    else:
        ckv_ref[:, 0] = ckvn.reshape(ckv_ref.shape[0], ckv_ref.shape[2], MLA_KV_RANK)
        kr_ref[:, 0] = kr[:, MLA_NOPE_DIM:MLA_NOPE_DIM + MLA_ROPE_DIM].reshape(
            kr_ref.shape[0], kr_ref.shape[2], MLA_ROPE_DIM)
    ckvb = ckvn.astype(BF16)

    qm = _dot(cqn, wuq_ref[0])
    km = _dot(ckvb, wuk_ref[0])
    if latent:
        qms = _dot(cqn, wuqs_ref[0])
    for hd in range(MLA_HEADS):
        sl = slice(hd * LANES, (hd + 1) * LANES)
        q_h = qm[:, sl]
        if latent:
            q_h = q_h * cos + qms[:, sl] * sin
        qm_ref[:, sl] = (q_h * MLA_SCALE).astype(BF16)
        km_ref[:, sl] = (km[:, sl] + kr).astype(BF16)
    vm_ref[...] = _dot(ckvb, wuv_ref[0]).astype(BF16)


def _even_proj(x, mods, norm_g, packed, q_norm, kv_norm, layer, latent, seq, n_even, rope=None, caches=None):
    w_in, w_uq, w_uqs, w_uk, w_uv = packed
    i_even = layer // 2
    m = x.shape[0]
    tm = 512
    bt = tm // seq if not latent else 0
    tok = lambda i: (i, 0)
    in_specs = [pl.BlockSpec((tm, D_MODEL), tok),
                pl.BlockSpec((1, 1, N_MOD * D_MODEL), _mod_index(layer, latent, tm, seq)),
                pl.BlockSpec((1, 1, D_MODEL), lambda i: (layer * 3 + 1, 0, 0)),
                _resident((1, D_MODEL, _EV_COLS), lambda i: (i_even, 0, 0)),
                pl.BlockSpec((1, 1, MLA_Q_RANK), lambda i: (i_even, 0, 0)),
                pl.BlockSpec((1, 1, MLA_KV_RANK), lambda i: (i_even, 0, 0)),
                _resident((1, MLA_Q_RANK, MLA_QK_W), lambda i: (i_even, 0, 0)),
                _resident((1, MLA_Q_RANK, MLA_QK_W), lambda i: (i_even, 0, 0)),
                _resident((1, MLA_KV_RANK, MLA_QK_W), lambda i: (i_even, 0, 0)),
                _resident((1, MLA_KV_RANK, MLA_OUT), lambda i: (i_even, 0, 0))]
    args = [x, mods, norm_g, w_in, q_norm, kv_norm, w_uq, w_uqs, w_uk, w_uv]
    bf = lambda w: jax.ShapeDtypeStruct((m, w), BF16)
    aliases = {}
    if latent:
        nt = seq // tm
        in_specs += [pl.BlockSpec((tm, LANES), lambda i: (i % nt, 0))] * 2
        args += list(rope)
        out_shape = [bf(NA_WIDTH), bf(NA_WIDTH), bf(NA_WIDTH), bf(MLA_QK_W), bf(MLA_QK_W), bf(MLA_OUT)]
        out_specs = [pl.BlockSpec((tm, s.shape[1]), tok) for s in out_shape]
    else:
        nb = m // seq
        cache = lambda w: jax.ShapeDtypeStruct((nb, n_even, seq, w), F32)
        cspec = lambda w: pl.BlockSpec((bt, 1, seq, w), lambda i: (i, i_even, 0, 0))
        out_shape = [bf(NA_WIDTH), cache(NA_WIDTH), cache(NA_WIDTH), bf(MLA_QK_W), bf(MLA_QK_W),
                     bf(MLA_OUT), cache(MLA_KV_RANK), cache(MLA_ROPE_DIM)]
        out_specs = [pl.BlockSpec((tm, NA_WIDTH), tok), cspec(NA_WIDTH), cspec(NA_WIDTH),
                     pl.BlockSpec((tm, MLA_QK_W), tok), pl.BlockSpec((tm, MLA_QK_W), tok),
                     pl.BlockSpec((tm, MLA_OUT), tok), cspec(MLA_KV_RANK), cspec(MLA_ROPE_DIM)]
        if caches is not None:
            for arr, out_idx in zip(caches, (1, 2, 6, 7)):
                aliases[len(args)] = out_idx
                in_specs.append(pl.BlockSpec(memory_space=pl.ANY))
                args.append(arr)
    return pl.pallas_call(
        functools.partial(_even_proj_kernel, latent=latent, n_alias=len(aliases)),
        grid=(m // tm,),
        in_specs=in_specs,
        out_specs=out_specs,
        out_shape=out_shape,
        input_output_aliases=aliases,
        compiler_params=_cparams(1),
        name="even_proj_latent" if latent else "even_proj_context",
    )(*args)


def _mla_cache_kernel(ckv_ref, kr_ref, wuk_ref, wuv_ref, km_ref, vm_ref):
    ckvb = ckv_ref[0, 0].astype(BF16)
    km = _dot(ckvb, wuk_ref[0])
    kr = kr_ref[...]
    for hd in range(MLA_HEADS):
        sl = slice(hd * LANES, (hd + 1) * LANES)
        km_ref[:, sl] = (km[:, sl] + kr).astype(BF16)
    vm_ref[...] = _dot(ckvb, wuv_ref[0]).astype(BF16)


def _mla_cache_keys(cache_ckv, kr_padded, w_uk, w_uv, i_even):
    nb, _, past, _ = cache_ckv.shape
    return pl.pallas_call(
        _mla_cache_kernel,
        grid=(nb,),
        in_specs=[pl.BlockSpec((1, 1, past, MLA_KV_RANK), lambda b: (b, i_even, 0, 0)),
                  pl.BlockSpec((past, LANES), lambda b: (b, 0)),
                  _resident((1, MLA_KV_RANK, MLA_QK_W), lambda b: (i_even, 0, 0)),
                  _resident((1, MLA_KV_RANK, MLA_OUT), lambda b: (i_even, 0, 0))],
        out_specs=[pl.BlockSpec((past, MLA_QK_W), lambda b: (b, 0)),
                   pl.BlockSpec((past, MLA_OUT), lambda b: (b, 0))],
        out_shape=[jax.ShapeDtypeStruct((nb * past, MLA_QK_W), BF16),
                   jax.ShapeDtypeStruct((nb * past, MLA_OUT), BF16)],
        compiler_params=_cparams(1),
        name="mla_cache_keys",
    )(cache_ckv, kr_padded, w_uk, w_uv)


def _half_masks():
    lane = lax.broadcasted_iota(jnp.int32, (1, LANES), 1)
    lo = lane < (LANES // 2)
    return lo, jnp.logical_not(lo)


def _scores_pass(q, key_blocks, bias_blocks, s_ref, kb):
    mx = None
    for j, (k_blk, bias) in enumerate(zip(key_blocks, bias_blocks)):
        sc = _dot_nt(q, k_blk())
        if bias is not None:
            sc = sc + bias()
        s_ref[:, j * kb:(j + 1) * kb] = sc
        for g in range(kb // LANES):
            part = sc[:, g * LANES:(g + 1) * LANES]
            mx = part if mx is None else jnp.maximum(mx, part)
    return jnp.max(mx, axis=-1, keepdims=True)


def _values_pass(s_ref, m, value_blocks, kb):
    ls = None
    acc = None
    for j, v_blk in enumerate(value_blocks):
        p = jnp.exp(s_ref[:, j * kb:(j + 1) * kb] - m)
        for g in range(kb // LANES):
            part = p[:, g * LANES:(g + 1) * LANES]
            ls = part if ls is None else ls + part
        o = _dot(p.astype(BF16), v_blk())
        acc = o if acc is None else acc + o
    return acc / jnp.sum(ls, axis=-1, keepdims=True)


def _attention_chains(chains, s_scr, kb=None):
    kb = ATT_KB if kb is None else kb
    depth = s_scr.shape[0]
    outs = []
    maxes = {}
    for i in range(len(chains) + depth - 1):
        if i < len(chains):
            q, key_blocks, _, bias_blocks = chains[i]
            maxes[i] = _scores_pass(q(), key_blocks, bias_blocks, s_scr.at[i % depth], kb)
        j = i - (depth - 1)
        if j >= 0:
            outs.append(_values_pass(s_scr.at[j % depth], maxes.pop(j), chains[j][2], kb))
    return outs


def _rows(ref, start, size, lanes):
    return lambda: ref[start:start + size, lanes]


def _ctx_attn_kernel(qa_ref, ka_ref, va_ref, qm_ref, km_ref, vm_ref, o_ref, s_scr, kv_scr):
    lo, hi = _half_masks()
    zero = jnp.zeros((), BF16)
    seq = qa_ref.shape[0]
    kb_sz = min(ATT_KB, seq)
    qs_sz = min(ATT_QS, seq)
    kv_scr[0] = ka_ref[0, 0].astype(BF16)
    kv_scr[1] = va_ref[0, 0].astype(BF16)
    chains = []
    for g in range((NA_WIDTH + MLA_OUT) // LANES):
        mla = g >= NA_WIDTH // LANES
        gl = g - NA_WIDTH // LANES if mla else g
        sl = slice(gl * LANES, (gl + 1) * LANES)
        for qs in range(seq // qs_sz):
            rows = slice(qs * qs_sz, (qs + 1) * qs_sz)
            for t, msk in enumerate((lo, hi)):
                if mla:
                    hsl = slice((2 * gl + t) * LANES, (2 * gl + t + 1) * LANES)
                    q = lambda rows=rows, hsl=hsl: qm_ref[rows, hsl]
                    kb = [_rows(km_ref, j * kb_sz, kb_sz, hsl) for j in range(seq // kb_sz)]
                    vb = [_rows(vm_ref, j * kb_sz, kb_sz, sl) for j in range(seq // kb_sz)]
                else:
                    q = lambda rows=rows, sl=sl, msk=msk: jnp.where(msk, qa_ref[rows, sl], zero)
                    kb = [_rows(kv_scr.at[0], j * kb_sz, kb_sz, sl) for j in range(seq // kb_sz)]
                    vb = [_rows(kv_scr.at[1], j * kb_sz, kb_sz, sl) for j in range(seq // kb_sz)]
                chains.append((q, kb, vb, [None] * len(kb)))
    outs = _attention_chains(chains, s_scr, kb_sz)
    n = 0
    for g in range((NA_WIDTH + MLA_OUT) // LANES):
        for qs in range(seq // qs_sz):
            o_ref[qs * qs_sz:(qs + 1) * qs_sz, g * LANES:(g + 1) * LANES] = (
                jnp.where(lo, outs[n], outs[n + 1]).astype(BF16))
            n += 2


def _ctx_attention(qa, ka, va, qm, km, vm, seq, i_even):
    m = qa.shape[0]
    nb = m // seq
    tok = lambda b: (b, 0)
    cache = lambda b: (b, i_even, 0, 0)
    return pl.pallas_call(
        _ctx_attn_kernel,
        grid=(nb,),
        in_specs=[pl.BlockSpec((seq, NA_WIDTH), tok),
                  pl.BlockSpec((1, 1, seq, NA_WIDTH), cache),
                  pl.BlockSpec((1, 1, seq, NA_WIDTH), cache),
                  pl.BlockSpec((seq, MLA_QK_W), tok),
                  pl.BlockSpec((seq, MLA_QK_W), tok),
                  pl.BlockSpec((seq, MLA_OUT), tok)],
        out_specs=pl.BlockSpec((seq, NA_WIDTH + MLA_OUT), tok),
        out_shape=jax.ShapeDtypeStruct((m, NA_WIDTH + MLA_OUT), BF16),
        scratch_shapes=[pltpu.VMEM((2, min(ATT_QS, seq), seq), F32), pltpu.VMEM((2, seq, NA_WIDTH), BF16)],
        compiler_params=_cparams(1),
        name="context_attention",
    )(qa, ka, va, qm, km, vm)


def _lat_mla_kernel(q_ref, kc_ref, kl_ref, vc_ref, vl_ref, o_ref, s_scr):
    lo, _ = _half_masks()
    past, seq = kc_ref.shape[0], kl_ref.shape[0]
    all_lanes = slice(0, LANES)
    kbs = []
    for t in range(2):
        sl = slice(t * LANES, (t + 1) * LANES)
        kbs.append([_rows(kc_ref, j * ATT_KB, ATT_KB, sl) for j in range(past // ATT_KB)]
                   + [_rows(kl_ref, j * ATT_KB, ATT_KB, sl) for j in range(seq // ATT_KB)])
    vb = ([_rows(vc_ref, j * ATT_KB, ATT_KB, all_lanes) for j in range(past // ATT_KB)]
          + [_rows(vl_ref, j * ATT_KB, ATT_KB, all_lanes) for j in range(seq // ATT_KB)])
    n_qs = ATT_TQ // ATT_QS

    def tile(i, carry):
        base = pl.multiple_of(i * ATT_TQ, ATT_TQ)
        chains = []
        for qs in range(n_qs):
            rows = pl.ds(base + qs * ATT_QS, ATT_QS)
            for t in range(2):
                sl = slice(t * LANES, (t + 1) * LANES)
                chains.append((lambda rows=rows, sl=sl: q_ref[rows, sl], kbs[t], vb, [None] * len(vb)))
        outs = _attention_chains(chains, s_scr)
        for qs in range(n_qs):
            o_ref[pl.ds(base + qs * ATT_QS, ATT_QS), :] = (
                jnp.where(lo, outs[2 * qs], outs[2 * qs + 1]).astype(BF16))
        return carry

    lax.fori_loop(0, seq // ATT_TQ, tile, 0)


def _latent_mla_attention(qm, km_ctx, km_lat, vm_ctx, vm_lat, seq, past):
    m = qm.shape[0]
    nb = m // seq
    pair = 2 * LANES
    return pl.pallas_call(
        _lat_mla_kernel,
        grid=(nb, MLA_HEADS // 2),
        in_specs=[pl.BlockSpec((seq, pair), lambda b, g: (b, g)),
                  pl.BlockSpec((past, pair), lambda b, g: (b, g)),
                  pl.BlockSpec((seq, pair), lambda b, g: (b, g)),
                  pl.BlockSpec((past, LANES), lambda b, g: (b, g)),
                  pl.BlockSpec((seq, LANES), lambda b, g: (b, g))],
        out_specs=pl.BlockSpec((seq, LANES), lambda b, g: (b, g)),
        out_shape=jax.ShapeDtypeStruct((m, MLA_OUT), BF16),
        scratch_shapes=[pltpu.VMEM((2, ATT_QS, past + seq), F32)],
        compiler_params=_cparams(2),
        name="latent_mla_attention",
    )(qm, km_ctx, km_lat, vm_ctx, vm_lat)


def _na_key_start(row_block, rows):
    return np.clip(row_block * NA_QROWS - NA_ROWS // 2, 0, rows - NA_KROWS)


def _na_bias_kernel(rpb_ref, o_ref, *, rows):
    hd = pl.program_id(0)
    kr = min(NA_ROWS, rows)
    n_dc = 2 * NA_COLS - 1
    c = lax.broadcasted_iota(jnp.int32, (GRID_W, LANES), 0)
    lane = lax.broadcasted_iota(jnp.int32, (GRID_W, LANES), 1)
    n = lane % GRID_W
    c0 = jnp.clip(c - NA_COLS // 2, 0, GRID_W - NA_COLS)
    in_win = (n >= c0) & (n < c0 + NA_COLS)
    dc = n - c + NA_COLS - 1
    left = lane < GRID_W
    neg = jnp.full((GRID_W, LANES), NEG_BIG, F32)
    col_tables = []
    for dr in range(2 * NA_ROWS - 1):
        t = neg
        for j in range(n_dc):
            t = jnp.where(dc == j, rpb_ref[hd * (2 * NA_ROWS - 1) * n_dc + dr * n_dc + j], t)
        col_tables.append(jnp.where(in_win, t, neg))
    for rb in range(rows // NA_QROWS):
        start = int(_na_key_start(rb, rows))
        for rl in range(NA_QROWS):
            r = rb * NA_QROWS + rl
            r0 = min(max(r - kr // 2, 0), rows - kr)
            for pr in range(NA_KROWS // 2):
                halves = []
                for key_row in (start + 2 * pr, start + 2 * pr + 1):
                    ok = r0 <= key_row < r0 + kr
                    halves.append(col_tables[key_row - r + NA_ROWS - 1] if ok else neg)
                o_ref[rb, 0, rl * GRID_W:(rl + 1) * GRID_W, pr * LANES:(pr + 1) * LANES] = (
                    jnp.where(left, halves[0], halves[1]))


def _na_bias(rpb, rows):
    n_rb = rows // NA_QROWS
    flat = rpb.reshape(-1)
    return pl.pallas_call(
        functools.partial(_na_bias_kernel, rows=rows),
        grid=(NA_HEADS,),
        in_specs=[pl.BlockSpec(memory_space=pltpu.SMEM)],
        out_specs=pl.BlockSpec((n_rb, 1, NA_QROWS * GRID_W, NA_KROWS * GRID_W), lambda h: (0, h, 0, 0)),
        out_shape=jax.ShapeDtypeStruct((n_rb, NA_HEADS, NA_QROWS * GRID_W, NA_KROWS * GRID_W), F32),
        compiler_params=_cparams(1),
        name="na_bias",
    )(flat)


def _lat_na_kernel(q_ref, k_ref, v_ref, kc_ref, vc_ref, bias_ref, o_ref, s_scr, kv_scr, *, rows):
    lo, hi = _half_masks()
    zero = jnp.zeros((), BF16)
    rb = pl.program_id(0)
    start = jnp.clip(rb * NA_QROWS - NA_ROWS // 2, 0, rows - NA_KROWS) * GRID_W
    start = pl.multiple_of(start, GRID_W * (NA_ROWS // 2))
    nk = NA_KROWS * GRID_W
    past = kc_ref.shape[2]
    all_lanes = slice(0, LANES)
    n_qs = q_ref.shape[1] // ATT_QS

    def batch(b, carry):
        kv_scr[0] = kc_ref[b, 0].astype(BF16)
        kv_scr[1] = vc_ref[b, 0].astype(BF16)

        def win(ref, j):
            return lambda: ref[b, pl.ds(start + j * ATT_KB, ATT_KB), :]

        kb = ([win(k_ref, j) for j in range(nk // ATT_KB)]
              + [_rows(kv_scr.at[0], j * ATT_KB, ATT_KB, all_lanes) for j in range(past // ATT_KB)])
        vb = ([win(v_ref, j) for j in range(nk // ATT_KB)]
              + [_rows(kv_scr.at[1], j * ATT_KB, ATT_KB, all_lanes) for j in range(past // ATT_KB)])
        chains = []
        for qs in range(n_qs):
            qrows = slice(qs * ATT_QS, (qs + 1) * ATT_QS)
            for t, msk in enumerate((lo, hi)):
                bias = ([(lambda t=t, j=j, qrows=qrows: bias_ref[0, t, qrows, j * ATT_KB:(j + 1) * ATT_KB])
                         for j in range(nk // ATT_KB)] + [None] * (past // ATT_KB))
                chains.append((lambda qrows=qrows, msk=msk: jnp.where(msk, q_ref[b, qrows, :], zero),
                               kb, vb, bias))
        outs = _attention_chains(chains, s_scr)
        for qs in range(n_qs):
            o_ref[b, qs * ATT_QS:(qs + 1) * ATT_QS, :] = (
                jnp.where(lo, outs[2 * qs], outs[2 * qs + 1]).astype(BF16))
        return carry

    lax.fori_loop(0, q_ref.shape[0], batch, 0)


def _latent_na_attention(qa, ka, va, cache_k, cache_v, bias, i_even, seq):
    m = qa.shape[0]
    nb = m // seq
    rows = seq // GRID_W
    n_rb = rows // NA_QROWS
    tq = NA_QROWS * GRID_W
    past = cache_k.shape[2]
    per_batch = lambda a: a.reshape(nb, seq, NA_WIDTH)
    out = pl.pallas_call(
        functools.partial(_lat_na_kernel, rows=rows),
        grid=(n_rb, NA_HEADS // 2),
        in_specs=[pl.BlockSpec((nb, tq, LANES), lambda r, g: (0, r, g)),
                  pl.BlockSpec((nb, seq, LANES), lambda r, g: (0, 0, g)),
                  pl.BlockSpec((nb, seq, LANES), lambda r, g: (0, 0, g)),
                  pl.BlockSpec((nb, 1, past, LANES), lambda r, g: (0, i_even, 0, g)),
                  pl.BlockSpec((nb, 1, past, LANES), lambda r, g: (0, i_even, 0, g)),
                  pl.BlockSpec((1, 2, tq, NA_KROWS * GRID_W), lambda r, g: (r, g, 0, 0))],
        out_specs=pl.BlockSpec((nb, tq, LANES), lambda r, g: (0, r, g)),
        out_shape=jax.ShapeDtypeStruct((nb, seq, NA_WIDTH), BF16),
        scratch_shapes=[pltpu.VMEM((2, ATT_QS, NA_KROWS * GRID_W + past), F32),
                        pltpu.VMEM((2, past, LANES), BF16)],
        compiler_params=_cparams(2),
        name="latent_na_attention",
    )(per_batch(qa), per_batch(ka), per_batch(va), cache_k, cache_v, bias)
    return out.reshape(m, NA_WIDTH)


_RET_NK = RET_HEADS * RET_DK
_RET_NV = RET_HEADS * RET_DV


def _ret_proj_kernel(x_ref, mod_ref, g_ref, w_ref, cos_ref, sin_ref, q_ref, k_ref, v_ref, gate_ref):
    h = _modulated(x_ref[...], mod_ref, g_ref, 1).astype(BF16)

    def rope(y):
        parts = [pltpu.roll(y[:, g * LANES:(g + 1) * LANES], LANES // 2, 1) for g in range(RET_DK // LANES)]
        return y * cos_ref[...] + jnp.concatenate(parts, axis=1) * sin_ref[...]

    for hd in range(RET_HEADS):
        sl = slice(hd * RET_DK, (hd + 1) * RET_DK)
        q_ref[:, sl] = rope(_dot(h, w_ref[0, :, sl])).astype(BF16)
        k = _dot(h, w_ref[0, :, _RET_NK + hd * RET_DK:_RET_NK + (hd + 1) * RET_DK]) * (RET_DK ** -0.5)
        k_ref[:, sl] = rope(k)
    step = 512
    for j in range(_RET_NV // step):
        o = 2 * _RET_NK + j * step
        v_ref[:, j * step:(j + 1) * step] = _dot(h, w_ref[0, :, o:o + step]).astype(BF16)
    for j in range(2 * _RET_NV // step):
        o = 2 * _RET_NK + _RET_NV + j * step
        gate_ref[:, j * step:(j + 1) * step] = _dot(h, w_ref[0, :, o:o + step])


def _ret_proj_latent(x, mods, norm_g, w_in, layer, seq, rope):
    m = x.shape[0]
    tm = 256
    i_odd = layer // 2
    tok = lambda i: (i, 0)
    n_in = 2 * _RET_NK + 3 * _RET_NV
    nt = seq // tm
    return pl.pallas_call(
        _ret_proj_kernel,
        grid=(m // tm,),
        in_specs=[pl.BlockSpec((tm, D_MODEL), tok),
                  pl.BlockSpec((1, 1, N_MOD * D_MODEL), _mod_index(layer, True, tm, seq)),
                  pl.BlockSpec((1, 1, D_MODEL), lambda i: (layer * 3 + 1, 0, 0)),
                  _resident((1, D_MODEL, n_in), lambda i: (i_odd, 0, 0)),
                  pl.BlockSpec((tm, RET_DK), lambda i: (i % nt, 0)),
                  pl.BlockSpec((tm, RET_DK), lambda i: (i % nt, 0))],
        out_specs=[pl.BlockSpec((tm, _RET_NK), tok),
                   pl.BlockSpec((tm, _RET_NK), tok),
                   pl.BlockSpec((tm, _RET_NV), tok),
                   pl.BlockSpec((tm, 2 * _RET_NV), tok)],
        out_shape=[jax.ShapeDtypeStruct((m, _RET_NK), BF16),
                   jax.ShapeDtypeStruct((m, _RET_NK), F32),
                   jax.ShapeDtypeStruct((m, _RET_NV), BF16),
                   jax.ShapeDtypeStruct((m, 2 * _RET_NV), F32)],
        compiler_params=_cparams(1),
        name="ret_proj_latent",
    )(x, mods, norm_g, w_in, *rope)


def _ret_rope_tables(seq):
    d = RET_DK // 2
    half = d // 2
    t = jnp.arange(seq)
    j = np.arange(RET_DK)
    pos = jnp.where((j // d)[None, :] == 0, (t // GRID_W)[:, None], (t % GRID_W)[:, None])
    inv = ROPE_BASE ** (-jnp.arange(half, dtype=F32) / half)
    ang = pos.astype(F32) * inv[j % half][None, :]
    sign = np.where((j % d) < half, -1.0, 1.0).astype(np.float32)
    return jnp.cos(ang), jnp.sin(ang) * sign[None, :]


def _head_norm(o):
    mu = jnp.mean(o, axis=-1, keepdims=True)
    d = o - mu
    return d * lax.rsqrt(jnp.mean(d * d, axis=-1, keepdims=True) + EPS)


def _ret_decays(lg, dirn, c):
    ii = lax.broadcasted_iota(jnp.int32, (c, c), 0)
    jj = lax.broadcasted_iota(jnp.int32, (c, c), 1)
    row = lax.broadcasted_iota(jnp.int32, (c, 1), 0).astype(F32)
    diff = (ii - jj if dirn == 0 else jj - ii).astype(F32)
    intra = jnp.where(diff >= 0, jnp.exp(lg * jnp.maximum(diff, 0.0)), 0.0)
    if dirn == 0:
        q_dec = jnp.exp(lg * (row + 1.0))
        k_dec = jnp.exp(lg * (c - 1.0 - row))
    else:
        q_dec = jnp.exp(lg * (c - row))
        k_dec = jnp.exp(lg * row)
    return intra, q_dec, k_dec, jnp.exp(lg * float(c))


def _ret_ctx_kernel(*refs, n_alias):
    ld_ref, x_ref, mod_ref, g_ref, w_ref = refs[:5]
    y_ref, st_ref = refs[5 + n_alias:]
    c = x_ref.shape[0]
    h = _modulated(x_ref[...], mod_ref, g_ref, 1).astype(BF16)
    for hd in range(RET_HEADS):
        ksl = slice(hd * RET_DK, (hd + 1) * RET_DK)
        qb = _dot(h, w_ref[0, :, ksl]).astype(BF16)
        k = _dot(h, w_ref[0, :, _RET_NK + hd * RET_DK:_RET_NK + (hd + 1) * RET_DK]) * (RET_DK ** -0.5)
        vo = 2 * _RET_NK + hd * RET_DV
        v = _dot(h, w_ref[0, :, vo:vo + RET_DV]).astype(BF16)
        qk = _dot_nt(qb, k.astype(BF16))
        y = None
        for dirn in range(2):
            intra, _, k_dec, _ = _ret_decays(-jnp.abs(ld_ref[dirn, hd]), dirn, c)
            o = _dot((qk * intra).astype(BF16), v)
            st_ref[0, 0, dirn, hd] = _dot((k * k_dec).T.astype(BF16), v)
            go = 2 * _RET_NK + (1 + dirn) * _RET_NV + hd * RET_DV
            yd = _silu(_dot(h, w_ref[0, :, go:go + RET_DV])) * _head_norm(o)
            y = yd if y is None else y + yd
        y_ref[:, hd * RET_DV:(hd + 1) * RET_DV] = y.astype(BF16)


def _retention_context(x, mods, norm_g, w_in, log_decay, layer, seq, n_odd, prev_states=None):
    assert seq == RET_CHUNK
    m = x.shape[0]
    nb = m // seq
    i_odd = layer // 2
    n_in = 2 * _RET_NK + 3 * _RET_NV
    in_specs = [pl.BlockSpec(memory_space=pltpu.SMEM),
                pl.BlockSpec((seq, D_MODEL), lambda b: (b, 0)),
                pl.BlockSpec((1, 1, N_MOD * D_MODEL), _mod_index(layer, False, seq, seq)),
                pl.BlockSpec((1, 1, D_MODEL), lambda b: (layer * 3 + 1, 0, 0)),
                _resident((1, D_MODEL, n_in), lambda b: (i_odd, 0, 0))]
    args = [log_decay[i_odd], x, mods, norm_g, w_in]
    aliases = {}
    if prev_states is not None:
        aliases[len(args)] = 1
        in_specs.append(pl.BlockSpec(memory_space=pl.ANY))
        args.append(prev_states)
    return pl.pallas_call(
        functools.partial(_ret_ctx_kernel, n_alias=len(aliases)),
        grid=(nb,),
        in_specs=in_specs,
        out_specs=[pl.BlockSpec((seq, _RET_NV), lambda b: (b, 0)),
                   pl.BlockSpec((1, 1, 2, RET_HEADS, RET_DK, RET_DV), lambda b: (b, i_odd, 0, 0, 0, 0))],
        out_shape=[jax.ShapeDtypeStruct((m, _RET_NV), BF16),
                   jax.ShapeDtypeStruct((nb, n_odd, 2, RET_HEADS, RET_DK, RET_DV), F32)],
        input_output_aliases=aliases,
        compiler_params=_cparams(1),
        name="retention_context",
    )(*args)


def _ret_lat_kernel(ld_ref, q_ref, k_ref, v_ref, gf_ref, gb_ref, s0_ref, y_ref, s_scr, of_scr):
    hd = pl.program_id(1)
    c = RET_CHUNK
    n_chunks = q_ref.shape[0] // c
    for dirn in range(2):
        intra, q_dec, k_dec, c_dec = _ret_decays(-jnp.abs(ld_ref[dirn, hd]), dirn, c)
        g_ref = gf_ref if dirn == 0 else gb_ref
        s_scr[...] = s0_ref[0, 0, dirn, 0]

        def body(t, carry, dirn=dirn, intra=intra, q_dec=q_dec, k_dec=k_dec, c_dec=c_dec, g_ref=g_ref):
            ci = t if dirn == 0 else n_chunks - 1 - t
            rows = pl.ds(pl.multiple_of(ci * c, c), c)
            qb = q_ref[rows, :]
            k = k_ref[rows, :]
            v = v_ref[rows, :]
            a = _dot_nt(qb, k.astype(BF16)) * intra
            s_prev = s_scr[...]
            o = _dot(a.astype(BF16), v) + _dot(qb, s_prev.astype(BF16)) * q_dec
            s_scr[...] = s_prev * c_dec + _dot((k * k_dec).T.astype(BF16), v)
            y = _silu(g_ref[rows, :]) * _head_norm(o)
            if dirn == 0:
                of_scr[rows, :] = y
            else:
                y_ref[rows, :] = (of_scr[rows, :] + y).astype(BF16)
            return carry

        lax.fori_loop(0, n_chunks, body, 0)


def _retention_latent(q, k, v, gates, log_decay, i_odd, seq, state):
    m = q.shape[0]
    nb = m // seq
    return pl.pallas_call(
        _ret_lat_kernel,
        grid=(nb, RET_HEADS),
        in_specs=[pl.BlockSpec(memory_space=pltpu.SMEM),
                  pl.BlockSpec((seq, RET_DK), lambda b, h: (b, h)),
                  pl.BlockSpec((seq, RET_DK), lambda b, h: (b, h)),
                  pl.BlockSpec((seq, RET_DV), lambda b, h: (b, h)),
                  pl.BlockSpec((seq, RET_DV), lambda b, h: (b, h)),
                  pl.BlockSpec((seq, RET_DV), lambda b, h: (b, RET_HEADS + h)),
                  pl.BlockSpec((1, 1, 2, 1, RET_DK, RET_DV), lambda b, h: (b, i_odd, 0, h, 0, 0))],
        out_specs=pl.BlockSpec((seq, RET_DV), lambda b, h: (b, h)),
        out_shape=jax.ShapeDtypeStruct((m, _RET_NV), BF16),
        scratch_shapes=[pltpu.VMEM((RET_DK, RET_DV), F32), pltpu.VMEM((seq, RET_DV), F32)],
        compiler_params=_cparams(2),
        name="retention_latent",
    )(log_decay[i_odd], q, k, v, gates, gates, state)


def kernel(x_prompt, x_sample, c, cache_na_k, cache_na_v, cache_mla_ckv, cache_mla_krope, state_ret,
           c_ctx, norm_g, ada_w, ada_b, ffn_w13, ffn_w2, mix_w_in, mla_q_norm, mla_kv_norm, mla_w_uq,
           mla_w_ukv, na_rpb, mix_w_out, ret_w_in, ret_log_decay, ret_w_out, final_norm_g):
    batch, seq, _ = x_prompt.shape
    dec_batch, dec_seq, _ = x_sample.shape
    past = cache_na_k.shape[2]
    n_even = mix_w_in.shape[0]
    assert dec_batch + 1 <= MOD_ROWS and seq == RET_CHUNK and dec_seq % RET_CHUNK == 0
    assert (dec_seq // GRID_W) % NA_QROWS == 0 and dec_seq // GRID_W >= NA_KROWS

    w13 = ffn_w13.astype(BF16)
    w2 = ffn_w2.astype(BF16)
    w_out_even = mix_w_out.astype(BF16)
    w_in_ret = ret_w_in.astype(BF16)
    w_out_ret = ret_w_out.astype(BF16)
    packed = _pack_even_weights(mix_w_in, mla_w_uq, mla_w_ukv)
    q_norm = mla_q_norm.reshape(n_even, 1, MLA_Q_RANK)
    kv_norm = mla_kv_norm.reshape(n_even, 1, MLA_KV_RANK)
    norm_g3 = norm_g.reshape(DEPTH * 3, 1, D_MODEL)
    mla_rope = _mla_rope_tables(dec_seq)
    ret_rope = _ret_rope_tables(dec_seq)
    cache_k = cache_na_k.reshape(dec_batch, n_even, past, NA_WIDTH)
    cache_v = cache_na_v.reshape(dec_batch, n_even, past, NA_WIDTH)

    cvec = jnp.concatenate([c_ctx[None, :], c, jnp.zeros((MOD_ROWS - 1 - dec_batch, D_MODEL), F32)], axis=0)
    mods = _ada_all(cvec, ada_w, ada_b).reshape(DEPTH * MOD_ROWS, 1, N_MOD * D_MODEL)

    xp = x_prompt.reshape(batch * seq, D_MODEL)
    xs = x_sample.reshape(dec_batch * dec_seq, D_MODEL)
    n_odd = ret_w_in.shape[0]
    caches = None
    states = None
    for layer in range(DEPTH):
        i = layer // 2
        xp = _ffn(xp, mods, norm_g3, w13, w2, layer, 0, False, seq)
        xs = _ffn(xs, mods, norm_g3, w13, w2, layer, 0, True, dec_seq)
        if layer % 2 == 0:
            qa, ka, va, qm, km, vm, ckv, kr = _even_proj(
                xp, mods, norm_g3, packed, q_norm, kv_norm, layer, False, seq, n_even, caches=caches)
            caches = (ka, va, ckv, kr)
            op = _ctx_attention(qa, ka, va, qm, km, vm, seq, i)
            pre_p = ([op], [(w_out_even, NA_WIDTH + MLA_OUT, 0)], i)

            qa, ka, va, qm, km, vm = _even_proj(
                xs, mods, norm_g3, packed, q_norm, kv_norm, layer, True, dec_seq, n_even, rope=mla_rope)
            kr_pad = jnp.pad(cache_mla_krope[:, i].reshape(dec_batch * past, MLA_ROPE_DIM),
                             ((0, 0), (MLA_NOPE_DIM, LANES - MLA_NOPE_DIM - MLA_ROPE_DIM)))
            km_ctx, vm_ctx = _mla_cache_keys(cache_mla_ckv, kr_pad, packed[3], packed[4], i)
            bias = _na_bias(na_rpb[i], dec_seq // GRID_W)
            oa = _latent_na_attention(qa, ka, va, cache_k, cache_v, bias, i, dec_seq)
            ob = _latent_mla_attention(qm, km_ctx, km, vm_ctx, vm, dec_seq, past)
            pre_s = ([oa, ob], [(w_out_even, NA_WIDTH, 0), (w_out_even, MLA_OUT, NA_WIDTH // MLA_OUT)], i)
        else:
            yp, states = _retention_context(xp, mods, norm_g3, w_in_ret, ret_log_decay, layer, seq, n_odd,
                                            prev_states=states)
            pre_p = ([yp], [(w_out_ret, _RET_NV, 0)], i)

            q, k, v, gates = _ret_proj_latent(xs, mods, norm_g3, w_in_ret, layer, dec_seq, ret_rope)
            ys = _retention_latent(q, k, v, gates, ret_log_decay, i, dec_seq, state_ret)
            pre_s = ([ys], [(w_out_ret, _RET_NV, 0)], i)
        last = final_norm_g if layer == DEPTH - 1 else None
        xp = _ffn(xp, mods, norm_g3, w13, w2, layer, 1, False, seq, pre=pre_p, final_g=last)
        xs = _ffn(xs, mods, norm_g3, w13, w2, layer, 1, True, dec_seq, pre=pre_s, final_g=last)

    y_prompt = xp.reshape(batch, seq, D_MODEL)
    y_sample = xs.reshape(dec_batch, dec_seq, D_MODEL)
    ka, va, ckv, kr = caches
    new_na_k = ka.reshape(batch, n_even, seq, NA_HEADS, NA_HEAD_DIM)
    new_na_v = va.reshape(batch, n_even, seq, NA_HEADS, NA_HEAD_DIM)
    return (y_prompt, y_sample, new_na_k, new_na_v, ckv, kr, states)
```

```python
import functools

import jax
import jax.numpy as jnp
import numpy as np
from jax import lax
from jax.experimental import pallas as pl
from jax.experimental.pallas import tpu as pltpu

F32 = jnp.float32
BF16 = jnp.bfloat16

D_MODEL = 1024
DEPTH = 4
GRID_W = 64
NA_HEADS = 8
NA_HEAD_DIM = 64
NA_ROWS = 8
NA_COLS = 16
MLA_HEADS = 8
MLA_Q_RANK = 384
MLA_KV_RANK = 256
MLA_NOPE_DIM = 64
MLA_ROPE_DIM = 32
MLA_V_DIM = 64
RET_HEADS = 4
RET_DK = D_MODEL // RET_HEADS
RET_DV = 2 * D_MODEL // RET_HEADS
FFN_DIM = ((8 * D_MODEL // 3 + 127) // 128) * 128
N_MOD = 9
ROPE_BASE = 10000.0
EPS = 1e-6
NA_WIDTH = NA_HEADS * NA_HEAD_DIM
MLA_OUT = MLA_HEADS * MLA_V_DIM

LANES = 128
MOD_ROWS = 8
NEG_BIG = -1e30
VMEM_LIMIT = 52 * 1024 * 1024

FFN_TF = 256
RET_CHUNK = 256
NA_QROWS = 8
NA_WROWS = 12
ATT_QS = 128
ATT_KB = 256
ATT_TQ = 512


def _cparams(n_axes):
    return pltpu.CompilerParams(dimension_semantics=("arbitrary",) * n_axes,
                                vmem_limit_bytes=VMEM_LIMIT)


def _resident(block, index_map):
    return pl.BlockSpec(block, index_map, pipeline_mode=pl.Buffered(1))


def _dot(a, b):
    return jnp.dot(a, b, preferred_element_type=F32)


def _dot_nt(a, b):
    return lax.dot_general(a, b, (((1,), (1,)), ((), ())), preferred_element_type=F32)


def _silu(x):
    return x * (0.5 * jnp.tanh(0.5 * x) + 0.5)


def _rms(x, g):
    return x * lax.rsqrt(jnp.mean(x * x, axis=-1, keepdims=True) + EPS) * g


def _modulated(x, mod_ref, g_ref, s):
    shift = mod_ref[0, :, 3 * s * D_MODEL:(3 * s + 1) * D_MODEL]
    scale = mod_ref[0, :, (3 * s + 1) * D_MODEL:(3 * s + 2) * D_MODEL]
    return _rms(x, g_ref[0]) * (1.0 + scale) + shift


def _gate(mod_ref, s):
    return mod_ref[0, :, (3 * s + 2) * D_MODEL:(3 * s + 3) * D_MODEL]


def _mod_index(layer, latent, tile_rows, seq):
    if not latent:
        return lambda i: (layer * MOD_ROWS, 0, 0)
    return lambda i: (layer * MOD_ROWS + 1 + (i * tile_rows) // seq, 0, 0)


def _ada_kernel(c_ref, w_ref, b_ref, o_ref):
    s = _silu(c_ref[...]).astype(BF16)
    o_ref[0] = _dot(s, w_ref[0].astype(BF16)) + b_ref[0]


def _ada_all(cvec, ada_w, ada_b):
    tn = 1024
    n = N_MOD * D_MODEL
    return pl.pallas_call(
        _ada_kernel,
        grid=(DEPTH, n // tn),
        in_specs=[pl.BlockSpec((MOD_ROWS, D_MODEL), lambda l, j: (0, 0)),
                  pl.BlockSpec((1, D_MODEL, tn), lambda l, j: (l, 0, j)),
                  pl.BlockSpec((1, 1, tn), lambda l, j: (l, 0, j))],
        out_specs=pl.BlockSpec((1, MOD_ROWS, tn), lambda l, j: (l, 0, j)),
        out_shape=jax.ShapeDtypeStruct((DEPTH, MOD_ROWS, n), F32),
        compiler_params=_cparams(2),
        name="adaln",
    )(cvec, ada_w, ada_b.reshape(DEPTH, 1, n))


def _ffn_kernel(*refs, s, n_pre, final):
    x_ref, mod_ref, g_ref, w13_ref, w2_ref = refs[:5]
    pre_y = refs[5:5 + n_pre]
    pre_w = refs[5 + n_pre:5 + 2 * n_pre]
    fg_ref = refs[5 + 2 * n_pre] if final else None
    o_ref = refs[-1]
    x = x_ref[...]
    if n_pre:
        mix = None
        for y_ref, w_ref in zip(pre_y, pre_w):
            y = _dot(y_ref[...], w_ref[0])
            mix = y if mix is None else mix + y
        x = x + _gate(mod_ref, 1) * mix
    h = _modulated(x, mod_ref, g_ref, s).astype(BF16)
    acc = jnp.zeros(x.shape, F32)
    for j in range(FFN_DIM // FFN_TF):
        a = _dot(h, w13_ref[0, 0, :, j * FFN_TF:(j + 1) * FFN_TF])
        b = _dot(h, w13_ref[0, 0, :, FFN_DIM + j * FFN_TF:FFN_DIM + (j + 1) * FFN_TF])
        u = (_silu(a) * b).astype(BF16)
        acc = acc + _dot(u, w2_ref[0, 0, j * FFN_TF:(j + 1) * FFN_TF, :])
    out = x + 0.5 * _gate(mod_ref, s) * acc
    if final:
        out = _rms(out, fg_ref[...])
    o_ref[...] = out


def _ffn(x, mods, norm_g, w13, w2, layer, which, latent, seq, pre=None, final_g=None):
    tm = 1024 if pre is None else 512
    m = x.shape[0]
    assert m % tm == 0
    s = 0 if which == 0 else 2
    in_specs = [pl.BlockSpec((tm, D_MODEL), lambda i: (i, 0)),
                pl.BlockSpec((1, 1, N_MOD * D_MODEL), _mod_index(layer, latent, tm, seq)),
                pl.BlockSpec((1, 1, D_MODEL), lambda i: (layer * 3 + s, 0, 0)),
                _resident((1, 1, D_MODEL, 2 * FFN_DIM), lambda i: (layer, which, 0, 0)),
                _resident((1, 1, FFN_DIM, D_MODEL), lambda i: (layer, which, 0, 0))]
    args = [x, mods, norm_g, w13, w2]
    n_pre = 0
    if pre is not None:
        ys, ws, w_layer = pre
        n_pre = len(ys)
        for y in ys:
            in_specs.append(pl.BlockSpec((tm, y.shape[1]), lambda i: (i, 0)))
        for w, rows, blk in ws:
            in_specs.append(_resident((1, rows, D_MODEL), lambda i, blk=blk: (w_layer, blk, 0)))
        args += list(ys) + [w for w, _, _ in ws]
    if final_g is not None:
        in_specs.append(pl.BlockSpec((1, D_MODEL), lambda i: (0, 0)))
        args.append(final_g.reshape(1, D_MODEL))
    return pl.pallas_call(
        functools.partial(_ffn_kernel, s=s, n_pre=n_pre, final=final_g is not None),
        grid=(m // tm,),
        in_specs=in_specs,
        out_specs=pl.BlockSpec((tm, D_MODEL), lambda i: (i, 0)),
        out_shape=jax.ShapeDtypeStruct((m, D_MODEL), F32),
        compiler_params=_cparams(1),
        name="ffn",
    )(*args)


_EV_QA = 0
_EV_KA = NA_WIDTH
_EV_VA = 2 * NA_WIDTH
_EV_CQ = 3 * NA_WIDTH
_EV_CKV = _EV_CQ + MLA_Q_RANK
_EV_KR = _EV_CKV + MLA_KV_RANK
_EV_KRS = _EV_KR + LANES
_EV_COLS = _EV_KRS + LANES
MLA_QK_W = MLA_HEADS * LANES
LOG2E = 1.4426950408889634
NA_QSCALE = NA_HEAD_DIM ** -0.5 * LOG2E
MLA_QSCALE = (MLA_NOPE_DIM + MLA_ROPE_DIM) ** -0.5 * LOG2E


def _rope_swap_perm(width, half):
    j = np.arange(width)
    return np.where((j % (2 * half)) < half, j + half, j - half)


def _pack_even_weights(mix_w_in, mla_w_uq, mla_w_ukv):
    n_even = mix_w_in.shape[0]
    o = 3 * NA_WIDTH + MLA_Q_RANK + MLA_KV_RANK
    w_kr = mix_w_in[:, :, o:o + MLA_ROPE_DIM]
    w_krs = w_kr[:, :, _rope_swap_perm(MLA_ROPE_DIM, MLA_ROPE_DIM // 4)]
    pad = ((0, 0), (0, 0), (MLA_NOPE_DIM, LANES - MLA_NOPE_DIM - MLA_ROPE_DIM))
    w_in = jnp.concatenate([mix_w_in[:, :, :o], jnp.pad(w_kr, pad), jnp.pad(w_krs, pad)], axis=-1)

    uq = mla_w_uq.reshape(n_even, MLA_Q_RANK, MLA_HEADS, MLA_NOPE_DIM + MLA_ROPE_DIM)
    uq_rope = uq[..., MLA_NOPE_DIM:]
    uq_rope_s = uq_rope[..., _rope_swap_perm(MLA_ROPE_DIM, MLA_ROPE_DIM // 4)]
    zpad = LANES - MLA_NOPE_DIM - MLA_ROPE_DIM
    w_uq = jnp.pad(uq, ((0, 0), (0, 0), (0, 0), (0, zpad))).reshape(n_even, MLA_Q_RANK, MLA_QK_W)
    w_uqs = jnp.pad(uq_rope_s, ((0, 0), (0, 0), (0, 0), (MLA_NOPE_DIM, zpad)))
    w_uqs = w_uqs.reshape(n_even, MLA_Q_RANK, MLA_QK_W)

    ukv = mla_w_ukv.reshape(n_even, MLA_KV_RANK, MLA_HEADS, MLA_NOPE_DIM + MLA_V_DIM)
    w_uk = jnp.pad(ukv[..., :MLA_NOPE_DIM], ((0, 0), (0, 0), (0, 0), (0, LANES - MLA_NOPE_DIM)))
    w_uk = w_uk.reshape(n_even, MLA_KV_RANK, MLA_QK_W)
    w_uv = ukv[..., MLA_NOPE_DIM:].reshape(n_even, MLA_KV_RANK, MLA_OUT)
    return (w_in.astype(BF16), w_uq.astype(BF16), w_uqs.astype(BF16), w_uk.astype(BF16),
            w_uv.astype(BF16))


def _mla_rope_tables(seq):
    half = MLA_ROPE_DIM // 4
    t = jnp.arange(seq)
    j = np.arange(MLA_ROPE_DIM)
    pos = jnp.where((j // (2 * half))[None, :] == 0, (t // GRID_W)[:, None], (t % GRID_W)[:, None])
    inv = ROPE_BASE ** (-jnp.arange(half, dtype=F32) / half)
    ang = pos.astype(F32) * inv[j % half][None, :]
    sign = np.where((j % (2 * half)) < half, -1.0, 1.0).astype(np.float32)
    cos = jnp.concatenate([jnp.ones((seq, MLA_NOPE_DIM), F32), jnp.cos(ang),
                           jnp.ones((seq, LANES - MLA_NOPE_DIM - MLA_ROPE_DIM), F32)], axis=1)
    sin = jnp.concatenate([jnp.zeros((seq, MLA_NOPE_DIM), F32), jnp.sin(ang) * sign[None, :],
                           jnp.zeros((seq, LANES - MLA_NOPE_DIM - MLA_ROPE_DIM), F32)], axis=1)
    return cos, sin


def _even_proj_kernel(*refs, latent, n_alias):
    (x_ref, mod_ref, g_ref, win_ref, qn_ref, kvn_ref, wuq_ref, wuqs_ref, wuk_ref, wuv_ref) = refs[:10]
    if latent:
        cos_ref, sin_ref = refs[10:12]
        qa_ref, ka_ref, va_ref, qm_ref, km_ref, vm_ref = refs[12:]
    else:
        qa_ref, ka_ref, va_ref, qm_ref, km_ref, vm_ref, ckv_ref, kr_ref = refs[10 + n_alias:]

    def w_in(lo, width):
        return win_ref[0, :, lo:lo + width]

    h = _modulated(x_ref[...], mod_ref, g_ref, 1).astype(BF16)
    qa_ref[...] = (_dot(h, w_in(_EV_QA, NA_WIDTH)) * NA_QSCALE).astype(qa_ref.dtype)
    ka = _dot(h, w_in(_EV_KA, NA_WIDTH))
    va = _dot(h, w_in(_EV_VA, NA_WIDTH))
    if latent:
        ka_ref[...] = ka.astype(BF16)
        va_ref[...] = va.astype(BF16)
    else:
        ka_ref[:, 0] = ka.reshape(ka_ref.shape[0], ka_ref.shape[2], NA_WIDTH)
        va_ref[:, 0] = va.reshape(va_ref.shape[0], va_ref.shape[2], NA_WIDTH)

    cqn = _rms(_dot(h, w_in(_EV_CQ, MLA_Q_RANK)), qn_ref[0]).astype(BF16)
    ckvn = _rms(_dot(h, w_in(_EV_CKV, MLA_KV_RANK)), kvn_ref[0])
    kr = _dot(h, w_in(_EV_KR, LANES))
    if latent:
        cos = cos_ref[...]
        sin = sin_ref[...]
        kr = kr * cos + _dot(h, w_in(_EV_KRS, LANES)) * sin
    else:
        ckv_ref[:, 0] = ckvn.reshape(ckv_ref.shape[0], ckv_ref.shape[2], MLA_KV_RANK)
        kr_ref[:, 0] = kr[:, MLA_NOPE_DIM:MLA_NOPE_DIM + MLA_ROPE_DIM].reshape(
            kr_ref.shape[0], kr_ref.shape[2], MLA_ROPE_DIM)
    ckvb = ckvn.astype(BF16)

    qm = _dot(cqn, wuq_ref[0])
    km = _dot(ckvb, wuk_ref[0])
    if latent:
        qms = _dot(cqn, wuqs_ref[0])
    for hd in range(MLA_HEADS):
        sl = slice(hd * LANES, (hd + 1) * LANES)
        q_h = qm[:, sl]
        if latent:
            q_h = q_h * cos + qms[:, sl] * sin
        qm_ref[:, sl] = (q_h * MLA_QSCALE).astype(BF16)
        km_ref[:, sl] = (km[:, sl] + kr).astype(BF16)
    vm_ref[...] = _dot(ckvb, wuv_ref[0]).astype(BF16)


def _even_proj(x, mods, norm_g, packed, q_norm, kv_norm, layer, latent, seq, n_even, rope=None, caches=None):
    w_in, w_uq, w_uqs, w_uk, w_uv = packed
    i_even = layer // 2
    m = x.shape[0]
    tm = 512
    bt = tm // seq if not latent else 0
    tok = lambda i: (i, 0)
    in_specs = [pl.BlockSpec((tm, D_MODEL), tok),
                pl.BlockSpec((1, 1, N_MOD * D_MODEL), _mod_index(layer, latent, tm, seq)),
                pl.BlockSpec((1, 1, D_MODEL), lambda i: (layer * 3 + 1, 0, 0)),
                _resident((1, D_MODEL, _EV_COLS), lambda i: (i_even, 0, 0)),
                pl.BlockSpec((1, 1, MLA_Q_RANK), lambda i: (i_even, 0, 0)),
                pl.BlockSpec((1, 1, MLA_KV_RANK), lambda i: (i_even, 0, 0)),
                _resident((1, MLA_Q_RANK, MLA_QK_W), lambda i: (i_even, 0, 0)),
                _resident((1, MLA_Q_RANK, MLA_QK_W), lambda i: (i_even, 0, 0)),
                _resident((1, MLA_KV_RANK, MLA_QK_W), lambda i: (i_even, 0, 0)),
                _resident((1, MLA_KV_RANK, MLA_OUT), lambda i: (i_even, 0, 0))]
    args = [x, mods, norm_g, w_in, q_norm, kv_norm, w_uq, w_uqs, w_uk, w_uv]
    bf = lambda w: jax.ShapeDtypeStruct((m, w), BF16)
    aliases = {}
    if latent:
        nt = seq // tm
        in_specs += [pl.BlockSpec((tm, LANES), lambda i: (i % nt, 0))] * 2
        args += list(rope)
        out_shape = [bf(NA_WIDTH), bf(NA_WIDTH), bf(NA_WIDTH), bf(MLA_QK_W), bf(MLA_QK_W), bf(MLA_OUT)]
        out_specs = [pl.BlockSpec((tm, s.shape[1]), tok) for s in out_shape]
    else:
        nb = m // seq
        cache = lambda w: jax.ShapeDtypeStruct((nb, n_even, seq, w), F32)
        cspec = lambda w: pl.BlockSpec((bt, 1, seq, w), lambda i: (i, i_even, 0, 0))
        out_shape = [bf(NA_WIDTH), cache(NA_WIDTH), cache(NA_WIDTH), bf(MLA_QK_W), bf(MLA_QK_W),
                     bf(MLA_OUT), cache(MLA_KV_RANK), cache(MLA_ROPE_DIM)]
        out_specs = [pl.BlockSpec((tm, NA_WIDTH), tok), cspec(NA_WIDTH), cspec(NA_WIDTH),
                     pl.BlockSpec((tm, MLA_QK_W), tok), pl.BlockSpec((tm, MLA_QK_W), tok),
                     pl.BlockSpec((tm, MLA_OUT), tok), cspec(MLA_KV_RANK), cspec(MLA_ROPE_DIM)]
        if caches is not None:
            for arr, out_idx in zip(caches, (1, 2, 6, 7)):
                aliases[len(args)] = out_idx
                in_specs.append(pl.BlockSpec(memory_space=pl.ANY))
                args.append(arr)
    return pl.pallas_call(
        functools.partial(_even_proj_kernel, latent=latent, n_alias=len(aliases)),
        grid=(m // tm,),
        in_specs=in_specs,
        out_specs=out_specs,
        out_shape=out_shape,
        input_output_aliases=aliases,
        compiler_params=_cparams(1),
        name="even_proj_latent" if latent else "even_proj_context",
    )(*args)


def _mla_cache_kernel(ckv_ref, kr_ref, wuk_ref, wuv_ref, km_ref, vm_ref):
    ckvb = ckv_ref[0, 0].astype(BF16)
    km = _dot(ckvb, wuk_ref[0])
    kr = kr_ref[...]
    for hd in range(MLA_HEADS):
        sl = slice(hd * LANES, (hd + 1) * LANES)
        km_ref[:, sl] = (km[:, sl] + kr).astype(BF16)
    vm_ref[...] = _dot(ckvb, wuv_ref[0]).astype(BF16)


def _mla_cache_keys(cache_ckv, kr_padded, w_uk, w_uv, i_even):
    nb, _, past, _ = cache_ckv.shape
    return pl.pallas_call(
        _mla_cache_kernel,
        grid=(nb,),
        in_specs=[pl.BlockSpec((1, 1, past, MLA_KV_RANK), lambda b: (b, i_even, 0, 0)),
                  pl.BlockSpec((past, LANES), lambda b: (b, 0)),
                  _resident((1, MLA_KV_RANK, MLA_QK_W), lambda b: (i_even, 0, 0)),
                  _resident((1, MLA_KV_RANK, MLA_OUT), lambda b: (i_even, 0, 0))],
        out_specs=[pl.BlockSpec((past, MLA_QK_W), lambda b: (b, 0)),
                   pl.BlockSpec((past, MLA_OUT), lambda b: (b, 0))],
        out_shape=[jax.ShapeDtypeStruct((nb * past, MLA_QK_W), BF16),
                   jax.ShapeDtypeStruct((nb * past, MLA_OUT), BF16)],
        compiler_params=_cparams(1),
        name="mla_cache_keys",
    )(cache_ckv, kr_padded, w_uk, w_uv)


def _half_masks():
    lane = lax.broadcasted_iota(jnp.int32, (1, LANES), 1)
    lo = lane < (LANES // 2)
    return lo, jnp.logical_not(lo)


def _scores_pass(q, key_blocks, bias_blocks, s_ref, kb):
    mx = None
    for j, (k_blk, bias) in enumerate(zip(key_blocks, bias_blocks)):
        sc = _dot_nt(q, k_blk())
        if bias is not None:
            sc = sc + bias()
        s_ref[:, j * kb:(j + 1) * kb] = sc
        for g in range(kb // LANES):
            part = sc[:, g * LANES:(g + 1) * LANES]
            mx = part if mx is None else jnp.maximum(mx, part)
    return jnp.max(mx, axis=-1, keepdims=True)


def _values_pass(s_ref, m, value_blocks, kb):
    ls = None
    acc = None
    for j, v_blk in enumerate(value_blocks):
        p = jnp.exp2(s_ref[:, j * kb:(j + 1) * kb] - m)
        for g in range(kb // LANES):
            part = p[:, g * LANES:(g + 1) * LANES]
            ls = part if ls is None else ls + part
        o = _dot(p.astype(BF16), v_blk())
        acc = o if acc is None else acc + o
    return acc / jnp.sum(ls, axis=-1, keepdims=True)


def _attention_chains(chains, s_scr, kb=None):
    kb = ATT_KB if kb is None else kb
    depth = s_scr.shape[0]
    outs = []
    maxes = {}
    for i in range(len(chains) + depth - 1):
        if i < len(chains):
            q, key_blocks, _, bias_blocks = chains[i]
            maxes[i] = _scores_pass(q(), key_blocks, bias_blocks, s_scr.at[i % depth], kb)
        j = i - (depth - 1)
        if j >= 0:
            outs.append(_values_pass(s_scr.at[j % depth], maxes.pop(j), chains[j][2], kb))
    return outs


def _rows(ref, start, size, lanes):
    return lambda: ref[start:start + size, lanes]


def _ctx_attn_kernel(qa_ref, ka_ref, va_ref, qm_ref, km_ref, vm_ref, o_ref, s_scr, kv_scr):
    lo, hi = _half_masks()
    zero = jnp.zeros((), BF16)
    seq = qa_ref.shape[0]
    kb_sz = min(ATT_KB, seq)
    qs_sz = min(ATT_QS, seq)
    kv_scr[0] = ka_ref[0, 0].astype(BF16)
    kv_scr[1] = va_ref[0, 0].astype(BF16)
    chains = []
    for g in range((NA_WIDTH + MLA_OUT) // LANES):
        mla = g >= NA_WIDTH // LANES
        gl = g - NA_WIDTH // LANES if mla else g
        sl = slice(gl * LANES, (gl + 1) * LANES)
        for qs in range(seq // qs_sz):
            rows = slice(qs * qs_sz, (qs + 1) * qs_sz)
            for t, msk in enumerate((lo, hi)):
                if mla:
                    hsl = slice((2 * gl + t) * LANES, (2 * gl + t + 1) * LANES)
                    q = lambda rows=rows, hsl=hsl: qm_ref[rows, hsl]
                    kb = [_rows(km_ref, j * kb_sz, kb_sz, hsl) for j in range(seq // kb_sz)]
                    vb = [_rows(vm_ref, j * kb_sz, kb_sz, sl) for j in range(seq // kb_sz)]
                else:
                    q = lambda rows=rows, sl=sl, msk=msk: jnp.where(msk, qa_ref[rows, sl], zero)
                    kb = [_rows(kv_scr.at[0], j * kb_sz, kb_sz, sl) for j in range(seq // kb_sz)]
                    vb = [_rows(kv_scr.at[1], j * kb_sz, kb_sz, sl) for j in range(seq // kb_sz)]
                chains.append((q, kb, vb, [None] * len(kb)))
    outs = _attention_chains(chains, s_scr, kb_sz)
    n = 0
    for g in range((NA_WIDTH + MLA_OUT) // LANES):
        for qs in range(seq // qs_sz):
            o_ref[qs * qs_sz:(qs + 1) * qs_sz, g * LANES:(g + 1) * LANES] = (
                jnp.where(lo, outs[n], outs[n + 1]).astype(BF16))
            n += 2


def _ctx_attention(qa, ka, va, qm, km, vm, seq, i_even):
    m = qa.shape[0]
    nb = m // seq
    tok = lambda b: (b, 0)
    cache = lambda b: (b, i_even, 0, 0)
    return pl.pallas_call(
        _ctx_attn_kernel,
        grid=(nb,),
        in_specs=[pl.BlockSpec((seq, NA_WIDTH), tok),
                  pl.BlockSpec((1, 1, seq, NA_WIDTH), cache),
                  pl.BlockSpec((1, 1, seq, NA_WIDTH), cache),
                  pl.BlockSpec((seq, MLA_QK_W), tok),
                  pl.BlockSpec((seq, MLA_QK_W), tok),
                  pl.BlockSpec((seq, MLA_OUT), tok)],
        out_specs=pl.BlockSpec((seq, NA_WIDTH + MLA_OUT), tok),
        out_shape=jax.ShapeDtypeStruct((m, NA_WIDTH + MLA_OUT), BF16),
        scratch_shapes=[pltpu.VMEM((2, min(ATT_QS, seq), seq), F32), pltpu.VMEM((2, seq, NA_WIDTH), BF16)],
        compiler_params=_cparams(1),
        name="context_attention",
    )(qa, ka, va, qm, km, vm)


def _lat_mla_kernel(q_ref, kc_ref, kl_ref, vc_ref, vl_ref, o_ref, s_scr):
    lo, _ = _half_masks()
    past, seq = kc_ref.shape[0], kl_ref.shape[0]
    all_lanes = slice(0, LANES)
    kbs = []
    for t in range(2):
        sl = slice(t * LANES, (t + 1) * LANES)
        kbs.append([_rows(kc_ref, j * ATT_KB, ATT_KB, sl) for j in range(past // ATT_KB)]
                   + [_rows(kl_ref, j * ATT_KB, ATT_KB, sl) for j in range(seq // ATT_KB)])
    vb = ([_rows(vc_ref, j * ATT_KB, ATT_KB, all_lanes) for j in range(past // ATT_KB)]
          + [_rows(vl_ref, j * ATT_KB, ATT_KB, all_lanes) for j in range(seq // ATT_KB)])
    n_qs = ATT_TQ // ATT_QS

    def tile(i, carry):
        base = pl.multiple_of(i * ATT_TQ, ATT_TQ)
        chains = []
        for qs in range(n_qs):
            rows = pl.ds(base + qs * ATT_QS, ATT_QS)
            for t in range(2):
                sl = slice(t * LANES, (t + 1) * LANES)
                chains.append((lambda rows=rows, sl=sl: q_ref[rows, sl], kbs[t], vb, [None] * len(vb)))
        outs = _attention_chains(chains, s_scr)
        for qs in range(n_qs):
            o_ref[pl.ds(base + qs * ATT_QS, ATT_QS), :] = (
                jnp.where(lo, outs[2 * qs], outs[2 * qs + 1]).astype(BF16))
        return carry

    lax.fori_loop(0, seq // ATT_TQ, tile, 0)


def _latent_mla_attention(qm, km_ctx, km_lat, vm_ctx, vm_lat, seq, past):
    m = qm.shape[0]
    nb = m // seq
    pair = 2 * LANES
    return pl.pallas_call(
        _lat_mla_kernel,
        grid=(nb, MLA_HEADS // 2),
        in_specs=[pl.BlockSpec((seq, pair), lambda b, g: (b, g)),
                  pl.BlockSpec((past, pair), lambda b, g: (b, g)),
                  pl.BlockSpec((seq, pair), lambda b, g: (b, g)),
                  pl.BlockSpec((past, LANES), lambda b, g: (b, g)),
                  pl.BlockSpec((seq, LANES), lambda b, g: (b, g))],
        out_specs=pl.BlockSpec((seq, LANES), lambda b, g: (b, g)),
        out_shape=jax.ShapeDtypeStruct((m, MLA_OUT), BF16),
        scratch_shapes=[pltpu.VMEM((2, ATT_QS, past + seq), F32)],
        compiler_params=_cparams(2),
        name="latent_mla_attention",
    )(qm, km_ctx, km_lat, vm_ctx, vm_lat)


def _na_window_start(first_row, rows, clip=np.clip):
    r0 = clip(first_row - NA_ROWS // 2, 0, rows - NA_ROWS)
    return clip(r0 - (NA_WROWS - NA_ROWS) // 2, 0, rows - NA_WROWS)


def _na_bias_kernel(rpb_ref, o_ref, *, rows):
    hd = pl.program_id(0)
    kr = min(NA_ROWS, rows)
    n_dc = 2 * NA_COLS - 1
    band = ATT_QS // GRID_W
    c = lax.broadcasted_iota(jnp.int32, (GRID_W, LANES), 0)
    lane = lax.broadcasted_iota(jnp.int32, (GRID_W, LANES), 1)
    n = lane % GRID_W
    c0 = jnp.clip(c - NA_COLS // 2, 0, GRID_W - NA_COLS)
    in_win = (n >= c0) & (n < c0 + NA_COLS)
    dc = n - c + NA_COLS - 1
    left = lane < GRID_W
    neg = jnp.full((GRID_W, LANES), NEG_BIG, F32)
    col_tables = []
    for dr in range(2 * NA_ROWS - 1):
        t = neg
        for j in range(n_dc):
            t = jnp.where(dc == j, rpb_ref[hd * (2 * NA_ROWS - 1) * n_dc + dr * n_dc + j] * LOG2E, t)
        col_tables.append(jnp.where(in_win, t, neg))
    for rb in range(rows // NA_QROWS):
        for rl in range(NA_QROWS):
            r = rb * NA_QROWS + rl
            r0 = min(max(r - kr // 2, 0), rows - kr)
            start = int(_na_window_start(r - rl % band, rows))
            assert start <= r0 and r0 + kr <= start + NA_WROWS
            for pr in range(NA_WROWS // 2):
                halves = []
                for key_row in (start + 2 * pr, start + 2 * pr + 1):
                    ok = r0 <= key_row < r0 + kr
                    halves.append(col_tables[key_row - r + NA_ROWS - 1] if ok else neg)
                o_ref[rb, 0, rl * GRID_W:(rl + 1) * GRID_W, pr * LANES:(pr + 1) * LANES] = (
                    jnp.where(left, halves[0], halves[1]))


def _na_bias(rpb, rows):
    n_rb = rows // NA_QROWS
    flat = rpb.reshape(-1)
    return pl.pallas_call(
        functools.partial(_na_bias_kernel, rows=rows),
        grid=(NA_HEADS,),
        in_specs=[pl.BlockSpec(memory_space=pltpu.SMEM)],
        out_specs=pl.BlockSpec((n_rb, 1, NA_QROWS * GRID_W, NA_WROWS * GRID_W), lambda h: (0, h, 0, 0)),
        out_shape=jax.ShapeDtypeStruct((n_rb, NA_HEADS, NA_QROWS * GRID_W, NA_WROWS * GRID_W), F32),
        compiler_params=_cparams(1),
        name="na_bias",
    )(flat)


def _lat_na_kernel(q_ref, k_ref, v_ref, kc_ref, vc_ref, bias_ref, o_ref, s_scr, kv_scr, *, rows):
    lo, hi = _half_masks()
    zero = jnp.zeros((), BF16)
    rb = pl.program_id(0)
    nk = NA_WROWS * GRID_W
    past = kc_ref.shape[2]
    all_lanes = slice(0, LANES)
    n_qs = q_ref.shape[1] // ATT_QS
    starts = [pl.multiple_of(_na_window_start(rb * NA_QROWS + qs * (ATT_QS // GRID_W), rows, jnp.clip) * GRID_W,
                             GRID_W) for qs in range(n_qs)]

    def batch(b, carry):
        kv_scr[0] = kc_ref[b, 0].astype(BF16)
        kv_scr[1] = vc_ref[b, 0].astype(BF16)

        def win(ref, start, j):
            return lambda: ref[b, pl.ds(start + j * ATT_KB, ATT_KB), :]

        ctx_k = [_rows(kv_scr.at[0], j * ATT_KB, ATT_KB, all_lanes) for j in range(past // ATT_KB)]
        ctx_v = [_rows(kv_scr.at[1], j * ATT_KB, ATT_KB, all_lanes) for j in range(past // ATT_KB)]
        chains = []
        for qs in range(n_qs):
            qrows = slice(qs * ATT_QS, (qs + 1) * ATT_QS)
            kb = [win(k_ref, starts[qs], j) for j in range(nk // ATT_KB)] + ctx_k
            vb = [win(v_ref, starts[qs], j) for j in range(nk // ATT_KB)] + ctx_v
            for t, msk in enumerate((lo, hi)):
                bias = ([(lambda t=t, j=j, qrows=qrows: bias_ref[0, t, qrows, j * ATT_KB:(j + 1) * ATT_KB])
                         for j in range(nk // ATT_KB)] + [None] * (past // ATT_KB))
                chains.append((lambda qrows=qrows, msk=msk: jnp.where(msk, q_ref[b, qrows, :], zero),
                               kb, vb, bias))
        outs = _attention_chains(chains, s_scr)
        for qs in range(n_qs):
            o_ref[b, qs * ATT_QS:(qs + 1) * ATT_QS, :] = (
                jnp.where(lo, outs[2 * qs], outs[2 * qs + 1]).astype(BF16))
        return carry

    lax.fori_loop(0, q_ref.shape[0], batch, 0)


def _latent_na_attention(qa, ka, va, cache_k, cache_v, bias, i_even, seq):
    m = qa.shape[0]
    nb = m // seq
    rows = seq // GRID_W
    n_rb = rows // NA_QROWS
    tq = NA_QROWS * GRID_W
    past = cache_k.shape[2]
    per_batch = lambda a: a.reshape(nb, seq, NA_WIDTH)
    out = pl.pallas_call(
        functools.partial(_lat_na_kernel, rows=rows),
        grid=(n_rb, NA_HEADS // 2),
        in_specs=[pl.BlockSpec((nb, tq, LANES), lambda r, g: (0, r, g)),
                  pl.BlockSpec((nb, seq, LANES), lambda r, g: (0, 0, g)),
                  pl.BlockSpec((nb, seq, LANES), lambda r, g: (0, 0, g)),
                  pl.BlockSpec((nb, 1, past, LANES), lambda r, g: (0, i_even, 0, g)),
                  pl.BlockSpec((nb, 1, past, LANES), lambda r, g: (0, i_even, 0, g)),
                  pl.BlockSpec((1, 2, tq, NA_WROWS * GRID_W), lambda r, g: (r, g, 0, 0))],
        out_specs=pl.BlockSpec((nb, tq, LANES), lambda r, g: (0, r, g)),
        out_shape=jax.ShapeDtypeStruct((nb, seq, NA_WIDTH), BF16),
        scratch_shapes=[pltpu.VMEM((2, ATT_QS, NA_WROWS * GRID_W + past), F32),
                        pltpu.VMEM((2, past, LANES), BF16)],
        compiler_params=_cparams(2),
        name="latent_na_attention",
    )(per_batch(qa), per_batch(ka), per_batch(va), cache_k, cache_v, bias)
    return out.reshape(m, NA_WIDTH)


_RET_NK = RET_HEADS * RET_DK
_RET_NV = RET_HEADS * RET_DV


def _ret_proj_kernel(x_ref, mod_ref, g_ref, w_ref, cos_ref, sin_ref, q_ref, k_ref, v_ref, gate_ref):
    h = _modulated(x_ref[...], mod_ref, g_ref, 1).astype(BF16)

    def rope(y):
        parts = [pltpu.roll(y[:, g * LANES:(g + 1) * LANES], LANES // 2, 1) for g in range(RET_DK // LANES)]
        return y * cos_ref[...] + jnp.concatenate(parts, axis=1) * sin_ref[...]

    for hd in range(RET_HEADS):
        sl = slice(hd * RET_DK, (hd + 1) * RET_DK)
        q_ref[:, sl] = rope(_dot(h, w_ref[0, :, sl])).astype(BF16)
        k = _dot(h, w_ref[0, :, _RET_NK + hd * RET_DK:_RET_NK + (hd + 1) * RET_DK]) * (RET_DK ** -0.5)
        k_ref[:, sl] = rope(k)
    step = 512
    for j in range(_RET_NV // step):
        o = 2 * _RET_NK + j * step
        v_ref[:, j * step:(j + 1) * step] = _dot(h, w_ref[0, :, o:o + step]).astype(BF16)
    for j in range(2 * _RET_NV // step):
        o = 2 * _RET_NK + _RET_NV + j * step
        gate_ref[:, j * step:(j + 1) * step] = _dot(h, w_ref[0, :, o:o + step])


def _ret_proj_latent(x, mods, norm_g, w_in, layer, seq, rope):
    m = x.shape[0]
    tm = 256
    i_odd = layer // 2
    tok = lambda i: (i, 0)
    n_in = 2 * _RET_NK + 3 * _RET_NV
    nt = seq // tm
    return pl.pallas_call(
        _ret_proj_kernel,
        grid=(m // tm,),
        in_specs=[pl.BlockSpec((tm, D_MODEL), tok),
                  pl.BlockSpec((1, 1, N_MOD * D_MODEL), _mod_index(layer, True, tm, seq)),
                  pl.BlockSpec((1, 1, D_MODEL), lambda i: (layer * 3 + 1, 0, 0)),
                  _resident((1, D_MODEL, n_in), lambda i: (i_odd, 0, 0)),
                  pl.BlockSpec((tm, RET_DK), lambda i: (i % nt, 0)),
                  pl.BlockSpec((tm, RET_DK), lambda i: (i % nt, 0))],
        out_specs=[pl.BlockSpec((tm, _RET_NK), tok),
                   pl.BlockSpec((tm, _RET_NK), tok),
                   pl.BlockSpec((tm, _RET_NV), tok),
                   pl.BlockSpec((tm, 2 * _RET_NV), tok)],
        out_shape=[jax.ShapeDtypeStruct((m, _RET_NK), BF16),
                   jax.ShapeDtypeStruct((m, _RET_NK), F32),
                   jax.ShapeDtypeStruct((m, _RET_NV), BF16),
                   jax.ShapeDtypeStruct((m, 2 * _RET_NV), F32)],
        compiler_params=_cparams(1),
        name="ret_proj_latent",
    )(x, mods, norm_g, w_in, *rope)


def _ret_rope_tables(seq):
    d = RET_DK // 2
    half = d // 2
    t = jnp.arange(seq)
    j = np.arange(RET_DK)
    pos = jnp.where((j // d)[None, :] == 0, (t // GRID_W)[:, None], (t % GRID_W)[:, None])
    inv = ROPE_BASE ** (-jnp.arange(half, dtype=F32) / half)
    ang = pos.astype(F32) * inv[j % half][None, :]
    sign = np.where((j % d) < half, -1.0, 1.0).astype(np.float32)
    return jnp.cos(ang), jnp.sin(ang) * sign[None, :]


def _head_norm(o):
    mu = jnp.mean(o, axis=-1, keepdims=True)
    d = o - mu
    return d * lax.rsqrt(jnp.mean(d * d, axis=-1, keepdims=True) + EPS)


def _ret_decays(lg, dirn, c):
    ii = lax.broadcasted_iota(jnp.int32, (c, c), 0)
    jj = lax.broadcasted_iota(jnp.int32, (c, c), 1)
    row = lax.broadcasted_iota(jnp.int32, (c, 1), 0).astype(F32)
    diff = (ii - jj if dirn == 0 else jj - ii).astype(F32)
    intra = jnp.where(diff >= 0, jnp.exp(lg * jnp.maximum(diff, 0.0)), 0.0)
    if dirn == 0:
        q_dec = jnp.exp(lg * (row + 1.0))
        k_dec = jnp.exp(lg * (c - 1.0 - row))
    else:
        q_dec = jnp.exp(lg * (c - row))
        k_dec = jnp.exp(lg * row)
    return intra, q_dec, k_dec, jnp.exp(lg * float(c))


def _ret_ctx_kernel(*refs, n_alias):
    ld_ref, x_ref, mod_ref, g_ref, w_ref = refs[:5]
    y_ref, st_ref = refs[5 + n_alias:]
    c = x_ref.shape[0]
    h = _modulated(x_ref[...], mod_ref, g_ref, 1).astype(BF16)
    for hd in range(RET_HEADS):
        ksl = slice(hd * RET_DK, (hd + 1) * RET_DK)
        qb = _dot(h, w_ref[0, :, ksl]).astype(BF16)
        k = _dot(h, w_ref[0, :, _RET_NK + hd * RET_DK:_RET_NK + (hd + 1) * RET_DK]) * (RET_DK ** -0.5)
        vo = 2 * _RET_NK + hd * RET_DV
        v = _dot(h, w_ref[0, :, vo:vo + RET_DV]).astype(BF16)
        qk = _dot_nt(qb, k.astype(BF16))
        y = None
        for dirn in range(2):
            intra, _, k_dec, _ = _ret_decays(-jnp.abs(ld_ref[dirn, hd]), dirn, c)
            o = _dot((qk * intra).astype(BF16), v)
            st_ref[0, 0, dirn, hd] = _dot((k * k_dec).T.astype(BF16), v)
            go = 2 * _RET_NK + (1 + dirn) * _RET_NV + hd * RET_DV
            yd = _silu(_dot(h, w_ref[0, :, go:go + RET_DV])) * _head_norm(o)
            y = yd if y is None else y + yd
        y_ref[:, hd * RET_DV:(hd + 1) * RET_DV] = y.astype(BF16)


def _retention_context(x, mods, norm_g, w_in, log_decay, layer, seq, n_odd, prev_states=None):
    assert seq == RET_CHUNK
    m = x.shape[0]
    nb = m // seq
    i_odd = layer // 2
    n_in = 2 * _RET_NK + 3 * _RET_NV
    in_specs = [pl.BlockSpec(memory_space=pltpu.SMEM),
                pl.BlockSpec((seq, D_MODEL), lambda b: (b, 0)),
                pl.BlockSpec((1, 1, N_MOD * D_MODEL), _mod_index(layer, False, seq, seq)),
                pl.BlockSpec((1, 1, D_MODEL), lambda b: (layer * 3 + 1, 0, 0)),
                _resident((1, D_MODEL, n_in), lambda b: (i_odd, 0, 0))]
    args = [log_decay[i_odd], x, mods, norm_g, w_in]
    aliases = {}
    if prev_states is not None:
        aliases[len(args)] = 1
        in_specs.append(pl.BlockSpec(memory_space=pl.ANY))
        args.append(prev_states)
    return pl.pallas_call(
        functools.partial(_ret_ctx_kernel, n_alias=len(aliases)),
        grid=(nb,),
        in_specs=in_specs,
        out_specs=[pl.BlockSpec((seq, _RET_NV), lambda b: (b, 0)),
                   pl.BlockSpec((1, 1, 2, RET_HEADS, RET_DK, RET_DV), lambda b: (b, i_odd, 0, 0, 0, 0))],
        out_shape=[jax.ShapeDtypeStruct((m, _RET_NV), BF16),
                   jax.ShapeDtypeStruct((nb, n_odd, 2, RET_HEADS, RET_DK, RET_DV), F32)],
        input_output_aliases=aliases,
        compiler_params=_cparams(1),
        name="retention_context",
    )(*args)


def _ret_lat_kernel(ld_ref, q_ref, k_ref, v_ref, gf_ref, gb_ref, s0_ref, y_ref, s_scr, o_scr):
    hd = pl.program_id(1)
    c = RET_CHUNK
    n_chunks = q_ref.shape[0] // c
    decays = [_ret_decays(-jnp.abs(ld_ref[dirn, hd]), dirn, c) for dirn in range(2)]
    gates = (gf_ref, gb_ref)
    for dirn in range(2):
        s_scr[dirn] = s0_ref[0, 0, dirn, 0]

    def scan(first_visit):
        def body(t, carry):
            for dirn in range(2):
                intra, q_dec, k_dec, c_dec = decays[dirn]
                ci = t if dirn == 0 else n_chunks - 1 - t
                rows = pl.ds(pl.multiple_of(ci * c, c), c)
                qb = q_ref[rows, :]
                k = k_ref[rows, :]
                v = v_ref[rows, :]
                a = _dot_nt(qb, k.astype(BF16)) * intra
                s_prev = s_scr[dirn]
                o = _dot(a.astype(BF16), v) + _dot(qb, s_prev.astype(BF16)) * q_dec
                s_scr[dirn] = s_prev * c_dec + _dot((k * k_dec).T.astype(BF16), v)
                y = _silu(gates[dirn][rows, :]) * _head_norm(o)
                if first_visit:
                    o_scr[rows, :] = y
                else:
                    y_ref[rows, :] = (o_scr[rows, :] + y).astype(BF16)
            return carry
        return body

    lax.fori_loop(0, n_chunks // 2, scan(True), 0, unroll=2)
    lax.fori_loop(n_chunks // 2, n_chunks, scan(False), 0, unroll=2)


def _retention_latent(q, k, v, gates, log_decay, i_odd, seq, state):
    m = q.shape[0]
    nb = m // seq
    assert (seq // RET_CHUNK) % 4 == 0
    return pl.pallas_call(
        _ret_lat_kernel,
        grid=(nb, RET_HEADS),
        in_specs=[pl.BlockSpec(memory_space=pltpu.SMEM),
                  pl.BlockSpec((seq, RET_DK), lambda b, h: (b, h)),
                  pl.BlockSpec((seq, RET_DK), lambda b, h: (b, h)),
                  pl.BlockSpec((seq, RET_DV), lambda b, h: (b, h)),
                  pl.BlockSpec((seq, RET_DV), lambda b, h: (b, h)),
                  pl.BlockSpec((seq, RET_DV), lambda b, h: (b, RET_HEADS + h)),
                  pl.BlockSpec((1, 1, 2, 1, RET_DK, RET_DV), lambda b, h: (b, i_odd, 0, h, 0, 0))],
        out_specs=pl.BlockSpec((seq, RET_DV), lambda b, h: (b, h)),
        out_shape=jax.ShapeDtypeStruct((m, _RET_NV), BF16),
        scratch_shapes=[pltpu.VMEM((2, RET_DK, RET_DV), F32), pltpu.VMEM((seq, RET_DV), F32)],
        compiler_params=_cparams(2),
        name="retention_latent",
    )(log_decay[i_odd], q, k, v, gates, gates, state)


def kernel(x_prompt, x_sample, c, cache_na_k, cache_na_v, cache_mla_ckv, cache_mla_krope, state_ret,
           c_ctx, norm_g, ada_w, ada_b, ffn_w13, ffn_w2, mix_w_in, mla_q_norm, mla_kv_norm, mla_w_uq,
           mla_w_ukv, na_rpb, mix_w_out, ret_w_in, ret_log_decay, ret_w_out, final_norm_g):
    batch, seq, _ = x_prompt.shape
    dec_batch, dec_seq, _ = x_sample.shape
    past = cache_na_k.shape[2]
    n_even = mix_w_in.shape[0]
    assert dec_batch + 1 <= MOD_ROWS and seq == RET_CHUNK and dec_seq % RET_CHUNK == 0
    assert (dec_seq // GRID_W) % NA_QROWS == 0 and dec_seq // GRID_W >= NA_WROWS

    w13 = ffn_w13.astype(BF16)
    w2 = ffn_w2.astype(BF16)
    w_out_even = mix_w_out.astype(BF16)
    w_in_ret = ret_w_in.astype(BF16)
    w_out_ret = ret_w_out.astype(BF16)
    packed = _pack_even_weights(mix_w_in, mla_w_uq, mla_w_ukv)
    q_norm = mla_q_norm.reshape(n_even, 1, MLA_Q_RANK)
    kv_norm = mla_kv_norm.reshape(n_even, 1, MLA_KV_RANK)
    norm_g3 = norm_g.reshape(DEPTH * 3, 1, D_MODEL)
    mla_rope = _mla_rope_tables(dec_seq)
    ret_rope = _ret_rope_tables(dec_seq)
    cache_k = cache_na_k.reshape(dec_batch, n_even, past, NA_WIDTH)
    cache_v = cache_na_v.reshape(dec_batch, n_even, past, NA_WIDTH)

    cvec = jnp.concatenate([c_ctx[None, :], c, jnp.zeros((MOD_ROWS - 1 - dec_batch, D_MODEL), F32)], axis=0)
    mods = _ada_all(cvec, ada_w, ada_b).reshape(DEPTH * MOD_ROWS, 1, N_MOD * D_MODEL)

    xp = x_prompt.reshape(batch * seq, D_MODEL)
    xs = x_sample.reshape(dec_batch * dec_seq, D_MODEL)
    n_odd = ret_w_in.shape[0]
    caches = None
    states = None
    for layer in range(DEPTH):
        i = layer // 2
        xp = _ffn(xp, mods, norm_g3, w13, w2, layer, 0, False, seq)
        xs = _ffn(xs, mods, norm_g3, w13, w2, layer, 0, True, dec_seq)
        if layer % 2 == 0:
            qa, ka, va, qm, km, vm, ckv, kr = _even_proj(
                xp, mods, norm_g3, packed, q_norm, kv_norm, layer, False, seq, n_even, caches=caches)
            caches = (ka, va, ckv, kr)
            op = _ctx_attention(qa, ka, va, qm, km, vm, seq, i)
            pre_p = ([op], [(w_out_even, NA_WIDTH + MLA_OUT, 0)], i)

            qa, ka, va, qm, km, vm = _even_proj(
                xs, mods, norm_g3, packed, q_norm, kv_norm, layer, True, dec_seq, n_even, rope=mla_rope)
            kr_pad = jnp.pad(cache_mla_krope[:, i].reshape(dec_batch * past, MLA_ROPE_DIM),
                             ((0, 0), (MLA_NOPE_DIM, LANES - MLA_NOPE_DIM - MLA_ROPE_DIM)))
            km_ctx, vm_ctx = _mla_cache_keys(cache_mla_ckv, kr_pad, packed[3], packed[4], i)
            bias = _na_bias(na_rpb[i], dec_seq // GRID_W)
            oa = _latent_na_attention(qa, ka, va, cache_k, cache_v, bias, i, dec_seq)
            ob = _latent_mla_attention(qm, km_ctx, km, vm_ctx, vm, dec_seq, past)
            pre_s = ([oa, ob], [(w_out_even, NA_WIDTH, 0), (w_out_even, MLA_OUT, NA_WIDTH // MLA_OUT)], i)
        else:
            yp, states = _retention_context(xp, mods, norm_g3, w_in_ret, ret_log_decay, layer, seq, n_odd,
                                            prev_states=states)
            pre_p = ([yp], [(w_out_ret, _RET_NV, 0)], i)

            q, k, v, gates = _ret_proj_latent(xs, mods, norm_g3, w_in_ret, layer, dec_seq, ret_rope)
            ys = _retention_latent(q, k, v, gates, ret_log_decay, i, dec_seq, state_ret)
            pre_s = ([ys], [(w_out_ret, _RET_NV, 0)], i)
        last = final_norm_g if layer == DEPTH - 1 else None
        xp = _ffn(xp, mods, norm_g3, w13, w2, layer, 1, False, seq, pre=pre_p, final_g=last)
        xs = _ffn(xs, mods, norm_g3, w13, w2, layer, 1, True, dec_seq, pre=pre_s, final_g=last)

    y_prompt = xp.reshape(batch, seq, D_MODEL)
    y_sample = xs.reshape(dec_batch, dec_seq, D_MODEL)
    ka, va, ckv, kr = caches
    new_na_k = ka.reshape(batch, n_even, seq, NA_HEADS, NA_HEAD_DIM)
    new_na_v = va.reshape(batch, n_even, seq, NA_HEADS, NA_HEAD_DIM)
    return (y_prompt, y_sample, new_na_k, new_na_v, ckv, kr, states)
```

```python
import functools

import jax
import jax.numpy as jnp
import numpy as np
from jax import lax
from jax.experimental import pallas as pl
from jax.experimental.pallas import tpu as pltpu

F32 = jnp.float32
BF16 = jnp.bfloat16

D_MODEL = 1024
DEPTH = 4
GRID_W = 64
NA_HEADS = 8
NA_HEAD_DIM = 64
NA_ROWS = 8
NA_COLS = 16
MLA_HEADS = 8
MLA_Q_RANK = 384
MLA_KV_RANK = 256
MLA_NOPE_DIM = 64
MLA_ROPE_DIM = 32
MLA_V_DIM = 64
RET_HEADS = 4
RET_DK = D_MODEL // RET_HEADS
RET_DV = 2 * D_MODEL // RET_HEADS
FFN_DIM = ((8 * D_MODEL // 3 + 127) // 128) * 128
N_MOD = 9
ROPE_BASE = 10000.0
EPS = 1e-6
NA_WIDTH = NA_HEADS * NA_HEAD_DIM
MLA_OUT = MLA_HEADS * MLA_V_DIM

LANES = 128
MOD_ROWS = 8
NEG_BIG = -1e30
VMEM_LIMIT = 52 * 1024 * 1024

FFN_TF = 256
RET_CHUNK = 256
NA_QROWS = 8
NA_WROWS = 12
ATT_QS = 128
ATT_KB = 256
ATT_TQ = 512


def _cparams(n_axes):
    return pltpu.CompilerParams(dimension_semantics=("arbitrary",) * n_axes,
                                vmem_limit_bytes=VMEM_LIMIT)


def _resident(block, index_map):
    return pl.BlockSpec(block, index_map, pipeline_mode=pl.Buffered(1))


def _dot(a, b):
    return jnp.dot(a, b, preferred_element_type=F32)


def _dot_nt(a, b):
    return lax.dot_general(a, b, (((1,), (1,)), ((), ())), preferred_element_type=F32)


def _silu(x):
    return x * (0.5 * jnp.tanh(0.5 * x) + 0.5)


def _rms(x, g):
    return x * lax.rsqrt(jnp.mean(x * x, axis=-1, keepdims=True) + EPS) * g


def _modulated(x, mod_ref, g_ref, s):
    shift = mod_ref[0, :, 3 * s * D_MODEL:(3 * s + 1) * D_MODEL]
    scale = mod_ref[0, :, (3 * s + 1) * D_MODEL:(3 * s + 2) * D_MODEL]
    return _rms(x, g_ref[0]) * (1.0 + scale) + shift


def _gate(mod_ref, s):
    return mod_ref[0, :, (3 * s + 2) * D_MODEL:(3 * s + 3) * D_MODEL]


def _mod_index(layer, latent, tile_rows, seq):
    if not latent:
        return lambda i: (layer * MOD_ROWS, 0, 0)
    return lambda i: (layer * MOD_ROWS + 1 + (i * tile_rows) // seq, 0, 0)


def _ada_kernel(c_ref, w_ref, b_ref, o_ref):
    s = _silu(c_ref[...]).astype(BF16)
    o_ref[0] = _dot(s, w_ref[0].astype(BF16)) + b_ref[0]


def _ada_all(cvec, ada_w, ada_b):
    tn = 1024
    n = N_MOD * D_MODEL
    return pl.pallas_call(
        _ada_kernel,
        grid=(DEPTH, n // tn),
        in_specs=[pl.BlockSpec((MOD_ROWS, D_MODEL), lambda l, j: (0, 0)),
                  pl.BlockSpec((1, D_MODEL, tn), lambda l, j: (l, 0, j)),
                  pl.BlockSpec((1, 1, tn), lambda l, j: (l, 0, j))],
        out_specs=pl.BlockSpec((1, MOD_ROWS, tn), lambda l, j: (l, 0, j)),
        out_shape=jax.ShapeDtypeStruct((DEPTH, MOD_ROWS, n), F32),
        compiler_params=_cparams(2),
        name="adaln",
    )(cvec, ada_w, ada_b.reshape(DEPTH, 1, n))


def _ffn_kernel(*refs, s, n_pre, final, layer, which):
    x_ref, mod_ref, g_ref, w13_hbm, w2_hbm = refs[:5]
    pre_y = refs[5:5 + n_pre]
    pre_w = refs[5 + n_pre:5 + 2 * n_pre]
    fg_ref = refs[5 + 2 * n_pre] if final else None
    o_ref, w13_ref, w2_ref, stage13, stage2, sem = refs[-6:]
    n_chunks = FFN_DIM // FFN_TF

    def chunk_copies(j):
        slot = j % 2
        cols = pl.ds(j * FFN_TF, FFN_TF)
        gate_cols = pl.ds(FFN_DIM + j * FFN_TF, FFN_TF)
        return (pltpu.make_async_copy(w13_hbm.at[layer, which, :, cols], stage13.at[slot, 0], sem.at[slot, 0]),
                pltpu.make_async_copy(w13_hbm.at[layer, which, :, gate_cols], stage13.at[slot, 1], sem.at[slot, 1]),
                pltpu.make_async_copy(w2_hbm.at[layer, which, cols, :], stage2.at[slot], sem.at[slot, 2]))

    def body(stream_weights):
        if stream_weights:
            for cp in chunk_copies(0):
                cp.start()
        x = x_ref[...]
        if n_pre:
            mix = None
            for y_ref, w_ref in zip(pre_y, pre_w):
                y = _dot(y_ref[...], w_ref[0])
                mix = y if mix is None else mix + y
            x = x + _gate(mod_ref, 1) * mix
        h = _modulated(x, mod_ref, g_ref, s).astype(BF16)
        acc = jnp.zeros(x.shape, F32)
        for j in range(n_chunks):
            up = slice(j * FFN_TF, (j + 1) * FFN_TF)
            gate = slice(FFN_DIM + j * FFN_TF, FFN_DIM + (j + 1) * FFN_TF)
            if stream_weights:
                if j + 1 < n_chunks:
                    for cp in chunk_copies(j + 1):
                        cp.start()
                for cp in chunk_copies(j):
                    cp.wait()
                w13_ref[:, up] = stage13[j % 2, 0].astype(BF16)
                w13_ref[:, gate] = stage13[j % 2, 1].astype(BF16)
                w2_ref[up, :] = stage2[j % 2].astype(BF16)
            a = _dot(h, w13_ref[:, up])
            b = _dot(h, w13_ref[:, gate])
            u = (_silu(a) * b).astype(BF16)
            acc = acc + _dot(u, w2_ref[up, :])
        out = x + 0.5 * _gate(mod_ref, s) * acc
        if final:
            out = _rms(out, fg_ref[...])
        o_ref[...] = out

    first = pl.program_id(0) == 0
    pl.when(first)(functools.partial(body, True))
    pl.when(jnp.logical_not(first))(functools.partial(body, False))


def _ffn(x, mods, norm_g, w13, w2, layer, which, latent, seq, pre=None, final_g=None):
    tm = 512
    m = x.shape[0]
    assert m % tm == 0
    s = 0 if which == 0 else 2
    in_specs = [pl.BlockSpec((tm, D_MODEL), lambda i: (i, 0)),
                pl.BlockSpec((1, 1, N_MOD * D_MODEL), _mod_index(layer, latent, tm, seq)),
                pl.BlockSpec((1, 1, D_MODEL), lambda i: (layer * 3 + s, 0, 0)),
                pl.BlockSpec(memory_space=pl.ANY),
                pl.BlockSpec(memory_space=pl.ANY)]
    args = [x, mods, norm_g, w13, w2]
    n_pre = 0
    if pre is not None:
        ys, ws, w_layer = pre
        n_pre = len(ys)
        for y in ys:
            in_specs.append(pl.BlockSpec((tm, y.shape[1]), lambda i: (i, 0)))
        for w, rows, blk in ws:
            in_specs.append(_resident((1, rows, D_MODEL), lambda i, blk=blk: (w_layer, blk, 0)))
        args += list(ys) + [w for w, _, _ in ws]
    if final_g is not None:
        in_specs.append(pl.BlockSpec((1, D_MODEL), lambda i: (0, 0)))
        args.append(final_g.reshape(1, D_MODEL))
    return pl.pallas_call(
        functools.partial(_ffn_kernel, s=s, n_pre=n_pre, final=final_g is not None, layer=layer, which=which),
        grid=(m // tm,),
        in_specs=in_specs,
        out_specs=pl.BlockSpec((tm, D_MODEL), lambda i: (i, 0)),
        out_shape=jax.ShapeDtypeStruct((m, D_MODEL), F32),
        scratch_shapes=[pltpu.VMEM((D_MODEL, 2 * FFN_DIM), BF16),
                        pltpu.VMEM((FFN_DIM, D_MODEL), BF16),
                        pltpu.VMEM((2, 2, D_MODEL, FFN_TF), F32),
                        pltpu.VMEM((2, FFN_TF, D_MODEL), F32),
                        pltpu.SemaphoreType.DMA((2, 3))],
        compiler_params=_cparams(1),
        name="ffn",
    )(*args)


_EV_QA = 0
_EV_KA = NA_WIDTH
_EV_VA = 2 * NA_WIDTH
_EV_CQ = 3 * NA_WIDTH
_EV_CKV = _EV_CQ + MLA_Q_RANK
_EV_COLS = _EV_CKV + MLA_KV_RANK + MLA_ROPE_DIM
MLA_QK_W = MLA_HEADS * LANES
LOG2E = 1.4426950408889634
NA_QSCALE = NA_HEAD_DIM ** -0.5 * LOG2E
MLA_QSCALE = (MLA_NOPE_DIM + MLA_ROPE_DIM) ** -0.5 * LOG2E


def _rope_swap_perm(width, half):
    j = np.arange(width)
    return np.where((j % (2 * half)) < half, j + half, j - half)


def _pack_even_weights(mix_w_in, mla_w_uq, mla_w_ukv):
    n_even = mix_w_in.shape[0]
    o = 3 * NA_WIDTH + MLA_Q_RANK + MLA_KV_RANK
    w_kr = mix_w_in[:, :, o:o + MLA_ROPE_DIM]
    w_krs = w_kr[:, :, _rope_swap_perm(MLA_ROPE_DIM, MLA_ROPE_DIM // 4)]
    pad = ((0, 0), (0, 0), (MLA_NOPE_DIM, LANES - MLA_NOPE_DIM - MLA_ROPE_DIM))
    w_kr2 = jnp.concatenate([jnp.pad(w_kr, pad), jnp.pad(w_krs, pad)], axis=-1)

    uq = mla_w_uq.reshape(n_even, MLA_Q_RANK, MLA_HEADS, MLA_NOPE_DIM + MLA_ROPE_DIM)
    uq_rope = uq[..., MLA_NOPE_DIM:]
    uq_rope_s = uq_rope[..., _rope_swap_perm(MLA_ROPE_DIM, MLA_ROPE_DIM // 4)]
    zpad = LANES - MLA_NOPE_DIM - MLA_ROPE_DIM
    w_uq = jnp.pad(uq, ((0, 0), (0, 0), (0, 0), (0, zpad))).reshape(n_even, MLA_Q_RANK, MLA_QK_W)
    w_uqs = jnp.pad(uq_rope_s, ((0, 0), (0, 0), (0, 0), (MLA_NOPE_DIM, zpad)))
    w_uqs = w_uqs.reshape(n_even, MLA_Q_RANK, MLA_QK_W)

    ukv = mla_w_ukv.reshape(n_even, MLA_KV_RANK, MLA_HEADS, MLA_NOPE_DIM + MLA_V_DIM)
    w_uk = jnp.pad(ukv[..., :MLA_NOPE_DIM], ((0, 0), (0, 0), (0, 0), (0, LANES - MLA_NOPE_DIM)))
    w_uk = w_uk.reshape(n_even, MLA_KV_RANK, MLA_QK_W)
    w_uv = ukv[..., MLA_NOPE_DIM:].reshape(n_even, MLA_KV_RANK, MLA_OUT)
    return (mix_w_in.astype(BF16), w_kr2.astype(BF16), w_uq.astype(BF16), w_uqs.astype(BF16),
            w_uk.astype(BF16), w_uv.astype(BF16))


def _rope_angles(seq, width, group):
    half = group // 2
    t = np.arange(seq)
    j = np.arange(width)
    pos = np.where((j // group)[None, :] % 2 == 0, (t // GRID_W)[:, None], (t % GRID_W)[:, None])
    inv = np.float32(ROPE_BASE) ** (-np.arange(half, dtype=np.float32) / np.float32(half))
    ang = pos.astype(np.float32) * inv[j % half][None, :]
    sign = np.where((j % group) < half, -1.0, 1.0).astype(np.float32)
    return ang, sign


def _mla_rope_tables(seq):
    ang, sign = _rope_angles(seq, MLA_ROPE_DIM, MLA_ROPE_DIM // 2)
    tail = LANES - MLA_NOPE_DIM - MLA_ROPE_DIM
    cos = np.concatenate([np.ones((seq, MLA_NOPE_DIM), np.float32), np.cos(ang),
                          np.ones((seq, tail), np.float32)], axis=1)
    sin = np.concatenate([np.zeros((seq, MLA_NOPE_DIM), np.float32), np.sin(ang) * sign[None, :],
                          np.zeros((seq, tail), np.float32)], axis=1)
    return jnp.asarray(cos), jnp.asarray(sin)


def _even_proj_kernel(*refs, latent, n_alias):
    (x_ref, mod_ref, g_ref, win_ref, wkr_ref, qn_ref, kvn_ref, wuq_ref, wuqs_ref, wuk_ref,
     wuv_ref) = refs[:11]
    if latent:
        cos_ref, sin_ref = refs[11:13]
        qa_ref, ka_ref, va_ref, qm_ref, km_ref, vm_ref = refs[13:]
    else:
        qa_ref, ka_ref, va_ref, qm_ref, km_ref, vm_ref, ckv_ref, kr_ref = refs[11 + n_alias:]

    def w_in(lo, width):
        return win_ref[0, :, lo:lo + width]

    h = _modulated(x_ref[...], mod_ref, g_ref, 1).astype(BF16)
    qa_ref[...] = (_dot(h, w_in(_EV_QA, NA_WIDTH)) * NA_QSCALE).astype(qa_ref.dtype)
    ka = _dot(h, w_in(_EV_KA, NA_WIDTH))
    va = _dot(h, w_in(_EV_VA, NA_WIDTH))
    if latent:
        ka_ref[...] = ka.astype(BF16)
        va_ref[...] = va.astype(BF16)
    else:
        ka_ref[:, 0] = ka.reshape(ka_ref.shape[0], ka_ref.shape[2], NA_WIDTH)
        va_ref[:, 0] = va.reshape(va_ref.shape[0], va_ref.shape[2], NA_WIDTH)

    cqn = _rms(_dot(h, w_in(_EV_CQ, MLA_Q_RANK)), qn_ref[0]).astype(BF16)
    ckvn = _rms(_dot(h, w_in(_EV_CKV, MLA_KV_RANK)), kvn_ref[0])
    kr = _dot(h, wkr_ref[0, :, :LANES])
    if latent:
        cos = cos_ref[...]
        sin = sin_ref[...]
        kr = kr * cos + _dot(h, wkr_ref[0, :, LANES:]) * sin
    else:
        ckv_ref[:, 0] = ckvn.reshape(ckv_ref.shape[0], ckv_ref.shape[2], MLA_KV_RANK)
        kr_ref[:, 0] = kr[:, MLA_NOPE_DIM:MLA_NOPE_DIM + MLA_ROPE_DIM].reshape(
            kr_ref.shape[0], kr_ref.shape[2], MLA_ROPE_DIM)
    ckvb = ckvn.astype(BF16)

    qm = _dot(cqn, wuq_ref[0])
    km = _dot(ckvb, wuk_ref[0])
    if latent:
        qms = _dot(cqn, wuqs_ref[0])
    for hd in range(MLA_HEADS):
        sl = slice(hd * LANES, (hd + 1) * LANES)
        q_h = qm[:, sl]
        if latent:
            q_h = q_h * cos + qms[:, sl] * sin
        qm_ref[:, sl] = (q_h * MLA_QSCALE).astype(BF16)
        km_ref[:, sl] = (km[:, sl] + kr).astype(BF16)
    vm_ref[...] = _dot(ckvb, wuv_ref[0]).astype(BF16)


def _even_proj(x, mods, norm_g, packed, q_norm, kv_norm, layer, latent, seq, n_even, rope=None, caches=None):
    w_in, w_kr2, w_uq, w_uqs, w_uk, w_uv = packed
    i_even = layer // 2
    m = x.shape[0]
    tm = 512
    bt = tm // seq if not latent else 0
    tok = lambda i: (i, 0)
    in_specs = [pl.BlockSpec((tm, D_MODEL), tok),
                pl.BlockSpec((1, 1, N_MOD * D_MODEL), _mod_index(layer, latent, tm, seq)),
                pl.BlockSpec((1, 1, D_MODEL), lambda i: (layer * 3 + 1, 0, 0)),
                _resident((1, D_MODEL, _EV_COLS), lambda i: (i_even, 0, 0)),
                _resident((1, D_MODEL, 2 * LANES), lambda i: (i_even, 0, 0)),
                pl.BlockSpec((1, 1, MLA_Q_RANK), lambda i: (i_even, 0, 0)),
                pl.BlockSpec((1, 1, MLA_KV_RANK), lambda i: (i_even, 0, 0)),
                _resident((1, MLA_Q_RANK, MLA_QK_W), lambda i: (i_even, 0, 0)),
                _resident((1, MLA_Q_RANK, MLA_QK_W), lambda i: (i_even, 0, 0)),
                _resident((1, MLA_KV_RANK, MLA_QK_W), lambda i: (i_even, 0, 0)),
                _resident((1, MLA_KV_RANK, MLA_OUT), lambda i: (i_even, 0, 0))]
    args = [x, mods, norm_g, w_in, w_kr2, q_norm, kv_norm, w_uq, w_uqs, w_uk, w_uv]
    bf = lambda w: jax.ShapeDtypeStruct((m, w), BF16)
    aliases = {}
    if latent:
        nt = seq // tm
        in_specs += [pl.BlockSpec((tm, LANES), lambda i: (i % nt, 0))] * 2
        args += list(rope)
        out_shape = [bf(NA_WIDTH), bf(NA_WIDTH), bf(NA_WIDTH), bf(MLA_QK_W), bf(MLA_QK_W), bf(MLA_OUT)]
        out_specs = [pl.BlockSpec((tm, s.shape[1]), tok) for s in out_shape]
    else:
        nb = m // seq
        cache = lambda w: jax.ShapeDtypeStruct((nb, n_even, seq, w), F32)
        cspec = lambda w: pl.BlockSpec((bt, 1, seq, w), lambda i: (i, i_even, 0, 0))
        out_shape = [bf(NA_WIDTH), cache(NA_WIDTH), cache(NA_WIDTH), bf(MLA_QK_W), bf(MLA_QK_W),
                     bf(MLA_OUT), cache(MLA_KV_RANK), cache(MLA_ROPE_DIM)]
        out_specs = [pl.BlockSpec((tm, NA_WIDTH), tok), cspec(NA_WIDTH), cspec(NA_WIDTH),
                     pl.BlockSpec((tm, MLA_QK_W), tok), pl.BlockSpec((tm, MLA_QK_W), tok),
                     pl.BlockSpec((tm, MLA_OUT), tok), cspec(MLA_KV_RANK), cspec(MLA_ROPE_DIM)]
        if caches is not None:
            for arr, out_idx in zip(caches, (1, 2, 6, 7)):
                aliases[len(args)] = out_idx
                in_specs.append(pl.BlockSpec(memory_space=pl.ANY))
                args.append(arr)
    return pl.pallas_call(
        functools.partial(_even_proj_kernel, latent=latent, n_alias=len(aliases)),
        grid=(m // tm,),
        in_specs=in_specs,
        out_specs=out_specs,
        out_shape=out_shape,
        input_output_aliases=aliases,
        compiler_params=_cparams(1),
        name="even_proj_latent" if latent else "even_proj_context",
    )(*args)


def _mla_cache_kernel(ckv_ref, kr_ref, wuk_ref, wuv_ref, km_ref, vm_ref):
    ckvb = ckv_ref[0, 0].astype(BF16)
    km = _dot(ckvb, wuk_ref[0])
    kr = kr_ref[...]
    for hd in range(MLA_HEADS):
        sl = slice(hd * LANES, (hd + 1) * LANES)
        km_ref[:, sl] = (km[:, sl] + kr).astype(BF16)
    vm_ref[...] = _dot(ckvb, wuv_ref[0]).astype(BF16)


def _mla_cache_keys(cache_ckv, kr_padded, w_uk, w_uv, i_even):
    nb, _, past, _ = cache_ckv.shape
    return pl.pallas_call(
        _mla_cache_kernel,
        grid=(nb,),
        in_specs=[pl.BlockSpec((1, 1, past, MLA_KV_RANK), lambda b: (b, i_even, 0, 0)),
                  pl.BlockSpec((past, LANES), lambda b: (b, 0)),
                  _resident((1, MLA_KV_RANK, MLA_QK_W), lambda b: (i_even, 0, 0)),
                  _resident((1, MLA_KV_RANK, MLA_OUT), lambda b: (i_even, 0, 0))],
        out_specs=[pl.BlockSpec((past, MLA_QK_W), lambda b: (b, 0)),
                   pl.BlockSpec((past, MLA_OUT), lambda b: (b, 0))],
        out_shape=[jax.ShapeDtypeStruct((nb * past, MLA_QK_W), BF16),
                   jax.ShapeDtypeStruct((nb * past, MLA_OUT), BF16)],
        compiler_params=_cparams(1),
        name="mla_cache_keys",
    )(cache_ckv, kr_padded, w_uk, w_uv)


def _half_masks():
    lane = lax.broadcasted_iota(jnp.int32, (1, LANES), 1)
    lo = lane < (LANES // 2)
    return lo, jnp.logical_not(lo)


def _scores_pass(q, key_blocks, bias_blocks, s_ref, kb):
    mx = None
    for j, (k_blk, bias) in enumerate(zip(key_blocks, bias_blocks)):
        sc = _dot_nt(q, k_blk())
        if bias is not None:
            sc = sc + bias()
        s_ref[:, j * kb:(j + 1) * kb] = sc
        for g in range(kb // LANES):
            part = sc[:, g * LANES:(g + 1) * LANES]
            mx = part if mx is None else jnp.maximum(mx, part)
    return jnp.max(mx, axis=-1, keepdims=True)


def _values_pass(s_ref, m, value_blocks, kb):
    ls = None
    acc = None
    for j, v_blk in enumerate(value_blocks):
        p = jnp.exp2(s_ref[:, j * kb:(j + 1) * kb] - m)
        for g in range(kb // LANES):
            part = p[:, g * LANES:(g + 1) * LANES]
            ls = part if ls is None else ls + part
        o = _dot(p.astype(BF16), v_blk())
        acc = o if acc is None else acc + o
    return acc / jnp.sum(ls, axis=-1, keepdims=True)


def _attention_chains(chains, s_scr, kb=None):
    kb = ATT_KB if kb is None else kb
    depth = s_scr.shape[0]
    outs = []
    maxes = {}
    for i in range(len(chains) + depth - 1):
        if i < len(chains):
            q, key_blocks, _, bias_blocks = chains[i]
            maxes[i] = _scores_pass(q(), key_blocks, bias_blocks, s_scr.at[i % depth], kb)
        j = i - (depth - 1)
        if j >= 0:
            outs.append(_values_pass(s_scr.at[j % depth], maxes.pop(j), chains[j][2], kb))
    return outs


def _rows(ref, start, size, lanes):
    return lambda: ref[start:start + size, lanes]


def _ctx_attn_kernel(qa_ref, ka_ref, va_ref, qm_ref, km_ref, vm_ref, o_ref, s_scr, kv_scr):
    lo, hi = _half_masks()
    zero = jnp.zeros((), BF16)
    seq = qa_ref.shape[0]
    kb_sz = min(ATT_KB, seq)
    qs_sz = min(ATT_QS, seq)
    kv_scr[0] = ka_ref[0, 0].astype(BF16)
    kv_scr[1] = va_ref[0, 0].astype(BF16)
    chains = []
    for g in range((NA_WIDTH + MLA_OUT) // LANES):
        mla = g >= NA_WIDTH // LANES
        gl = g - NA_WIDTH // LANES if mla else g
        sl = slice(gl * LANES, (gl + 1) * LANES)
        for qs in range(seq // qs_sz):
            rows = slice(qs * qs_sz, (qs + 1) * qs_sz)
            for t, msk in enumerate((lo, hi)):
                if mla:
                    hsl = slice((2 * gl + t) * LANES, (2 * gl + t + 1) * LANES)
                    q = lambda rows=rows, hsl=hsl: qm_ref[rows, hsl]
                    kb = [_rows(km_ref, j * kb_sz, kb_sz, hsl) for j in range(seq // kb_sz)]
                    vb = [_rows(vm_ref, j * kb_sz, kb_sz, sl) for j in range(seq // kb_sz)]
                else:
                    q = lambda rows=rows, sl=sl, msk=msk: jnp.where(msk, qa_ref[rows, sl], zero)
                    kb = [_rows(kv_scr.at[0], j * kb_sz, kb_sz, sl) for j in range(seq // kb_sz)]
                    vb = [_rows(kv_scr.at[1], j * kb_sz, kb_sz, sl) for j in range(seq // kb_sz)]
                chains.append((q, kb, vb, [None] * len(kb)))
    outs = _attention_chains(chains, s_scr, kb_sz)
    n = 0
    for g in range((NA_WIDTH + MLA_OUT) // LANES):
        for qs in range(seq // qs_sz):
            o_ref[qs * qs_sz:(qs + 1) * qs_sz, g * LANES:(g + 1) * LANES] = (
                jnp.where(lo, outs[n], outs[n + 1]).astype(BF16))
            n += 2


def _ctx_attention(qa, ka, va, qm, km, vm, seq, i_even):
    m = qa.shape[0]
    nb = m // seq
    tok = lambda b: (b, 0)
    cache = lambda b: (b, i_even, 0, 0)
    return pl.pallas_call(
        _ctx_attn_kernel,
        grid=(nb,),
        in_specs=[pl.BlockSpec((seq, NA_WIDTH), tok),
                  pl.BlockSpec((1, 1, seq, NA_WIDTH), cache),
                  pl.BlockSpec((1, 1, seq, NA_WIDTH), cache),
                  pl.BlockSpec((seq, MLA_QK_W), tok),
                  pl.BlockSpec((seq, MLA_QK_W), tok),
                  pl.BlockSpec((seq, MLA_OUT), tok)],
        out_specs=pl.BlockSpec((seq, NA_WIDTH + MLA_OUT), tok),
        out_shape=jax.ShapeDtypeStruct((m, NA_WIDTH + MLA_OUT), BF16),
        scratch_shapes=[pltpu.VMEM((2, min(ATT_QS, seq), seq), F32), pltpu.VMEM((2, seq, NA_WIDTH), BF16)],
        compiler_params=_cparams(1),
        name="context_attention",
    )(qa, ka, va, qm, km, vm)


def _lat_mla_kernel(q_ref, kc_ref, kl_ref, vc_ref, vl_ref, o_ref, s_scr):
    lo, _ = _half_masks()
    past, seq = kc_ref.shape[0], kl_ref.shape[0]
    all_lanes = slice(0, LANES)
    kbs = []
    for t in range(2):
        sl = slice(t * LANES, (t + 1) * LANES)
        kbs.append([_rows(kc_ref, j * ATT_KB, ATT_KB, sl) for j in range(past // ATT_KB)]
                   + [_rows(kl_ref, j * ATT_KB, ATT_KB, sl) for j in range(seq // ATT_KB)])
    vb = ([_rows(vc_ref, j * ATT_KB, ATT_KB, all_lanes) for j in range(past // ATT_KB)]
          + [_rows(vl_ref, j * ATT_KB, ATT_KB, all_lanes) for j in range(seq // ATT_KB)])
    n_qs = ATT_TQ // ATT_QS

    def tile(i, carry):
        base = pl.multiple_of(i * ATT_TQ, ATT_TQ)
        chains = []
        for qs in range(n_qs):
            rows = pl.ds(base + qs * ATT_QS, ATT_QS)
            for t in range(2):
                sl = slice(t * LANES, (t + 1) * LANES)
                chains.append((lambda rows=rows, sl=sl: q_ref[rows, sl], kbs[t], vb, [None] * len(vb)))
        outs = _attention_chains(chains, s_scr)
        for qs in range(n_qs):
            o_ref[pl.ds(base + qs * ATT_QS, ATT_QS), :] = (
                jnp.where(lo, outs[2 * qs], outs[2 * qs + 1]).astype(BF16))
        return carry

    lax.fori_loop(0, seq // ATT_TQ, tile, 0)


def _latent_mla_attention(qm, km_ctx, km_lat, vm_ctx, vm_lat, seq, past):
    m = qm.shape[0]
    nb = m // seq
    pair = 2 * LANES
    return pl.pallas_call(
        _lat_mla_kernel,
        grid=(nb, MLA_HEADS // 2),
        in_specs=[pl.BlockSpec((seq, pair), lambda b, g: (b, g)),
                  pl.BlockSpec((past, pair), lambda b, g: (b, g)),
                  pl.BlockSpec((seq, pair), lambda b, g: (b, g)),
                  pl.BlockSpec((past, LANES), lambda b, g: (b, g)),
                  pl.BlockSpec((seq, LANES), lambda b, g: (b, g))],
        out_specs=pl.BlockSpec((seq, LANES), lambda b, g: (b, g)),
        out_shape=jax.ShapeDtypeStruct((m, MLA_OUT), BF16),
        scratch_shapes=[pltpu.VMEM((2, ATT_QS, past + seq), F32)],
        compiler_params=_cparams(2),
        name="latent_mla_attention",
    )(qm, km_ctx, km_lat, vm_ctx, vm_lat)


def _na_window_start(first_row, rows, clip=np.clip):
    r0 = clip(first_row - NA_ROWS // 2, 0, rows - NA_ROWS)
    return clip(r0 - (NA_WROWS - NA_ROWS) // 2, 0, rows - NA_WROWS)


def _na_bias_kernel(rpb_ref, o_ref, *, rows):
    hd = pl.program_id(0)
    kr = min(NA_ROWS, rows)
    n_dc = 2 * NA_COLS - 1
    band = ATT_QS // GRID_W
    c = lax.broadcasted_iota(jnp.int32, (GRID_W, LANES), 0)
    lane = lax.broadcasted_iota(jnp.int32, (GRID_W, LANES), 1)
    n = lane % GRID_W
    c0 = jnp.clip(c - NA_COLS // 2, 0, GRID_W - NA_COLS)
    in_win = (n >= c0) & (n < c0 + NA_COLS)
    dc = n - c + NA_COLS - 1
    left = lane < GRID_W
    neg = jnp.full((GRID_W, LANES), NEG_BIG, F32)
    col_tables = []
    for dr in range(2 * NA_ROWS - 1):
        t = neg
        for j in range(n_dc):
            t = jnp.where(dc == j, rpb_ref[hd * (2 * NA_ROWS - 1) * n_dc + dr * n_dc + j] * LOG2E, t)
        col_tables.append(jnp.where(in_win, t, neg))
    for rb in range(rows // NA_QROWS):
        for rl in range(NA_QROWS):
            r = rb * NA_QROWS + rl
            r0 = min(max(r - kr // 2, 0), rows - kr)
            start = int(_na_window_start(r - rl % band, rows))
            assert start <= r0 and r0 + kr <= start + NA_WROWS
            for pr in range(NA_WROWS // 2):
                halves = []
                for key_row in (start + 2 * pr, start + 2 * pr + 1):
                    ok = r0 <= key_row < r0 + kr
                    halves.append(col_tables[key_row - r + NA_ROWS - 1] if ok else neg)
                o_ref[rb, 0, rl * GRID_W:(rl + 1) * GRID_W, pr * LANES:(pr + 1) * LANES] = (
                    jnp.where(left, halves[0], halves[1]))


def _na_bias(rpb, rows):
    n_rb = rows // NA_QROWS
    flat = rpb.reshape(-1)
    return pl.pallas_call(
        functools.partial(_na_bias_kernel, rows=rows),
        grid=(NA_HEADS,),
        in_specs=[pl.BlockSpec(memory_space=pltpu.SMEM)],
        out_specs=pl.BlockSpec((n_rb, 1, NA_QROWS * GRID_W, NA_WROWS * GRID_W), lambda h: (0, h, 0, 0)),
        out_shape=jax.ShapeDtypeStruct((n_rb, NA_HEADS, NA_QROWS * GRID_W, NA_WROWS * GRID_W), F32),
        compiler_params=_cparams(1),
        name="na_bias",
    )(flat)


def _lat_na_kernel(q_ref, k_ref, v_ref, kc_ref, vc_ref, bias_ref, o_ref, s_scr, kv_scr, *, rows):
    lo, hi = _half_masks()
    zero = jnp.zeros((), BF16)
    rb = pl.program_id(0)
    nk = NA_WROWS * GRID_W
    past = kc_ref.shape[2]
    all_lanes = slice(0, LANES)
    n_qs = q_ref.shape[1] // ATT_QS
    starts = [pl.multiple_of(_na_window_start(rb * NA_QROWS + qs * (ATT_QS // GRID_W), rows, jnp.clip) * GRID_W,
                             GRID_W) for qs in range(n_qs)]

    def batch(b, carry):
        kv_scr[0] = kc_ref[b, 0].astype(BF16)
        kv_scr[1] = vc_ref[b, 0].astype(BF16)

        def win(ref, start, j):
            return lambda: ref[b, pl.ds(start + j * ATT_KB, ATT_KB), :]

        ctx_k = [_rows(kv_scr.at[0], j * ATT_KB, ATT_KB, all_lanes) for j in range(past // ATT_KB)]
        ctx_v = [_rows(kv_scr.at[1], j * ATT_KB, ATT_KB, all_lanes) for j in range(past // ATT_KB)]
        chains = []
        for qs in range(n_qs):
            qrows = slice(qs * ATT_QS, (qs + 1) * ATT_QS)
            kb = [win(k_ref, starts[qs], j) for j in range(nk // ATT_KB)] + ctx_k
            vb = [win(v_ref, starts[qs], j) for j in range(nk // ATT_KB)] + ctx_v
            for t, msk in enumerate((lo, hi)):
                bias = ([(lambda t=t, j=j, qrows=qrows: bias_ref[0, t, qrows, j * ATT_KB:(j + 1) * ATT_KB])
                         for j in range(nk // ATT_KB)] + [None] * (past // ATT_KB))
                chains.append((lambda qrows=qrows, msk=msk: jnp.where(msk, q_ref[b, qrows, :], zero),
                               kb, vb, bias))
        outs = _attention_chains(chains, s_scr)
        for qs in range(n_qs):
            o_ref[b, qs * ATT_QS:(qs + 1) * ATT_QS, :] = (
                jnp.where(lo, outs[2 * qs], outs[2 * qs + 1]).astype(BF16))
        return carry

    lax.fori_loop(0, q_ref.shape[0], batch, 0)


def _latent_na_attention(qa, ka, va, cache_k, cache_v, bias, i_even, seq):
    m = qa.shape[0]
    nb = m // seq
    rows = seq // GRID_W
    n_rb = rows // NA_QROWS
    tq = NA_QROWS * GRID_W
    past = cache_k.shape[2]
    per_batch = lambda a: a.reshape(nb, seq, NA_WIDTH)
    out = pl.pallas_call(
        functools.partial(_lat_na_kernel, rows=rows),
        grid=(n_rb, NA_HEADS // 2),
        in_specs=[pl.BlockSpec((nb, tq, LANES), lambda r, g: (0, r, g)),
                  pl.BlockSpec((nb, seq, LANES), lambda r, g: (0, 0, g)),
                  pl.BlockSpec((nb, seq, LANES), lambda r, g: (0, 0, g)),
                  pl.BlockSpec((nb, 1, past, LANES), lambda r, g: (0, i_even, 0, g)),
                  pl.BlockSpec((nb, 1, past, LANES), lambda r, g: (0, i_even, 0, g)),
                  pl.BlockSpec((1, 2, tq, NA_WROWS * GRID_W), lambda r, g: (r, g, 0, 0))],
        out_specs=pl.BlockSpec((nb, tq, LANES), lambda r, g: (0, r, g)),
        out_shape=jax.ShapeDtypeStruct((nb, seq, NA_WIDTH), BF16),
        scratch_shapes=[pltpu.VMEM((2, ATT_QS, NA_WROWS * GRID_W + past), F32),
                        pltpu.VMEM((2, past, LANES), BF16)],
        compiler_params=_cparams(2),
        name="latent_na_attention",
    )(per_batch(qa), per_batch(ka), per_batch(va), cache_k, cache_v, bias)
    return out.reshape(m, NA_WIDTH)


_RET_NK = RET_HEADS * RET_DK
_RET_NV = RET_HEADS * RET_DV


def _ret_proj_kernel(x_ref, mod_ref, g_ref, w_ref, cos_ref, sin_ref, q_ref, k_ref, v_ref, gate_ref):
    h = _modulated(x_ref[...], mod_ref, g_ref, 1).astype(BF16)

    def rope(y):
        parts = [pltpu.roll(y[:, g * LANES:(g + 1) * LANES], LANES // 2, 1) for g in range(RET_DK // LANES)]
        return y * cos_ref[...] + jnp.concatenate(parts, axis=1) * sin_ref[...]

    for hd in range(RET_HEADS):
        sl = slice(hd * RET_DK, (hd + 1) * RET_DK)
        q_ref[:, sl] = rope(_dot(h, w_ref[0, :, sl])).astype(BF16)
        k = _dot(h, w_ref[0, :, _RET_NK + hd * RET_DK:_RET_NK + (hd + 1) * RET_DK]) * (RET_DK ** -0.5)
        k_ref[:, sl] = rope(k)
    step = 512
    for j in range(_RET_NV // step):
        o = 2 * _RET_NK + j * step
        v_ref[:, j * step:(j + 1) * step] = _dot(h, w_ref[0, :, o:o + step]).astype(BF16)
    for j in range(2 * _RET_NV // step):
        o = 2 * _RET_NK + _RET_NV + j * step
        gate_ref[:, j * step:(j + 1) * step] = _dot(h, w_ref[0, :, o:o + step])


def _ret_proj_latent(x, mods, norm_g, w_in, layer, seq, rope):
    m = x.shape[0]
    tm = 256
    i_odd = layer // 2
    tok = lambda i: (i, 0)
    n_in = 2 * _RET_NK + 3 * _RET_NV
    nt = seq // tm
    return pl.pallas_call(
        _ret_proj_kernel,
        grid=(m // tm,),
        in_specs=[pl.BlockSpec((tm, D_MODEL), tok),
                  pl.BlockSpec((1, 1, N_MOD * D_MODEL), _mod_index(layer, True, tm, seq)),
                  pl.BlockSpec((1, 1, D_MODEL), lambda i: (layer * 3 + 1, 0, 0)),
                  _resident((1, D_MODEL, n_in), lambda i: (i_odd, 0, 0)),
                  pl.BlockSpec((tm, RET_DK), lambda i: (i % nt, 0)),
                  pl.BlockSpec((tm, RET_DK), lambda i: (i % nt, 0))],
        out_specs=[pl.BlockSpec((tm, _RET_NK), tok),
                   pl.BlockSpec((tm, _RET_NK), tok),
                   pl.BlockSpec((tm, _RET_NV), tok),
                   pl.BlockSpec((tm, 2 * _RET_NV), tok)],
        out_shape=[jax.ShapeDtypeStruct((m, _RET_NK), BF16),
                   jax.ShapeDtypeStruct((m, _RET_NK), F32),
                   jax.ShapeDtypeStruct((m, _RET_NV), BF16),
                   jax.ShapeDtypeStruct((m, 2 * _RET_NV), F32)],
        compiler_params=_cparams(1),
        name="ret_proj_latent",
    )(x, mods, norm_g, w_in, *rope)


def _ret_rope_tables(seq):
    ang, sign = _rope_angles(seq, RET_DK, RET_DK // 2)
    return jnp.asarray(np.cos(ang)), jnp.asarray(np.sin(ang) * sign[None, :])


def _head_norm(o):
    mu = jnp.mean(o, axis=-1, keepdims=True)
    d = o - mu
    return d * lax.rsqrt(jnp.mean(d * d, axis=-1, keepdims=True) + EPS)


def _ret_decays(lg, dirn, c):
    ii = lax.broadcasted_iota(jnp.int32, (c, c), 0)
    jj = lax.broadcasted_iota(jnp.int32, (c, c), 1)
    row = lax.broadcasted_iota(jnp.int32, (c, 1), 0).astype(F32)
    diff = (ii - jj if dirn == 0 else jj - ii).astype(F32)
    intra = jnp.where(diff >= 0, jnp.exp(lg * jnp.maximum(diff, 0.0)), 0.0)
    if dirn == 0:
        q_dec = jnp.exp(lg * (row + 1.0))
        k_dec = jnp.exp(lg * (c - 1.0 - row))
    else:
        q_dec = jnp.exp(lg * (c - row))
        k_dec = jnp.exp(lg * row)
    return intra, q_dec, k_dec, jnp.exp(lg * float(c))


def _ret_ctx_kernel(*refs, n_alias):
    ld_ref, x_ref, mod_ref, g_ref, w_ref = refs[:5]
    y_ref, st_ref = refs[5 + n_alias:]
    c = x_ref.shape[0]
    h = _modulated(x_ref[...], mod_ref, g_ref, 1).astype(BF16)
    for hd in range(RET_HEADS):
        ksl = slice(hd * RET_DK, (hd + 1) * RET_DK)
        qb = _dot(h, w_ref[0, :, ksl]).astype(BF16)
        k = _dot(h, w_ref[0, :, _RET_NK + hd * RET_DK:_RET_NK + (hd + 1) * RET_DK]) * (RET_DK ** -0.5)
        vo = 2 * _RET_NK + hd * RET_DV
        v = _dot(h, w_ref[0, :, vo:vo + RET_DV]).astype(BF16)
        qk = _dot_nt(qb, k.astype(BF16))
        y = None
        for dirn in range(2):
            intra, _, k_dec, _ = _ret_decays(-jnp.abs(ld_ref[dirn, hd]), dirn, c)
            o = _dot((qk * intra).astype(BF16), v)
            st_ref[0, 0, dirn, hd] = _dot((k * k_dec).T.astype(BF16), v)
            go = 2 * _RET_NK + (1 + dirn) * _RET_NV + hd * RET_DV
            yd = _silu(_dot(h, w_ref[0, :, go:go + RET_DV])) * _head_norm(o)
            y = yd if y is None else y + yd
        y_ref[:, hd * RET_DV:(hd + 1) * RET_DV] = y.astype(BF16)


def _retention_context(x, mods, norm_g, w_in, log_decay, layer, seq, n_odd, prev_states=None):
    assert seq == RET_CHUNK
    m = x.shape[0]
    nb = m // seq
    i_odd = layer // 2
    n_in = 2 * _RET_NK + 3 * _RET_NV
    in_specs = [pl.BlockSpec(memory_space=pltpu.SMEM),
                pl.BlockSpec((seq, D_MODEL), lambda b: (b, 0)),
                pl.BlockSpec((1, 1, N_MOD * D_MODEL), _mod_index(layer, False, seq, seq)),
                pl.BlockSpec((1, 1, D_MODEL), lambda b: (layer * 3 + 1, 0, 0)),
                _resident((1, D_MODEL, n_in), lambda b: (i_odd, 0, 0))]
    args = [log_decay[i_odd], x, mods, norm_g, w_in]
    aliases = {}
    if prev_states is not None:
        aliases[len(args)] = 1
        in_specs.append(pl.BlockSpec(memory_space=pl.ANY))
        args.append(prev_states)
    return pl.pallas_call(
        functools.partial(_ret_ctx_kernel, n_alias=len(aliases)),
        grid=(nb,),
        in_specs=in_specs,
        out_specs=[pl.BlockSpec((seq, _RET_NV), lambda b: (b, 0)),
                   pl.BlockSpec((1, 1, 2, RET_HEADS, RET_DK, RET_DV), lambda b: (b, i_odd, 0, 0, 0, 0))],
        out_shape=[jax.ShapeDtypeStruct((m, _RET_NV), BF16),
                   jax.ShapeDtypeStruct((nb, n_odd, 2, RET_HEADS, RET_DK, RET_DV), F32)],
        input_output_aliases=aliases,
        compiler_params=_cparams(1),
        name="retention_context",
    )(*args)


def _ret_lat_kernel(ld_ref, q_ref, k_ref, v_ref, gf_ref, gb_ref, s0_ref, y_ref, s_scr, o_scr):
    hd = pl.program_id(1)
    c = RET_CHUNK
    n_chunks = q_ref.shape[0] // c
    decays = [_ret_decays(-jnp.abs(ld_ref[dirn, hd]), dirn, c) for dirn in range(2)]
    gates = (gf_ref, gb_ref)
    for dirn in range(2):
        s_scr[dirn] = s0_ref[0, 0, dirn, 0]

    def scan(first_visit):
        def body(t, carry):
            for dirn in range(2):
                intra, q_dec, k_dec, c_dec = decays[dirn]
                ci = t if dirn == 0 else n_chunks - 1 - t
                rows = pl.ds(pl.multiple_of(ci * c, c), c)
                qb = q_ref[rows, :]
                k = k_ref[rows, :]
                v = v_ref[rows, :]
                a = _dot_nt(qb, k.astype(BF16)) * intra
                s_prev = s_scr[dirn]
                o = _dot(a.astype(BF16), v) + _dot(qb, s_prev.astype(BF16)) * q_dec
                s_scr[dirn] = s_prev * c_dec + _dot((k * k_dec).T.astype(BF16), v)
                y = _silu(gates[dirn][rows, :]) * _head_norm(o)
                if first_visit:
                    o_scr[rows, :] = y
                else:
                    y_ref[rows, :] = (o_scr[rows, :] + y).astype(BF16)
            return carry
        return body

    lax.fori_loop(0, n_chunks // 2, scan(True), 0, unroll=2)
    lax.fori_loop(n_chunks // 2, n_chunks, scan(False), 0, unroll=2)


def _retention_latent(q, k, v, gates, log_decay, i_odd, seq, state):
    m = q.shape[0]
    nb = m // seq
    assert (seq // RET_CHUNK) % 4 == 0
    return pl.pallas_call(
        _ret_lat_kernel,
        grid=(nb, RET_HEADS),
        in_specs=[pl.BlockSpec(memory_space=pltpu.SMEM),
                  pl.BlockSpec((seq, RET_DK), lambda b, h: (b, h)),
                  pl.BlockSpec((seq, RET_DK), lambda b, h: (b, h)),
                  pl.BlockSpec((seq, RET_DV), lambda b, h: (b, h)),
                  pl.BlockSpec((seq, RET_DV), lambda b, h: (b, h)),
                  pl.BlockSpec((seq, RET_DV), lambda b, h: (b, RET_HEADS + h)),
                  pl.BlockSpec((1, 1, 2, 1, RET_DK, RET_DV), lambda b, h: (b, i_odd, 0, h, 0, 0))],
        out_specs=pl.BlockSpec((seq, RET_DV), lambda b, h: (b, h)),
        out_shape=jax.ShapeDtypeStruct((m, _RET_NV), BF16),
        scratch_shapes=[pltpu.VMEM((2, RET_DK, RET_DV), F32), pltpu.VMEM((seq, RET_DV), F32)],
        compiler_params=_cparams(2),
        name="retention_latent",
    )(log_decay[i_odd], q, k, v, gates, gates, state)


def kernel(x_prompt, x_sample, c, cache_na_k, cache_na_v, cache_mla_ckv, cache_mla_krope, state_ret,
           c_ctx, norm_g, ada_w, ada_b, ffn_w13, ffn_w2, mix_w_in, mla_q_norm, mla_kv_norm, mla_w_uq,
           mla_w_ukv, na_rpb, mix_w_out, ret_w_in, ret_log_decay, ret_w_out, final_norm_g):
    batch, seq, _ = x_prompt.shape
    dec_batch, dec_seq, _ = x_sample.shape
    past = cache_na_k.shape[2]
    n_even = mix_w_in.shape[0]
    assert dec_batch + 1 <= MOD_ROWS and seq == RET_CHUNK and dec_seq % RET_CHUNK == 0
    assert (dec_seq // GRID_W) % NA_QROWS == 0 and dec_seq // GRID_W >= NA_WROWS

    w_out_even = mix_w_out.astype(BF16)
    w_in_ret = ret_w_in.astype(BF16)
    w_out_ret = ret_w_out.astype(BF16)
    packed = _pack_even_weights(mix_w_in, mla_w_uq, mla_w_ukv)
    q_norm = mla_q_norm.reshape(n_even, 1, MLA_Q_RANK)
    kv_norm = mla_kv_norm.reshape(n_even, 1, MLA_KV_RANK)
    norm_g3 = norm_g.reshape(DEPTH * 3, 1, D_MODEL)
    mla_rope = _mla_rope_tables(dec_seq)
    ret_rope = _ret_rope_tables(dec_seq)
    cache_k = cache_na_k.reshape(dec_batch, n_even, past, NA_WIDTH)
    cache_v = cache_na_v.reshape(dec_batch, n_even, past, NA_WIDTH)

    cvec = jnp.concatenate([c_ctx[None, :], c, jnp.zeros((MOD_ROWS - 1 - dec_batch, D_MODEL), F32)], axis=0)
    mods = _ada_all(cvec, ada_w, ada_b).reshape(DEPTH * MOD_ROWS, 1, N_MOD * D_MODEL)

    xp = x_prompt.reshape(batch * seq, D_MODEL)
    xs = x_sample.reshape(dec_batch * dec_seq, D_MODEL)
    n_odd = ret_w_in.shape[0]
    caches = None
    states = None
    for layer in range(DEPTH):
        i = layer // 2
        xp = _ffn(xp, mods, norm_g3, ffn_w13, ffn_w2, layer, 0, False, seq)
        xs = _ffn(xs, mods, norm_g3, ffn_w13, ffn_w2, layer, 0, True, dec_seq)
        if layer % 2 == 0:
            qa, ka, va, qm, km, vm, ckv, kr = _even_proj(
                xp, mods, norm_g3, packed, q_norm, kv_norm, layer, False, seq, n_even, caches=caches)
            caches = (ka, va, ckv, kr)
            op = _ctx_attention(qa, ka, va, qm, km, vm, seq, i)
            pre_p = ([op], [(w_out_even, NA_WIDTH + MLA_OUT, 0)], i)

            qa, ka, va, qm, km, vm = _even_proj(
                xs, mods, norm_g3, packed, q_norm, kv_norm, layer, True, dec_seq, n_even, rope=mla_rope)
            kr_pad = jnp.pad(cache_mla_krope[:, i].reshape(dec_batch * past, MLA_ROPE_DIM),
                             ((0, 0), (MLA_NOPE_DIM, LANES - MLA_NOPE_DIM - MLA_ROPE_DIM)))
            km_ctx, vm_ctx = _mla_cache_keys(cache_mla_ckv, kr_pad, packed[4], packed[5], i)
            bias = _na_bias(na_rpb[i], dec_seq // GRID_W)
            oa = _latent_na_attention(qa, ka, va, cache_k, cache_v, bias, i, dec_seq)
            ob = _latent_mla_attention(qm, km_ctx, km, vm_ctx, vm, dec_seq, past)
            pre_s = ([oa, ob], [(w_out_even, NA_WIDTH, 0), (w_out_even, MLA_OUT, NA_WIDTH // MLA_OUT)], i)
        else:
            yp, states = _retention_context(xp, mods, norm_g3, w_in_ret, ret_log_decay, layer, seq, n_odd,
                                            prev_states=states)
            pre_p = ([yp], [(w_out_ret, _RET_NV, 0)], i)

            q, k, v, gates = _ret_proj_latent(xs, mods, norm_g3, w_in_ret, layer, dec_seq, ret_rope)
            ys = _retention_latent(q, k, v, gates, ret_log_decay, i, dec_seq, state_ret)
            pre_s = ([ys], [(w_out_ret, _RET_NV, 0)], i)
        last = final_norm_g if layer == DEPTH - 1 else None
        xp = _ffn(xp, mods, norm_g3, ffn_w13, ffn_w2, layer, 1, False, seq, pre=pre_p, final_g=last)
        xs = _ffn(xs, mods, norm_g3, ffn_w13, ffn_w2, layer, 1, True, dec_seq, pre=pre_s, final_g=last)

    y_prompt = xp.reshape(batch, seq, D_MODEL)
    y_sample = xs.reshape(dec_batch, dec_seq, D_MODEL)
    ka, va, ckv, kr = caches
    new_na_k = ka.reshape(batch, n_even, seq, NA_HEADS, NA_HEAD_DIM)
    new_na_v = va.reshape(batch, n_even, seq, NA_HEADS, NA_HEAD_DIM)
    return (y_prompt, y_sample, new_na_k, new_na_v, ckv, kr, states)
```

```python
import functools

import jax
import jax.numpy as jnp
import numpy as np
from jax import lax
from jax.experimental import pallas as pl
from jax.experimental.pallas import tpu as pltpu

F32 = jnp.float32
BF16 = jnp.bfloat16

D_MODEL = 1024
DEPTH = 4
GRID_W = 64
NA_HEADS = 8
NA_HEAD_DIM = 64
NA_ROWS = 8
NA_COLS = 16
MLA_HEADS = 8
MLA_Q_RANK = 384
MLA_KV_RANK = 256
MLA_NOPE_DIM = 64
MLA_ROPE_DIM = 32
MLA_V_DIM = 64
RET_HEADS = 4
RET_DK = D_MODEL // RET_HEADS
RET_DV = 2 * D_MODEL // RET_HEADS
FFN_DIM = ((8 * D_MODEL // 3 + 127) // 128) * 128
N_MOD = 9
ROPE_BASE = 10000.0
EPS = 1e-6
NA_WIDTH = NA_HEADS * NA_HEAD_DIM
MLA_OUT = MLA_HEADS * MLA_V_DIM

LANES = 128
MOD_ROWS = 8
NEG_BIG = -1e30
VMEM_LIMIT = 58 * 1024 * 1024

FFN_TF = 256
FFN_AHEAD = 2
RET_CHUNK = 256
NA_QROWS = 8
NA_WROWS = 12
ATT_QS = 128
ATT_KB = 256
ATT_TQ = 2048


def _cparams(n_axes):
    return pltpu.CompilerParams(dimension_semantics=("arbitrary",) * n_axes,
                                vmem_limit_bytes=VMEM_LIMIT)


def _resident(block, index_map):
    return pl.BlockSpec(block, index_map, pipeline_mode=pl.Buffered(1))


def _dot(a, b):
    return jnp.dot(a, b, preferred_element_type=F32)


def _dot_nt(a, b):
    return lax.dot_general(a, b, (((1,), (1,)), ((), ())), preferred_element_type=F32)


def _silu(x):
    return x * (0.5 * jnp.tanh(0.5 * x) + 0.5)


def _rms(x, g):
    return x * lax.rsqrt(jnp.mean(x * x, axis=-1, keepdims=True) + EPS) * g


def _modulated(x, mod_ref, g_ref, s):
    shift = mod_ref[0, :, 3 * s * D_MODEL:(3 * s + 1) * D_MODEL]
    scale = mod_ref[0, :, (3 * s + 1) * D_MODEL:(3 * s + 2) * D_MODEL]
    return _rms(x, g_ref[0]) * (1.0 + scale) + shift


def _gate(mod_ref, s):
    return mod_ref[0, :, (3 * s + 2) * D_MODEL:(3 * s + 3) * D_MODEL]


def _mod_index(layer, latent, tile_rows, seq):
    if not latent:
        return lambda i: (layer * MOD_ROWS, 0, 0)
    return lambda i: (layer * MOD_ROWS + 1 + (i * tile_rows) // seq, 0, 0)


def _ada_kernel(c_ref, w_ref, b_ref, o_ref):
    s = _silu(c_ref[...]).astype(BF16)
    o_ref[0] = _dot(s, w_ref[0].astype(BF16)) + b_ref[0]


def _ada_all(cvec, ada_w, ada_b):
    tn = 1024
    n = N_MOD * D_MODEL
    return pl.pallas_call(
        _ada_kernel,
        grid=(DEPTH, n // tn),
        in_specs=[pl.BlockSpec((MOD_ROWS, D_MODEL), lambda l, j: (0, 0)),
                  pl.BlockSpec((1, D_MODEL, tn), lambda l, j: (l, 0, j)),
                  pl.BlockSpec((1, 1, tn), lambda l, j: (l, 0, j))],
        out_specs=pl.BlockSpec((1, MOD_ROWS, tn), lambda l, j: (l, 0, j)),
        out_shape=jax.ShapeDtypeStruct((DEPTH, MOD_ROWS, n), F32),
        compiler_params=_cparams(2),
        name="adaln",
    )(cvec, ada_w, ada_b.reshape(DEPTH, 1, n))


def _ffn_kernel(*refs, s, n_pre, final, layer, which):
    x_ref, mod_ref, g_ref, w13_hbm, w2_hbm = refs[:5]
    pre_y = refs[5:5 + n_pre]
    pre_w = refs[5 + n_pre:5 + 2 * n_pre]
    fg_ref = refs[5 + 2 * n_pre] if final else None
    o_ref, w13_ref, w2_ref, stage13, stage2, sem = refs[-6:]
    n_chunks = FFN_DIM // FFN_TF

    def chunk_copies(j):
        slot = j % (FFN_AHEAD + 1)
        cols = pl.ds(j * FFN_TF, FFN_TF)
        gate_cols = pl.ds(FFN_DIM + j * FFN_TF, FFN_TF)
        return (pltpu.make_async_copy(w13_hbm.at[layer, which, :, cols], stage13.at[slot, 0], sem.at[slot, 0]),
                pltpu.make_async_copy(w13_hbm.at[layer, which, :, gate_cols], stage13.at[slot, 1], sem.at[slot, 1]),
                pltpu.make_async_copy(w2_hbm.at[layer, which, cols, :], stage2.at[slot], sem.at[slot, 2]))

    def body(stream_weights):
        if stream_weights:
            for jj in range(FFN_AHEAD):
                for cp in chunk_copies(jj):
                    cp.start()
        x = x_ref[...]
        if n_pre:
            mix = None
            for y_ref, w_ref in zip(pre_y, pre_w):
                y = _dot(y_ref[...], w_ref[0])
                mix = y if mix is None else mix + y
            x = x + _gate(mod_ref, 1) * mix
        h = _modulated(x, mod_ref, g_ref, s).astype(BF16)
        acc = jnp.zeros(x.shape, F32)
        for j in range(n_chunks):
            up = slice(j * FFN_TF, (j + 1) * FFN_TF)
            gate = slice(FFN_DIM + j * FFN_TF, FFN_DIM + (j + 1) * FFN_TF)
            if stream_weights:
                if j + FFN_AHEAD < n_chunks:
                    for cp in chunk_copies(j + FFN_AHEAD):
                        cp.start()
                for cp in chunk_copies(j):
                    cp.wait()
                w13_ref[:, up] = stage13[j % (FFN_AHEAD + 1), 0].astype(BF16)
                w13_ref[:, gate] = stage13[j % (FFN_AHEAD + 1), 1].astype(BF16)
                w2_ref[up, :] = stage2[j % (FFN_AHEAD + 1)].astype(BF16)
            a = _dot(h, w13_ref[:, up])
            b = _dot(h, w13_ref[:, gate])
            u = (_silu(a) * b).astype(BF16)
            acc = acc + _dot(u, w2_ref[up, :])
        out = x + 0.5 * _gate(mod_ref, s) * acc
        if final:
            out = _rms(out, fg_ref[...])
        o_ref[...] = out

    first = pl.program_id(0) == 0
    pl.when(first)(functools.partial(body, True))
    pl.when(jnp.logical_not(first))(functools.partial(body, False))


def _ffn(x, mods, norm_g, w13, w2, layer, which, latent, seq, pre=None, final_g=None):
    tm = 512
    m = x.shape[0]
    assert m % tm == 0
    s = 0 if which == 0 else 2
    in_specs = [pl.BlockSpec((tm, D_MODEL), lambda i: (i, 0)),
                pl.BlockSpec((1, 1, N_MOD * D_MODEL), _mod_index(layer, latent, tm, seq)),
                pl.BlockSpec((1, 1, D_MODEL), lambda i: (layer * 3 + s, 0, 0)),
                pl.BlockSpec(memory_space=pl.ANY),
                pl.BlockSpec(memory_space=pl.ANY)]
    args = [x, mods, norm_g, w13, w2]
    n_pre = 0
    if pre is not None:
        ys, ws, w_layer = pre
        n_pre = len(ys)
        for y in ys:
            in_specs.append(pl.BlockSpec((tm, y.shape[1]), lambda i: (i, 0)))
        for w, rows, blk in ws:
            in_specs.append(_resident((1, rows, D_MODEL), lambda i, blk=blk: (w_layer, blk, 0)))
        args += list(ys) + [w for w, _, _ in ws]
    if final_g is not None:
        in_specs.append(pl.BlockSpec((1, D_MODEL), lambda i: (0, 0)))
        args.append(final_g.reshape(1, D_MODEL))
    return pl.pallas_call(
        functools.partial(_ffn_kernel, s=s, n_pre=n_pre, final=final_g is not None, layer=layer, which=which),
        grid=(m // tm,),
        in_specs=in_specs,
        out_specs=pl.BlockSpec((tm, D_MODEL), lambda i: (i, 0)),
        out_shape=jax.ShapeDtypeStruct((m, D_MODEL), F32),
        scratch_shapes=[pltpu.VMEM((D_MODEL, 2 * FFN_DIM), BF16),
                        pltpu.VMEM((FFN_DIM, D_MODEL), BF16),
                        pltpu.VMEM((FFN_AHEAD + 1, 2, D_MODEL, FFN_TF), F32),
                        pltpu.VMEM((FFN_AHEAD + 1, FFN_TF, D_MODEL), F32),
                        pltpu.SemaphoreType.DMA((FFN_AHEAD + 1, 3))],
        compiler_params=_cparams(1),
        name="ffn",
    )(*args)


_EV_QA = 0
_EV_KA = NA_WIDTH
_EV_VA = 2 * NA_WIDTH
_EV_CQ = 3 * NA_WIDTH
_EV_CKV = _EV_CQ + MLA_Q_RANK
_EV_COLS = _EV_CKV + MLA_KV_RANK + MLA_ROPE_DIM
MLA_QK_W = MLA_HEADS * LANES
LOG2E = 1.4426950408889634
NA_QSCALE = NA_HEAD_DIM ** -0.5 * LOG2E
MLA_QSCALE = (MLA_NOPE_DIM + MLA_ROPE_DIM) ** -0.5 * LOG2E


def _rope_swap_perm(width, half):
    j = np.arange(width)
    return np.where((j % (2 * half)) < half, j + half, j - half)


def _pack_even_weights(mix_w_in, mla_w_uq, mla_w_ukv):
    n_even = mix_w_in.shape[0]
    o = 3 * NA_WIDTH + MLA_Q_RANK + MLA_KV_RANK
    w_kr = mix_w_in[:, :, o:o + MLA_ROPE_DIM]
    w_krs = w_kr[:, :, _rope_swap_perm(MLA_ROPE_DIM, MLA_ROPE_DIM // 4)]
    pad = ((0, 0), (0, 0), (MLA_NOPE_DIM, LANES - MLA_NOPE_DIM - MLA_ROPE_DIM))
    w_kr2 = jnp.concatenate([jnp.pad(w_kr, pad), jnp.pad(w_krs, pad)], axis=-1)

    uq = mla_w_uq.reshape(n_even, MLA_Q_RANK, MLA_HEADS, MLA_NOPE_DIM + MLA_ROPE_DIM)
    uq_rope = uq[..., MLA_NOPE_DIM:]
    uq_rope_s = uq_rope[..., _rope_swap_perm(MLA_ROPE_DIM, MLA_ROPE_DIM // 4)]
    zpad = LANES - MLA_NOPE_DIM - MLA_ROPE_DIM
    w_uq = jnp.pad(uq, ((0, 0), (0, 0), (0, 0), (0, zpad))).reshape(n_even, MLA_Q_RANK, MLA_QK_W)
    w_uqs = jnp.pad(uq_rope_s, ((0, 0), (0, 0), (0, 0), (MLA_NOPE_DIM, zpad)))
    w_uqs = w_uqs.reshape(n_even, MLA_Q_RANK, MLA_QK_W)

    ukv = mla_w_ukv.reshape(n_even, MLA_KV_RANK, MLA_HEADS, MLA_NOPE_DIM + MLA_V_DIM)
    w_uk = jnp.pad(ukv[..., :MLA_NOPE_DIM], ((0, 0), (0, 0), (0, 0), (0, LANES - MLA_NOPE_DIM)))
    w_uk = w_uk.reshape(n_even, MLA_KV_RANK, MLA_QK_W)
    w_uv = ukv[..., MLA_NOPE_DIM:].reshape(n_even, MLA_KV_RANK, MLA_OUT)
    return (mix_w_in.astype(BF16), w_kr2.astype(BF16), w_uq.astype(BF16), w_uqs.astype(BF16),
            w_uk.astype(BF16), w_uv.astype(BF16))


def _rope_angles(seq, width, group):
    half = group // 2
    t = np.arange(seq)
    j = np.arange(width)
    pos = np.where((j // group)[None, :] % 2 == 0, (t // GRID_W)[:, None], (t % GRID_W)[:, None])
    inv = np.float32(ROPE_BASE) ** (-np.arange(half, dtype=np.float32) / np.float32(half))
    ang = pos.astype(np.float32) * inv[j % half][None, :]
    sign = np.where((j % group) < half, -1.0, 1.0).astype(np.float32)
    return ang, sign


def _mla_rope_tables(seq):
    ang, sign = _rope_angles(seq, MLA_ROPE_DIM, MLA_ROPE_DIM // 2)
    tail = LANES - MLA_NOPE_DIM - MLA_ROPE_DIM
    cos = np.concatenate([np.ones((seq, MLA_NOPE_DIM), np.float32), np.cos(ang),
                          np.ones((seq, tail), np.float32)], axis=1)
    sin = np.concatenate([np.zeros((seq, MLA_NOPE_DIM), np.float32), np.sin(ang) * sign[None, :],
                          np.zeros((seq, tail), np.float32)], axis=1)
    return jnp.asarray(cos), jnp.asarray(sin)


def _even_proj_kernel(*refs, latent, n_alias):
    (x_ref, mod_ref, g_ref, win_ref, wkr_ref, qn_ref, kvn_ref, wuq_ref, wuqs_ref, wuk_ref,
     wuv_ref) = refs[:11]
    if latent:
        cos_ref, sin_ref = refs[11:13]
        qa_ref, ka_ref, va_ref, qm_ref, km_ref, vm_ref = refs[13:]
    else:
        qa_ref, ka_ref, va_ref, qm_ref, km_ref, vm_ref, ckv_ref, kr_ref = refs[11 + n_alias:]

    def w_in(lo, width):
        return win_ref[0, :, lo:lo + width]

    h = _modulated(x_ref[...], mod_ref, g_ref, 1).astype(BF16)
    qa_ref[...] = (_dot(h, w_in(_EV_QA, NA_WIDTH)) * NA_QSCALE).astype(qa_ref.dtype)
    ka = _dot(h, w_in(_EV_KA, NA_WIDTH))
    va = _dot(h, w_in(_EV_VA, NA_WIDTH))
    if latent:
        ka_ref[...] = ka.astype(BF16)
        va_ref[...] = va.astype(BF16)
    else:
        ka_ref[:, 0] = ka.reshape(ka_ref.shape[0], ka_ref.shape[2], NA_WIDTH)
        va_ref[:, 0] = va.reshape(va_ref.shape[0], va_ref.shape[2], NA_WIDTH)

    cqn = _rms(_dot(h, w_in(_EV_CQ, MLA_Q_RANK)), qn_ref[0]).astype(BF16)
    ckvn = _rms(_dot(h, w_in(_EV_CKV, MLA_KV_RANK)), kvn_ref[0])
    kr = _dot(h, wkr_ref[0, :, :LANES])
    if latent:
        cos = cos_ref[...]
        sin = sin_ref[...]
        kr = kr * cos + _dot(h, wkr_ref[0, :, LANES:]) * sin
    else:
        ckv_ref[:, 0] = ckvn.reshape(ckv_ref.shape[0], ckv_ref.shape[2], MLA_KV_RANK)
        kr_ref[:, 0] = kr[:, MLA_NOPE_DIM:MLA_NOPE_DIM + MLA_ROPE_DIM].reshape(
            kr_ref.shape[0], kr_ref.shape[2], MLA_ROPE_DIM)
    ckvb = ckvn.astype(BF16)

    qm = _dot(cqn, wuq_ref[0])
    km = _dot(ckvb, wuk_ref[0])
    if latent:
        qms = _dot(cqn, wuqs_ref[0])
    for hd in range(MLA_HEADS):
        sl = slice(hd * LANES, (hd + 1) * LANES)
        q_h = qm[:, sl]
        if latent:
            q_h = q_h * cos + qms[:, sl] * sin
        qm_ref[:, sl] = (q_h * MLA_QSCALE).astype(BF16)
        km_ref[:, sl] = (km[:, sl] + kr).astype(BF16)
    vm_ref[...] = _dot(ckvb, wuv_ref[0]).astype(BF16)


def _even_proj(x, mods, norm_g, packed, q_norm, kv_norm, layer, latent, seq, n_even, rope=None, caches=None):
    w_in, w_kr2, w_uq, w_uqs, w_uk, w_uv = packed
    i_even = layer // 2
    m = x.shape[0]
    tm = 512
    bt = tm // seq if not latent else 0
    tok = lambda i: (i, 0)
    in_specs = [pl.BlockSpec((tm, D_MODEL), tok),
                pl.BlockSpec((1, 1, N_MOD * D_MODEL), _mod_index(layer, latent, tm, seq)),
                pl.BlockSpec((1, 1, D_MODEL), lambda i: (layer * 3 + 1, 0, 0)),
                _resident((1, D_MODEL, _EV_COLS), lambda i: (i_even, 0, 0)),
                _resident((1, D_MODEL, 2 * LANES), lambda i: (i_even, 0, 0)),
                pl.BlockSpec((1, 1, MLA_Q_RANK), lambda i: (i_even, 0, 0)),
                pl.BlockSpec((1, 1, MLA_KV_RANK), lambda i: (i_even, 0, 0)),
                _resident((1, MLA_Q_RANK, MLA_QK_W), lambda i: (i_even, 0, 0)),
                _resident((1, MLA_Q_RANK, MLA_QK_W), lambda i: (i_even, 0, 0)),
                _resident((1, MLA_KV_RANK, MLA_QK_W), lambda i: (i_even, 0, 0)),
                _resident((1, MLA_KV_RANK, MLA_OUT), lambda i: (i_even, 0, 0))]
    args = [x, mods, norm_g, w_in, w_kr2, q_norm, kv_norm, w_uq, w_uqs, w_uk, w_uv]
    bf = lambda w: jax.ShapeDtypeStruct((m, w), BF16)
    aliases = {}
    if latent:
        nt = seq // tm
        in_specs += [pl.BlockSpec((tm, LANES), lambda i: (i % nt, 0))] * 2
        args += list(rope)
        out_shape = [bf(NA_WIDTH), bf(NA_WIDTH), bf(NA_WIDTH), bf(MLA_QK_W), bf(MLA_QK_W), bf(MLA_OUT)]
        out_specs = [pl.BlockSpec((tm, s.shape[1]), tok) for s in out_shape]
    else:
        nb = m // seq
        cache = lambda w: jax.ShapeDtypeStruct((nb, n_even, seq, w), F32)
        cspec = lambda w: pl.BlockSpec((bt, 1, seq, w), lambda i: (i, i_even, 0, 0))
        out_shape = [bf(NA_WIDTH), cache(NA_WIDTH), cache(NA_WIDTH), bf(MLA_QK_W), bf(MLA_QK_W),
                     bf(MLA_OUT), cache(MLA_KV_RANK), cache(MLA_ROPE_DIM)]
        out_specs = [pl.BlockSpec((tm, NA_WIDTH), tok), cspec(NA_WIDTH), cspec(NA_WIDTH),
                     pl.BlockSpec((tm, MLA_QK_W), tok), pl.BlockSpec((tm, MLA_QK_W), tok),
                     pl.BlockSpec((tm, MLA_OUT), tok), cspec(MLA_KV_RANK), cspec(MLA_ROPE_DIM)]
        if caches is not None:
            for arr, out_idx in zip(caches, (1, 2, 6, 7)):
                aliases[len(args)] = out_idx
                in_specs.append(pl.BlockSpec(memory_space=pl.ANY))
                args.append(arr)
    return pl.pallas_call(
        functools.partial(_even_proj_kernel, latent=latent, n_alias=len(aliases)),
        grid=(m // tm,),
        in_specs=in_specs,
        out_specs=out_specs,
        out_shape=out_shape,
        input_output_aliases=aliases,
        compiler_params=_cparams(1),
        name="even_proj_latent" if latent else "even_proj_context",
    )(*args)


def _mla_cache_kernel(ckv_ref, kr_ref, wuk_ref, wuv_ref, km_ref, vm_ref):
    ckvb = ckv_ref[0, 0].astype(BF16)
    km = _dot(ckvb, wuk_ref[0])
    kr = kr_ref[...]
    for hd in range(MLA_HEADS):
        sl = slice(hd * LANES, (hd + 1) * LANES)
        km_ref[:, sl] = (km[:, sl] + kr).astype(BF16)
    vm_ref[...] = _dot(ckvb, wuv_ref[0]).astype(BF16)


def _mla_cache_keys(cache_ckv, kr_padded, w_uk, w_uv, i_even):
    nb, _, past, _ = cache_ckv.shape
    return pl.pallas_call(
        _mla_cache_kernel,
        grid=(nb,),
        in_specs=[pl.BlockSpec((1, 1, past, MLA_KV_RANK), lambda b: (b, i_even, 0, 0)),
                  pl.BlockSpec((past, LANES), lambda b: (b, 0)),
                  _resident((1, MLA_KV_RANK, MLA_QK_W), lambda b: (i_even, 0, 0)),
                  _resident((1, MLA_KV_RANK, MLA_OUT), lambda b: (i_even, 0, 0))],
        out_specs=[pl.BlockSpec((past, MLA_QK_W), lambda b: (b, 0)),
                   pl.BlockSpec((past, MLA_OUT), lambda b: (b, 0))],
        out_shape=[jax.ShapeDtypeStruct((nb * past, MLA_QK_W), BF16),
                   jax.ShapeDtypeStruct((nb * past, MLA_OUT), BF16)],
        compiler_params=_cparams(1),
        name="mla_cache_keys",
    )(cache_ckv, kr_padded, w_uk, w_uv)


def _half_masks():
    lane = lax.broadcasted_iota(jnp.int32, (1, LANES), 1)
    lo = lane < (LANES // 2)
    return lo, jnp.logical_not(lo)


def _scores_pass(q, key_blocks, bias_blocks, s_ref, kb):
    mx = None
    for j, (k_blk, bias) in enumerate(zip(key_blocks, bias_blocks)):
        sc = _dot_nt(q, k_blk())
        if bias is not None:
            sc = sc + bias()
        s_ref[:, j * kb:(j + 1) * kb] = sc
        for g in range(kb // LANES):
            part = sc[:, g * LANES:(g + 1) * LANES]
            mx = part if mx is None else jnp.maximum(mx, part)
    return jnp.max(mx, axis=-1, keepdims=True)


def _values_pass(s_ref, m, value_blocks, kb):
    ls = None
    acc = None
    for j, v_blk in enumerate(value_blocks):
        p = jnp.exp2(s_ref[:, j * kb:(j + 1) * kb] - m)
        for g in range(kb // LANES):
            part = p[:, g * LANES:(g + 1) * LANES]
            ls = part if ls is None else ls + part
        o = _dot(p.astype(BF16), v_blk())
        acc = o if acc is None else acc + o
    return acc / jnp.sum(ls, axis=-1, keepdims=True)


def _attention_chains(chains, s_scr, kb=None):
    kb = ATT_KB if kb is None else kb
    depth = s_scr.shape[0]
    outs = []
    maxes = {}
    for i in range(len(chains) + depth - 1):
        if i < len(chains):
            q, key_blocks, _, bias_blocks = chains[i]
            maxes[i] = _scores_pass(q(), key_blocks, bias_blocks, s_scr.at[i % depth], kb)
        j = i - (depth - 1)
        if j >= 0:
            outs.append(_values_pass(s_scr.at[j % depth], maxes.pop(j), chains[j][2], kb))
    return outs


def _rows(ref, start, size, lanes):
    return lambda: ref[start:start + size, lanes]


def _ctx_attn_kernel(qa_ref, ka_ref, va_ref, qm_ref, km_ref, vm_ref, o_ref, s_scr, kv_scr):
    lo, hi = _half_masks()
    zero = jnp.zeros((), BF16)
    seq = qa_ref.shape[0]
    kb_sz = min(ATT_KB, seq)
    qs_sz = min(ATT_QS, seq)
    kv_scr[0] = ka_ref[0, 0].astype(BF16)
    kv_scr[1] = va_ref[0, 0].astype(BF16)
    chains = []
    for g in range((NA_WIDTH + MLA_OUT) // LANES):
        mla = g >= NA_WIDTH // LANES
        gl = g - NA_WIDTH // LANES if mla else g
        sl = slice(gl * LANES, (gl + 1) * LANES)
        for qs in range(seq // qs_sz):
            rows = slice(qs * qs_sz, (qs + 1) * qs_sz)
            for t, msk in enumerate((lo, hi)):
                if mla:
                    hsl = slice((2 * gl + t) * LANES, (2 * gl + t + 1) * LANES)
                    q = lambda rows=rows, hsl=hsl: qm_ref[rows, hsl]
                    kb = [_rows(km_ref, j * kb_sz, kb_sz, hsl) for j in range(seq // kb_sz)]
                    vb = [_rows(vm_ref, j * kb_sz, kb_sz, sl) for j in range(seq // kb_sz)]
                else:
                    q = lambda rows=rows, sl=sl, msk=msk: jnp.where(msk, qa_ref[rows, sl], zero)
                    kb = [_rows(kv_scr.at[0], j * kb_sz, kb_sz, sl) for j in range(seq // kb_sz)]
                    vb = [_rows(kv_scr.at[1], j * kb_sz, kb_sz, sl) for j in range(seq // kb_sz)]
                chains.append((q, kb, vb, [None] * len(kb)))
    outs = _attention_chains(chains, s_scr, kb_sz)
    n = 0
    for g in range((NA_WIDTH + MLA_OUT) // LANES):
        for qs in range(seq // qs_sz):
            o_ref[qs * qs_sz:(qs + 1) * qs_sz, g * LANES:(g + 1) * LANES] = (
                jnp.where(lo, outs[n], outs[n + 1]).astype(BF16))
            n += 2


def _ctx_attention(qa, ka, va, qm, km, vm, seq, i_even):
    m = qa.shape[0]
    nb = m // seq
    tok = lambda b: (b, 0)
    cache = lambda b: (b, i_even, 0, 0)
    return pl.pallas_call(
        _ctx_attn_kernel,
        grid=(nb,),
        in_specs=[pl.BlockSpec((seq, NA_WIDTH), tok),
                  pl.BlockSpec((1, 1, seq, NA_WIDTH), cache),
                  pl.BlockSpec((1, 1, seq, NA_WIDTH), cache),
                  pl.BlockSpec((seq, MLA_QK_W), tok),
                  pl.BlockSpec((seq, MLA_QK_W), tok),
                  pl.BlockSpec((seq, MLA_OUT), tok)],
        out_specs=pl.BlockSpec((seq, NA_WIDTH + MLA_OUT), tok),
        out_shape=jax.ShapeDtypeStruct((m, NA_WIDTH + MLA_OUT), BF16),
        scratch_shapes=[pltpu.VMEM((2, min(ATT_QS, seq), seq), F32), pltpu.VMEM((2, seq, NA_WIDTH), BF16)],
        compiler_params=_cparams(1),
        name="context_attention",
    )(qa, ka, va, qm, km, vm)


def _lat_mla_kernel(q_ref, kc_ref, kl_ref, vc_ref, vl_ref, o_ref, s_scr):
    lo, _ = _half_masks()
    past, seq = kc_ref.shape[0], kl_ref.shape[0]
    all_lanes = slice(0, LANES)
    kbs = []
    for t in range(2):
        sl = slice(t * LANES, (t + 1) * LANES)
        kbs.append([_rows(kc_ref, j * ATT_KB, ATT_KB, sl) for j in range(past // ATT_KB)]
                   + [_rows(kl_ref, j * ATT_KB, ATT_KB, sl) for j in range(seq // ATT_KB)])
    vb = ([_rows(vc_ref, j * ATT_KB, ATT_KB, all_lanes) for j in range(past // ATT_KB)]
          + [_rows(vl_ref, j * ATT_KB, ATT_KB, all_lanes) for j in range(seq // ATT_KB)])
    tq = min(ATT_TQ, seq)
    n_qs = tq // ATT_QS

    def tile(i, carry):
        base = pl.multiple_of(i * tq, tq)
        chains = []
        for qs in range(n_qs):
            rows = pl.ds(base + qs * ATT_QS, ATT_QS)
            for t in range(2):
                sl = slice(t * LANES, (t + 1) * LANES)
                chains.append((lambda rows=rows, sl=sl: q_ref[rows, sl], kbs[t], vb, [None] * len(vb)))
        outs = _attention_chains(chains, s_scr)
        for qs in range(n_qs):
            o_ref[pl.ds(base + qs * ATT_QS, ATT_QS), :] = (
                jnp.where(lo, outs[2 * qs], outs[2 * qs + 1]).astype(BF16))
        return carry

    lax.fori_loop(0, seq // tq, tile, 0)


def _latent_mla_attention(qm, km_ctx, km_lat, vm_ctx, vm_lat, seq, past):
    m = qm.shape[0]
    nb = m // seq
    pair = 2 * LANES
    return pl.pallas_call(
        _lat_mla_kernel,
        grid=(nb, MLA_HEADS // 2),
        in_specs=[pl.BlockSpec((seq, pair), lambda b, g: (b, g)),
                  pl.BlockSpec((past, pair), lambda b, g: (b, g)),
                  pl.BlockSpec((seq, pair), lambda b, g: (b, g)),
                  pl.BlockSpec((past, LANES), lambda b, g: (b, g)),
                  pl.BlockSpec((seq, LANES), lambda b, g: (b, g))],
        out_specs=pl.BlockSpec((seq, LANES), lambda b, g: (b, g)),
        out_shape=jax.ShapeDtypeStruct((m, MLA_OUT), BF16),
        scratch_shapes=[pltpu.VMEM((2, ATT_QS, past + seq), F32)],
        compiler_params=_cparams(2),
        name="latent_mla_attention",
    )(qm, km_ctx, km_lat, vm_ctx, vm_lat)


def _na_window_start(first_row, rows, clip=np.clip):
    r0 = clip(first_row - NA_ROWS // 2, 0, rows - NA_ROWS)
    return clip(r0 - (NA_WROWS - NA_ROWS) // 2, 0, rows - NA_WROWS)


def _na_bias_kernel(rpb_ref, o_ref, *, rows):
    hd = pl.program_id(0)
    kr = min(NA_ROWS, rows)
    n_dc = 2 * NA_COLS - 1
    band = ATT_QS // GRID_W
    c = lax.broadcasted_iota(jnp.int32, (GRID_W, LANES), 0)
    lane = lax.broadcasted_iota(jnp.int32, (GRID_W, LANES), 1)
    n = lane % GRID_W
    c0 = jnp.clip(c - NA_COLS // 2, 0, GRID_W - NA_COLS)
    in_win = (n >= c0) & (n < c0 + NA_COLS)
    dc = n - c + NA_COLS - 1
    left = lane < GRID_W
    neg = jnp.full((GRID_W, LANES), NEG_BIG, F32)
    col_tables = []
    for dr in range(2 * NA_ROWS - 1):
        t = neg
        for j in range(n_dc):
            t = jnp.where(dc == j, rpb_ref[hd * (2 * NA_ROWS - 1) * n_dc + dr * n_dc + j] * LOG2E, t)
        col_tables.append(jnp.where(in_win, t, neg))
    for rb in range(rows // NA_QROWS):
        for rl in range(NA_QROWS):
            r = rb * NA_QROWS + rl
            r0 = min(max(r - kr // 2, 0), rows - kr)
            start = int(_na_window_start(r - rl % band, rows))
            assert start <= r0 and r0 + kr <= start + NA_WROWS
            for pr in range(NA_WROWS // 2):
                halves = []
                for key_row in (start + 2 * pr, start + 2 * pr + 1):
                    ok = r0 <= key_row < r0 + kr
                    halves.append(col_tables[key_row - r + NA_ROWS - 1] if ok else neg)
                o_ref[rb, 0, rl * GRID_W:(rl + 1) * GRID_W, pr * LANES:(pr + 1) * LANES] = (
                    jnp.where(left, halves[0], halves[1]))


def _na_bias(rpb, rows):
    n_rb = rows // NA_QROWS
    flat = rpb.reshape(-1)
    return pl.pallas_call(
        functools.partial(_na_bias_kernel, rows=rows),
        grid=(NA_HEADS,),
        in_specs=[pl.BlockSpec(memory_space=pltpu.SMEM)],
        out_specs=pl.BlockSpec((n_rb, 1, NA_QROWS * GRID_W, NA_WROWS * GRID_W), lambda h: (0, h, 0, 0)),
        out_shape=jax.ShapeDtypeStruct((n_rb, NA_HEADS, NA_QROWS * GRID_W, NA_WROWS * GRID_W), F32),
        compiler_params=_cparams(1),
        name="na_bias",
    )(flat)


def _lat_na_kernel(q_ref, k_ref, v_ref, kc_ref, vc_ref, bias_ref, o_ref, s_scr, kv_scr, *, rows):
    lo, hi = _half_masks()
    zero = jnp.zeros((), BF16)
    rb = pl.program_id(0)
    nk = NA_WROWS * GRID_W
    past = kc_ref.shape[2]
    all_lanes = slice(0, LANES)
    n_qs = q_ref.shape[1] // ATT_QS
    starts = [pl.multiple_of(_na_window_start(rb * NA_QROWS + qs * (ATT_QS // GRID_W), rows, jnp.clip) * GRID_W,
                             GRID_W) for qs in range(n_qs)]

    def batch(b, carry):
        kv_scr[0] = kc_ref[b, 0].astype(BF16)
        kv_scr[1] = vc_ref[b, 0].astype(BF16)

        def win(ref, start, j):
            return lambda: ref[b, pl.ds(start + j * ATT_KB, ATT_KB), :]

        ctx_k = [_rows(kv_scr.at[0], j * ATT_KB, ATT_KB, all_lanes) for j in range(past // ATT_KB)]
        ctx_v = [_rows(kv_scr.at[1], j * ATT_KB, ATT_KB, all_lanes) for j in range(past // ATT_KB)]
        chains = []
        for qs in range(n_qs):
            qrows = slice(qs * ATT_QS, (qs + 1) * ATT_QS)
            kb = [win(k_ref, starts[qs], j) for j in range(nk // ATT_KB)] + ctx_k
            vb = [win(v_ref, starts[qs], j) for j in range(nk // ATT_KB)] + ctx_v
            for t, msk in enumerate((lo, hi)):
                bias = ([(lambda t=t, j=j, qrows=qrows: bias_ref[0, t, qrows, j * ATT_KB:(j + 1) * ATT_KB])
                         for j in range(nk // ATT_KB)] + [None] * (past // ATT_KB))
                chains.append((lambda qrows=qrows, msk=msk: jnp.where(msk, q_ref[b, qrows, :], zero),
                               kb, vb, bias))
        outs = _attention_chains(chains, s_scr)
        for qs in range(n_qs):
            o_ref[b, qs * ATT_QS:(qs + 1) * ATT_QS, :] = (
                jnp.where(lo, outs[2 * qs], outs[2 * qs + 1]).astype(BF16))
        return carry

    lax.fori_loop(0, q_ref.shape[0], batch, 0, unroll=True)


def _latent_na_attention(qa, ka, va, cache_k, cache_v, bias, i_even, seq):
    m = qa.shape[0]
    nb = m // seq
    rows = seq // GRID_W
    n_rb = rows // NA_QROWS
    tq = NA_QROWS * GRID_W
    past = cache_k.shape[2]
    per_batch = lambda a: a.reshape(nb, seq, NA_WIDTH)
    out = pl.pallas_call(
        functools.partial(_lat_na_kernel, rows=rows),
        grid=(n_rb, NA_HEADS // 2),
        in_specs=[pl.BlockSpec((nb, tq, LANES), lambda r, g: (0, r, g)),
                  pl.BlockSpec((nb, seq, LANES), lambda r, g: (0, 0, g)),
                  pl.BlockSpec((nb, seq, LANES), lambda r, g: (0, 0, g)),
                  pl.BlockSpec((nb, 1, past, LANES), lambda r, g: (0, i_even, 0, g)),
                  pl.BlockSpec((nb, 1, past, LANES), lambda r, g: (0, i_even, 0, g)),
                  pl.BlockSpec((1, 2, tq, NA_WROWS * GRID_W), lambda r, g: (r, g, 0, 0))],
        out_specs=pl.BlockSpec((nb, tq, LANES), lambda r, g: (0, r, g)),
        out_shape=jax.ShapeDtypeStruct((nb, seq, NA_WIDTH), BF16),
        scratch_shapes=[pltpu.VMEM((2, ATT_QS, NA_WROWS * GRID_W + past), F32),
                        pltpu.VMEM((2, past, LANES), BF16)],
        compiler_params=_cparams(2),
        name="latent_na_attention",
    )(per_batch(qa), per_batch(ka), per_batch(va), cache_k, cache_v, bias)
    return out.reshape(m, NA_WIDTH)


_RET_NK = RET_HEADS * RET_DK
_RET_NV = RET_HEADS * RET_DV


def _ret_proj_kernel(x_ref, mod_ref, g_ref, w_ref, cos_ref, sin_ref, q_ref, k_ref, v_ref, gate_ref):
    h = _modulated(x_ref[...], mod_ref, g_ref, 1).astype(BF16)

    def rope(y):
        parts = [pltpu.roll(y[:, g * LANES:(g + 1) * LANES], LANES // 2, 1) for g in range(RET_DK // LANES)]
        return y * cos_ref[...] + jnp.concatenate(parts, axis=1) * sin_ref[...]

    for hd in range(RET_HEADS):
        sl = slice(hd * RET_DK, (hd + 1) * RET_DK)
        q_ref[:, sl] = rope(_dot(h, w_ref[0, :, sl])).astype(BF16)
        k = _dot(h, w_ref[0, :, _RET_NK + hd * RET_DK:_RET_NK + (hd + 1) * RET_DK]) * (RET_DK ** -0.5)
        k_ref[:, sl] = rope(k)
    step = 512
    for j in range(_RET_NV // step):
        o = 2 * _RET_NK + j * step
        v_ref[:, j * step:(j + 1) * step] = _dot(h, w_ref[0, :, o:o + step]).astype(BF16)
    for j in range(2 * _RET_NV // step):
        o = 2 * _RET_NK + _RET_NV + j * step
        gate_ref[:, j * step:(j + 1) * step] = _dot(h, w_ref[0, :, o:o + step])


def _ret_proj_latent(x, mods, norm_g, w_in, layer, seq, rope):
    m = x.shape[0]
    tm = 256
    i_odd = layer // 2
    tok = lambda i: (i, 0)
    n_in = 2 * _RET_NK + 3 * _RET_NV
    nt = seq // tm
    return pl.pallas_call(
        _ret_proj_kernel,
        grid=(m // tm,),
        in_specs=[pl.BlockSpec((tm, D_MODEL), tok),
                  pl.BlockSpec((1, 1, N_MOD * D_MODEL), _mod_index(layer, True, tm, seq)),
                  pl.BlockSpec((1, 1, D_MODEL), lambda i: (layer * 3 + 1, 0, 0)),
                  _resident((1, D_MODEL, n_in), lambda i: (i_odd, 0, 0)),
                  pl.BlockSpec((tm, RET_DK), lambda i: (i % nt, 0)),
                  pl.BlockSpec((tm, RET_DK), lambda i: (i % nt, 0))],
        out_specs=[pl.BlockSpec((tm, _RET_NK), tok),
                   pl.BlockSpec((tm, _RET_NK), tok),
                   pl.BlockSpec((tm, _RET_NV), tok),
                   pl.BlockSpec((tm, 2 * _RET_NV), tok)],
        out_shape=[jax.ShapeDtypeStruct((m, _RET_NK), BF16),
                   jax.ShapeDtypeStruct((m, _RET_NK), F32),
                   jax.ShapeDtypeStruct((m, _RET_NV), BF16),
                   jax.ShapeDtypeStruct((m, 2 * _RET_NV), F32)],
        compiler_params=_cparams(1),
        name="ret_proj_latent",
    )(x, mods, norm_g, w_in, *rope)


def _ret_rope_tables(seq):
    ang, sign = _rope_angles(seq, RET_DK, RET_DK // 2)
    return jnp.asarray(np.cos(ang)), jnp.asarray(np.sin(ang) * sign[None, :])


def _head_norm(o):
    mu = jnp.mean(o, axis=-1, keepdims=True)
    d = o - mu
    return d * lax.rsqrt(jnp.mean(d * d, axis=-1, keepdims=True) + EPS)


def _ret_decays(lg, dirn, c):
    ii = lax.broadcasted_iota(jnp.int32, (c, c), 0)
    jj = lax.broadcasted_iota(jnp.int32, (c, c), 1)
    row = lax.broadcasted_iota(jnp.int32, (c, 1), 0).astype(F32)
    diff = (ii - jj if dirn == 0 else jj - ii).astype(F32)
    intra = jnp.where(diff >= 0, jnp.exp(lg * jnp.maximum(diff, 0.0)), 0.0)
    if dirn == 0:
        q_dec = jnp.exp(lg * (row + 1.0))
        k_dec = jnp.exp(lg * (c - 1.0 - row))
    else:
        q_dec = jnp.exp(lg * (c - row))
        k_dec = jnp.exp(lg * row)
    return intra, q_dec, k_dec, jnp.exp(lg * float(c))


def _ret_ctx_kernel(*refs, n_alias):
    ld_ref, x_ref, mod_ref, g_ref, w_ref = refs[:5]
    y_ref, st_ref = refs[5 + n_alias:]
    c = x_ref.shape[0]
    h = _modulated(x_ref[...], mod_ref, g_ref, 1).astype(BF16)
    for hd in range(RET_HEADS):
        ksl = slice(hd * RET_DK, (hd + 1) * RET_DK)
        qb = _dot(h, w_ref[0, :, ksl]).astype(BF16)
        k = _dot(h, w_ref[0, :, _RET_NK + hd * RET_DK:_RET_NK + (hd + 1) * RET_DK]) * (RET_DK ** -0.5)
        vo = 2 * _RET_NK + hd * RET_DV
        v = _dot(h, w_ref[0, :, vo:vo + RET_DV]).astype(BF16)
        qk = _dot_nt(qb, k.astype(BF16))
        y = None
        for dirn in range(2):
            intra, _, k_dec, _ = _ret_decays(-jnp.abs(ld_ref[dirn, hd]), dirn, c)
            o = _dot((qk * intra).astype(BF16), v)
            st_ref[0, 0, dirn, hd] = _dot((k * k_dec).T.astype(BF16), v)
            go = 2 * _RET_NK + (1 + dirn) * _RET_NV + hd * RET_DV
            yd = _silu(_dot(h, w_ref[0, :, go:go + RET_DV])) * _head_norm(o)
            y = yd if y is None else y + yd
        y_ref[:, hd * RET_DV:(hd + 1) * RET_DV] = y.astype(BF16)


def _retention_context(x, mods, norm_g, w_in, log_decay, layer, seq, n_odd, prev_states=None):
    assert seq == RET_CHUNK
    m = x.shape[0]
    nb = m // seq
    i_odd = layer // 2
    n_in = 2 * _RET_NK + 3 * _RET_NV
    in_specs = [pl.BlockSpec(memory_space=pltpu.SMEM),
                pl.BlockSpec((seq, D_MODEL), lambda b: (b, 0)),
                pl.BlockSpec((1, 1, N_MOD * D_MODEL), _mod_index(layer, False, seq, seq)),
                pl.BlockSpec((1, 1, D_MODEL), lambda b: (layer * 3 + 1, 0, 0)),
                _resident((1, D_MODEL, n_in), lambda b: (i_odd, 0, 0))]
    args = [log_decay[i_odd], x, mods, norm_g, w_in]
    aliases = {}
    if prev_states is not None:
        aliases[len(args)] = 1
        in_specs.append(pl.BlockSpec(memory_space=pl.ANY))
        args.append(prev_states)
    return pl.pallas_call(
        functools.partial(_ret_ctx_kernel, n_alias=len(aliases)),
        grid=(nb,),
        in_specs=in_specs,
        out_specs=[pl.BlockSpec((seq, _RET_NV), lambda b: (b, 0)),
                   pl.BlockSpec((1, 1, 2, RET_HEADS, RET_DK, RET_DV), lambda b: (b, i_odd, 0, 0, 0, 0))],
        out_shape=[jax.ShapeDtypeStruct((m, _RET_NV), BF16),
                   jax.ShapeDtypeStruct((nb, n_odd, 2, RET_HEADS, RET_DK, RET_DV), F32)],
        input_output_aliases=aliases,
        compiler_params=_cparams(1),
        name="retention_context",
    )(*args)


def _ret_lat_kernel(ld_ref, q_ref, k_ref, v_ref, gf_ref, gb_ref, s0_ref, y_ref, s_scr, o_scr):
    hd = pl.program_id(1)
    c = RET_CHUNK
    n_chunks = q_ref.shape[0] // c
    decays = [_ret_decays(-jnp.abs(ld_ref[dirn, hd]), dirn, c) for dirn in range(2)]
    gates = (gf_ref, gb_ref)
    for dirn in range(2):
        s_scr[dirn] = s0_ref[0, 0, dirn, 0]

    def scan(first_visit):
        def body(t, carry):
            for dirn in range(2):
                intra, q_dec, k_dec, c_dec = decays[dirn]
                ci = t if dirn == 0 else n_chunks - 1 - t
                rows = pl.ds(pl.multiple_of(ci * c, c), c)
                qb = q_ref[rows, :]
                k = k_ref[rows, :]
                v = v_ref[rows, :]
                a = _dot_nt(qb, k.astype(BF16)) * intra
                s_prev = s_scr[dirn]
                o = _dot(a.astype(BF16), v) + _dot(qb, s_prev.astype(BF16)) * q_dec
                s_scr[dirn] = s_prev * c_dec + _dot((k * k_dec).T.astype(BF16), v)
                y = _silu(gates[dirn][rows, :]) * _head_norm(o)
                if first_visit:
                    o_scr[rows, :] = y
                else:
                    y_ref[rows, :] = (o_scr[rows, :] + y).astype(BF16)
            return carry
        return body

    lax.fori_loop(0, n_chunks // 2, scan(True), 0, unroll=True)
    lax.fori_loop(n_chunks // 2, n_chunks, scan(False), 0, unroll=True)


def _retention_latent(q, k, v, gates, log_decay, i_odd, seq, state):
    m = q.shape[0]
    nb = m // seq
    assert (seq // RET_CHUNK) % 2 == 0
    return pl.pallas_call(
        _ret_lat_kernel,
        grid=(nb, RET_HEADS),
        in_specs=[pl.BlockSpec(memory_space=pltpu.SMEM),
                  pl.BlockSpec((seq, RET_DK), lambda b, h: (b, h)),
                  pl.BlockSpec((seq, RET_DK), lambda b, h: (b, h)),
                  pl.BlockSpec((seq, RET_DV), lambda b, h: (b, h)),
                  pl.BlockSpec((seq, RET_DV), lambda b, h: (b, h)),
                  pl.BlockSpec((seq, RET_DV), lambda b, h: (b, RET_HEADS + h)),
                  pl.BlockSpec((1, 1, 2, 1, RET_DK, RET_DV), lambda b, h: (b, i_odd, 0, h, 0, 0))],
        out_specs=pl.BlockSpec((seq, RET_DV), lambda b, h: (b, h)),
        out_shape=jax.ShapeDtypeStruct((m, _RET_NV), BF16),
        scratch_shapes=[pltpu.VMEM((2, RET_DK, RET_DV), F32), pltpu.VMEM((seq, RET_DV), F32)],
        compiler_params=_cparams(2),
        name="retention_latent",
    )(log_decay[i_odd], q, k, v, gates, gates, state)


def kernel(x_prompt, x_sample, c, cache_na_k, cache_na_v, cache_mla_ckv, cache_mla_krope, state_ret,
           c_ctx, norm_g, ada_w, ada_b, ffn_w13, ffn_w2, mix_w_in, mla_q_norm, mla_kv_norm, mla_w_uq,
           mla_w_ukv, na_rpb, mix_w_out, ret_w_in, ret_log_decay, ret_w_out, final_norm_g):
    batch, seq, _ = x_prompt.shape
    dec_batch, dec_seq, _ = x_sample.shape
    past = cache_na_k.shape[2]
    n_even = mix_w_in.shape[0]
    assert dec_batch + 1 <= MOD_ROWS and seq == RET_CHUNK and dec_seq % RET_CHUNK == 0
    assert (dec_seq // GRID_W) % NA_QROWS == 0 and dec_seq // GRID_W >= NA_WROWS

    w_out_even = mix_w_out.astype(BF16)
    w_in_ret = ret_w_in.astype(BF16)
    w_out_ret = ret_w_out.astype(BF16)
    packed = _pack_even_weights(mix_w_in, mla_w_uq, mla_w_ukv)
    q_norm = mla_q_norm.reshape(n_even, 1, MLA_Q_RANK)
    kv_norm = mla_kv_norm.reshape(n_even, 1, MLA_KV_RANK)
    norm_g3 = norm_g.reshape(DEPTH * 3, 1, D_MODEL)
    mla_rope = _mla_rope_tables(dec_seq)
    ret_rope = _ret_rope_tables(dec_seq)
    cache_k = cache_na_k.reshape(dec_batch, n_even, past, NA_WIDTH)
    cache_v = cache_na_v.reshape(dec_batch, n_even, past, NA_WIDTH)

    cvec = jnp.concatenate([c_ctx[None, :], c, jnp.zeros((MOD_ROWS - 1 - dec_batch, D_MODEL), F32)], axis=0)
    mods = _ada_all(cvec, ada_w, ada_b).reshape(DEPTH * MOD_ROWS, 1, N_MOD * D_MODEL)

    xp = x_prompt.reshape(batch * seq, D_MODEL)
    xs = x_sample.reshape(dec_batch * dec_seq, D_MODEL)
    n_odd = ret_w_in.shape[0]
    caches = None
    states = None
    for layer in range(DEPTH):
        i = layer // 2
        xp = _ffn(xp, mods, norm_g3, ffn_w13, ffn_w2, layer, 0, False, seq)
        xs = _ffn(xs, mods, norm_g3, ffn_w13, ffn_w2, layer, 0, True, dec_seq)
        if layer % 2 == 0:
            qa, ka, va, qm, km, vm, ckv, kr = _even_proj(
                xp, mods, norm_g3, packed, q_norm, kv_norm, layer, False, seq, n_even, caches=caches)
            caches = (ka, va, ckv, kr)
            op = _ctx_attention(qa, ka, va, qm, km, vm, seq, i)
            pre_p = ([op], [(w_out_even, NA_WIDTH + MLA_OUT, 0)], i)

            qa, ka, va, qm, km, vm = _even_proj(
                xs, mods, norm_g3, packed, q_norm, kv_norm, layer, True, dec_seq, n_even, rope=mla_rope)
            kr_pad = jnp.pad(cache_mla_krope[:, i].reshape(dec_batch * past, MLA_ROPE_DIM),
                             ((0, 0), (MLA_NOPE_DIM, LANES - MLA_NOPE_DIM - MLA_ROPE_DIM)))
            km_ctx, vm_ctx = _mla_cache_keys(cache_mla_ckv, kr_pad, packed[4], packed[5], i)
            bias = _na_bias(na_rpb[i], dec_seq // GRID_W)
            oa = _latent_na_attention(qa, ka, va, cache_k, cache_v, bias, i, dec_seq)
            ob = _latent_mla_attention(qm, km_ctx, km, vm_ctx, vm, dec_seq, past)
            pre_s = ([oa, ob], [(w_out_even, NA_WIDTH, 0), (w_out_even, MLA_OUT, NA_WIDTH // MLA_OUT)], i)
        else:
            yp, states = _retention_context(xp, mods, norm_g3, w_in_ret, ret_log_decay, layer, seq, n_odd,
                                            prev_states=states)
            pre_p = ([yp], [(w_out_ret, _RET_NV, 0)], i)

            q, k, v, gates = _ret_proj_latent(xs, mods, norm_g3, w_in_ret, layer, dec_seq, ret_rope)
            ys = _retention_latent(q, k, v, gates, ret_log_decay, i, dec_seq, state_ret)
            pre_s = ([ys], [(w_out_ret, _RET_NV, 0)], i)
        last = final_norm_g if layer == DEPTH - 1 else None
        xp = _ffn(xp, mods, norm_g3, ffn_w13, ffn_w2, layer, 1, False, seq, pre=pre_p, final_g=last)
        xs = _ffn(xs, mods, norm_g3, ffn_w13, ffn_w2, layer, 1, True, dec_seq, pre=pre_s, final_g=last)

    y_prompt = xp.reshape(batch, seq, D_MODEL)
    y_sample = xs.reshape(dec_batch, dec_seq, D_MODEL)
    ka, va, ckv, kr = caches
    new_na_k = ka.reshape(batch, n_even, seq, NA_HEADS, NA_HEAD_DIM)
    new_na_v = va.reshape(batch, n_even, seq, NA_HEADS, NA_HEAD_DIM)
    return (y_prompt, y_sample, new_na_k, new_na_v, ckv, kr, states)
```

```python
import functools

import jax
import jax.numpy as jnp
import numpy as np
from jax import lax
from jax.experimental import pallas as pl
from jax.experimental.pallas import tpu as pltpu

F32 = jnp.float32
BF16 = jnp.bfloat16

D_MODEL = 1024
DEPTH = 4
GRID_W = 64
NA_HEADS = 8
NA_HEAD_DIM = 64
NA_ROWS = 8
NA_COLS = 16
MLA_HEADS = 8
MLA_Q_RANK = 384
MLA_KV_RANK = 256
MLA_NOPE_DIM = 64
MLA_ROPE_DIM = 32
MLA_V_DIM = 64
RET_HEADS = 4
RET_DK = D_MODEL // RET_HEADS
RET_DV = 2 * D_MODEL // RET_HEADS
FFN_DIM = ((8 * D_MODEL // 3 + 127) // 128) * 128
N_MOD = 9
ROPE_BASE = 10000.0
EPS = 1e-6
NA_WIDTH = NA_HEADS * NA_HEAD_DIM
MLA_OUT = MLA_HEADS * MLA_V_DIM

LANES = 128
MOD_ROWS = 8
NEG_BIG = -1e30
VMEM_LIMIT = 56 * 1024 * 1024

FFN_TM = 512
RET_PROJ_TM = 256
FFN_TF = 256
FFN_AHEAD = 1
RET_CHUNK = 256
NA_QROWS = 8
NA_WROWS = 12
ATT_QS = 128
ATT_KB = 256
ATT_TQ = 2048


def _cparams(n_axes):
    return pltpu.CompilerParams(dimension_semantics=("arbitrary",) * n_axes,
                                vmem_limit_bytes=VMEM_LIMIT)


def _resident(block, index_map):
    return pl.BlockSpec(block, index_map, pipeline_mode=pl.Buffered(1))


def _dot(a, b):
    return jnp.dot(a, b, preferred_element_type=F32)


def _dot_nt(a, b):
    return lax.dot_general(a, b, (((1,), (1,)), ((), ())), preferred_element_type=F32)


def _silu(x):
    return x * (0.5 * jnp.tanh(0.5 * x) + 0.5)


def _rms(x, g):
    return x * lax.rsqrt(jnp.mean(x * x, axis=-1, keepdims=True) + EPS) * g


def _modulated(x, mod_ref, g_ref, s):
    shift = mod_ref[0, :, 3 * s * D_MODEL:(3 * s + 1) * D_MODEL]
    scale = mod_ref[0, :, (3 * s + 1) * D_MODEL:(3 * s + 2) * D_MODEL]
    return _rms(x, g_ref[0]) * (1.0 + scale) + shift


def _gate(mod_ref, s):
    return mod_ref[0, :, (3 * s + 2) * D_MODEL:(3 * s + 3) * D_MODEL]


def _mod_index(layer, latent, tile_rows, seq):
    if not latent:
        return lambda i: (layer * MOD_ROWS, 0, 0)
    return lambda i: (layer * MOD_ROWS + 1 + (i * tile_rows) // seq, 0, 0)


def _ada_kernel(c_ref, w_ref, b_ref, o_ref):
    s = _silu(c_ref[...]).astype(BF16)
    o_ref[0] = _dot(s, w_ref[0].astype(BF16)) + b_ref[0]


def _ada_all(cvec, ada_w, ada_b):
    tn = 1024
    n = N_MOD * D_MODEL
    return pl.pallas_call(
        _ada_kernel,
        grid=(DEPTH, n // tn),
        in_specs=[pl.BlockSpec((MOD_ROWS, D_MODEL), lambda l, j: (0, 0)),
                  pl.BlockSpec((1, D_MODEL, tn), lambda l, j: (l, 0, j)),
                  pl.BlockSpec((1, 1, tn), lambda l, j: (l, 0, j))],
        out_specs=pl.BlockSpec((1, MOD_ROWS, tn), lambda l, j: (l, 0, j)),
        out_shape=jax.ShapeDtypeStruct((DEPTH, MOD_ROWS, n), F32),
        compiler_params=_cparams(2),
        name="adaln",
    )(cvec, ada_w, ada_b.reshape(DEPTH, 1, n))


def _ffn_kernel(*refs, s, n_pre, final, layer, which):
    x_ref, mod_ref, g_ref, w13_hbm, w2_hbm = refs[:5]
    pre_y = refs[5:5 + n_pre]
    pre_w = refs[5 + n_pre:5 + 2 * n_pre]
    fg_ref = refs[5 + 2 * n_pre] if final else None
    o_ref, w13_ref, w2_ref, stage13, stage2, sem = refs[-6:]
    n_chunks = FFN_DIM // FFN_TF

    def chunk_copies(j):
        slot = j % (FFN_AHEAD + 1)
        cols = pl.ds(j * FFN_TF, FFN_TF)
        gate_cols = pl.ds(FFN_DIM + j * FFN_TF, FFN_TF)
        return (pltpu.make_async_copy(w13_hbm.at[layer, which, :, cols], stage13.at[slot, 0], sem.at[slot, 0]),
                pltpu.make_async_copy(w13_hbm.at[layer, which, :, gate_cols], stage13.at[slot, 1], sem.at[slot, 1]),
                pltpu.make_async_copy(w2_hbm.at[layer, which, cols, :], stage2.at[slot], sem.at[slot, 2]))

    def body(stream_weights):
        if stream_weights:
            for jj in range(FFN_AHEAD):
                for cp in chunk_copies(jj):
                    cp.start()
        x = x_ref[...]
        if n_pre:
            mix = None
            for y_ref, w_ref in zip(pre_y, pre_w):
                y = _dot(y_ref[...], w_ref[0])
                mix = y if mix is None else mix + y
            x = x + _gate(mod_ref, 1) * mix
        h = _modulated(x, mod_ref, g_ref, s).astype(BF16)
        acc = jnp.zeros(x.shape, F32)
        for j in range(n_chunks):
            up = slice(j * FFN_TF, (j + 1) * FFN_TF)
            gate = slice(FFN_DIM + j * FFN_TF, FFN_DIM + (j + 1) * FFN_TF)
            if stream_weights:
                if j + FFN_AHEAD < n_chunks:
                    for cp in chunk_copies(j + FFN_AHEAD):
                        cp.start()
                for cp in chunk_copies(j):
                    cp.wait()
                w13_ref[:, up] = stage13[j % (FFN_AHEAD + 1), 0].astype(BF16)
                w13_ref[:, gate] = stage13[j % (FFN_AHEAD + 1), 1].astype(BF16)
                w2_ref[up, :] = stage2[j % (FFN_AHEAD + 1)].astype(BF16)
            a = _dot(h, w13_ref[:, up])
            b = _dot(h, w13_ref[:, gate])
            u = (_silu(a) * b).astype(BF16)
            acc = acc + _dot(u, w2_ref[up, :])
        out = x + 0.5 * _gate(mod_ref, s) * acc
        if final:
            out = _rms(out, fg_ref[...])
        o_ref[...] = out

    first = pl.program_id(0) == 0
    pl.when(first)(functools.partial(body, True))
    pl.when(jnp.logical_not(first))(functools.partial(body, False))


def _ffn(x, mods, norm_g, w13, w2, layer, which, latent, seq, pre=None, final_g=None):
    tm = FFN_TM
    m = x.shape[0]
    assert m % tm == 0
    s = 0 if which == 0 else 2
    in_specs = [pl.BlockSpec((tm, D_MODEL), lambda i: (i, 0)),
                pl.BlockSpec((1, 1, N_MOD * D_MODEL), _mod_index(layer, latent, tm, seq)),
                pl.BlockSpec((1, 1, D_MODEL), lambda i: (layer * 3 + s, 0, 0)),
                pl.BlockSpec(memory_space=pl.ANY),
                pl.BlockSpec(memory_space=pl.ANY)]
    args = [x, mods, norm_g, w13, w2]
    n_pre = 0
    if pre is not None:
        ys, ws, w_layer = pre
        n_pre = len(ys)
        for y in ys:
            in_specs.append(pl.BlockSpec((tm, y.shape[1]), lambda i: (i, 0)))
        for w, rows, blk in ws:
            in_specs.append(_resident((1, rows, D_MODEL), lambda i, blk=blk: (w_layer, blk, 0)))
        args += list(ys) + [w for w, _, _ in ws]
    if final_g is not None:
        in_specs.append(pl.BlockSpec((1, D_MODEL), lambda i: (0, 0)))
        args.append(final_g.reshape(1, D_MODEL))
    return pl.pallas_call(
        functools.partial(_ffn_kernel, s=s, n_pre=n_pre, final=final_g is not None, layer=layer, which=which),
        grid=(m // tm,),
        in_specs=in_specs,
        out_specs=pl.BlockSpec((tm, D_MODEL), lambda i: (i, 0)),
        out_shape=jax.ShapeDtypeStruct((m, D_MODEL), F32),
        scratch_shapes=[pltpu.VMEM((D_MODEL, 2 * FFN_DIM), BF16),
                        pltpu.VMEM((FFN_DIM, D_MODEL), BF16),
                        pltpu.VMEM((FFN_AHEAD + 1, 2, D_MODEL, FFN_TF), F32),
                        pltpu.VMEM((FFN_AHEAD + 1, FFN_TF, D_MODEL), F32),
                        pltpu.SemaphoreType.DMA((FFN_AHEAD + 1, 3))],
        compiler_params=_cparams(1),
        name="ffn",
    )(*args)


_EV_QA = 0
_EV_KA = NA_WIDTH
_EV_VA = 2 * NA_WIDTH
_EV_CQ = 3 * NA_WIDTH
_EV_CKV = _EV_CQ + MLA_Q_RANK
_EV_COLS = _EV_CKV + MLA_KV_RANK + MLA_ROPE_DIM
MLA_QK_W = MLA_HEADS * LANES
LOG2E = 1.4426950408889634
NA_QSCALE = NA_HEAD_DIM ** -0.5 * LOG2E
MLA_QSCALE = (MLA_NOPE_DIM + MLA_ROPE_DIM) ** -0.5 * LOG2E


def _rope_swap_perm(width, half):
    j = np.arange(width)
    return np.where((j % (2 * half)) < half, j + half, j - half)


def _pack_even_weights(mix_w_in, mla_w_uq, mla_w_ukv):
    n_even = mix_w_in.shape[0]
    o = 3 * NA_WIDTH + MLA_Q_RANK + MLA_KV_RANK
    w_kr = mix_w_in[:, :, o:o + MLA_ROPE_DIM]
    w_krs = w_kr[:, :, _rope_swap_perm(MLA_ROPE_DIM, MLA_ROPE_DIM // 4)]
    pad = ((0, 0), (0, 0), (MLA_NOPE_DIM, LANES - MLA_NOPE_DIM - MLA_ROPE_DIM))
    w_kr2 = jnp.concatenate([jnp.pad(w_kr, pad), jnp.pad(w_krs, pad)], axis=-1)

    uq = mla_w_uq.reshape(n_even, MLA_Q_RANK, MLA_HEADS, MLA_NOPE_DIM + MLA_ROPE_DIM)
    uq_rope = uq[..., MLA_NOPE_DIM:]
    uq_rope_s = uq_rope[..., _rope_swap_perm(MLA_ROPE_DIM, MLA_ROPE_DIM // 4)]
    zpad = LANES - MLA_NOPE_DIM - MLA_ROPE_DIM
    w_uq = jnp.pad(uq, ((0, 0), (0, 0), (0, 0), (0, zpad))).reshape(n_even, MLA_Q_RANK, MLA_QK_W)
    w_uqs = jnp.pad(uq_rope_s, ((0, 0), (0, 0), (0, 0), (MLA_NOPE_DIM, zpad)))
    w_uqs = w_uqs.reshape(n_even, MLA_Q_RANK, MLA_QK_W)

    ukv = mla_w_ukv.reshape(n_even, MLA_KV_RANK, MLA_HEADS, MLA_NOPE_DIM + MLA_V_DIM)
    w_uk = jnp.pad(ukv[..., :MLA_NOPE_DIM], ((0, 0), (0, 0), (0, 0), (0, LANES - MLA_NOPE_DIM)))
    w_uk = w_uk.reshape(n_even, MLA_KV_RANK, MLA_QK_W)
    w_uv = ukv[..., MLA_NOPE_DIM:].reshape(n_even, MLA_KV_RANK, MLA_OUT)
    return (mix_w_in.astype(BF16), w_kr2.astype(BF16), w_uq.astype(BF16), w_uqs.astype(BF16),
            w_uk.astype(BF16), w_uv.astype(BF16))


def _rope_angles(seq, width, group):
    half = group // 2
    t = np.arange(seq)
    j = np.arange(width)
    pos = np.where((j // group)[None, :] % 2 == 0, (t // GRID_W)[:, None], (t % GRID_W)[:, None])
    inv = np.float32(ROPE_BASE) ** (-np.arange(half, dtype=np.float32) / np.float32(half))
    ang = pos.astype(np.float32) * inv[j % half][None, :]
    sign = np.where((j % group) < half, -1.0, 1.0).astype(np.float32)
    return ang, sign


def _mla_rope_tables(seq):
    ang, sign = _rope_angles(seq, MLA_ROPE_DIM, MLA_ROPE_DIM // 2)
    tail = LANES - MLA_NOPE_DIM - MLA_ROPE_DIM
    cos = np.concatenate([np.ones((seq, MLA_NOPE_DIM), np.float32), np.cos(ang),
                          np.ones((seq, tail), np.float32)], axis=1)
    sin = np.concatenate([np.zeros((seq, MLA_NOPE_DIM), np.float32), np.sin(ang) * sign[None, :],
                          np.zeros((seq, tail), np.float32)], axis=1)
    return jnp.asarray(cos), jnp.asarray(sin)


def _even_proj_kernel(*refs, latent, n_alias):
    (x_ref, mod_ref, g_ref, win_ref, wkr_ref, qn_ref, kvn_ref, wuq_ref, wuqs_ref, wuk_ref,
     wuv_ref) = refs[:11]
    if latent:
        cos_ref, sin_ref = refs[11:13]
        qa_ref, ka_ref, va_ref, qm_ref, km_ref, vm_ref = refs[13:]
    else:
        qa_ref, ka_ref, va_ref, qm_ref, km_ref, vm_ref, ckv_ref, kr_ref = refs[11 + n_alias:]

    def w_in(lo, width):
        return win_ref[0, :, lo:lo + width]

    h = _modulated(x_ref[...], mod_ref, g_ref, 1).astype(BF16)
    qa_ref[...] = (_dot(h, w_in(_EV_QA, NA_WIDTH)) * NA_QSCALE).astype(qa_ref.dtype)
    ka = _dot(h, w_in(_EV_KA, NA_WIDTH))
    va = _dot(h, w_in(_EV_VA, NA_WIDTH))
    if latent:
        ka_ref[...] = ka.astype(BF16)
        va_ref[...] = va.astype(BF16)
    else:
        ka_ref[:, 0] = ka.reshape(ka_ref.shape[0], ka_ref.shape[2], NA_WIDTH)
        va_ref[:, 0] = va.reshape(va_ref.shape[0], va_ref.shape[2], NA_WIDTH)

    cqn = _rms(_dot(h, w_in(_EV_CQ, MLA_Q_RANK)), qn_ref[0]).astype(BF16)
    ckvn = _rms(_dot(h, w_in(_EV_CKV, MLA_KV_RANK)), kvn_ref[0])
    kr = _dot(h, wkr_ref[0, :, :LANES])
    if latent:
        cos = cos_ref[...]
        sin = sin_ref[...]
        kr = kr * cos + _dot(h, wkr_ref[0, :, LANES:]) * sin
    else:
        ckv_ref[:, 0] = ckvn.reshape(ckv_ref.shape[0], ckv_ref.shape[2], MLA_KV_RANK)
        kr_ref[:, 0] = kr[:, MLA_NOPE_DIM:MLA_NOPE_DIM + MLA_ROPE_DIM].reshape(
            kr_ref.shape[0], kr_ref.shape[2], MLA_ROPE_DIM)
    ckvb = ckvn.astype(BF16)

    qm = _dot(cqn, wuq_ref[0])
    km = _dot(ckvb, wuk_ref[0])
    if latent:
        qms = _dot(cqn, wuqs_ref[0])
    for hd in range(MLA_HEADS):
        sl = slice(hd * LANES, (hd + 1) * LANES)
        q_h = qm[:, sl]
        if latent:
            q_h = q_h * cos + qms[:, sl] * sin
        qm_ref[:, sl] = (q_h * MLA_QSCALE).astype(BF16)
        km_ref[:, sl] = (km[:, sl] + kr).astype(BF16)
    vm_ref[...] = _dot(ckvb, wuv_ref[0]).astype(BF16)


def _even_proj(x, mods, norm_g, packed, q_norm, kv_norm, layer, latent, seq, n_even, rope=None, caches=None):
    w_in, w_kr2, w_uq, w_uqs, w_uk, w_uv = packed
    i_even = layer // 2
    m = x.shape[0]
    tm = FFN_TM
    assert m % tm == 0 and (latent or tm % seq == 0)
    bt = tm // seq if not latent else 0
    tok = lambda i: (i, 0)
    in_specs = [pl.BlockSpec((tm, D_MODEL), tok),
                pl.BlockSpec((1, 1, N_MOD * D_MODEL), _mod_index(layer, latent, tm, seq)),
                pl.BlockSpec((1, 1, D_MODEL), lambda i: (layer * 3 + 1, 0, 0)),
                _resident((1, D_MODEL, _EV_COLS), lambda i: (i_even, 0, 0)),
                _resident((1, D_MODEL, 2 * LANES), lambda i: (i_even, 0, 0)),
                pl.BlockSpec((1, 1, MLA_Q_RANK), lambda i: (i_even, 0, 0)),
                pl.BlockSpec((1, 1, MLA_KV_RANK), lambda i: (i_even, 0, 0)),
                _resident((1, MLA_Q_RANK, MLA_QK_W), lambda i: (i_even, 0, 0)),
                _resident((1, MLA_Q_RANK, MLA_QK_W), lambda i: (i_even, 0, 0)),
                _resident((1, MLA_KV_RANK, MLA_QK_W), lambda i: (i_even, 0, 0)),
                _resident((1, MLA_KV_RANK, MLA_OUT), lambda i: (i_even, 0, 0))]
    args = [x, mods, norm_g, w_in, w_kr2, q_norm, kv_norm, w_uq, w_uqs, w_uk, w_uv]
    bf = lambda w: jax.ShapeDtypeStruct((m, w), BF16)
    aliases = {}
    if latent:
        nt = seq // tm
        in_specs += [pl.BlockSpec((tm, LANES), lambda i: (i % nt, 0))] * 2
        args += list(rope)
        out_shape = [bf(NA_WIDTH), bf(NA_WIDTH), bf(NA_WIDTH), bf(MLA_QK_W), bf(MLA_QK_W), bf(MLA_OUT)]
        out_specs = [pl.BlockSpec((tm, s.shape[1]), tok) for s in out_shape]
    else:
        nb = m // seq
        cache = lambda w: jax.ShapeDtypeStruct((nb, n_even, seq, w), F32)
        cspec = lambda w: pl.BlockSpec((bt, 1, seq, w), lambda i: (i, i_even, 0, 0))
        out_shape = [bf(NA_WIDTH), cache(NA_WIDTH), cache(NA_WIDTH), bf(MLA_QK_W), bf(MLA_QK_W),
                     bf(MLA_OUT), cache(MLA_KV_RANK), cache(MLA_ROPE_DIM)]
        out_specs = [pl.BlockSpec((tm, NA_WIDTH), tok), cspec(NA_WIDTH), cspec(NA_WIDTH),
                     pl.BlockSpec((tm, MLA_QK_W), tok), pl.BlockSpec((tm, MLA_QK_W), tok),
                     pl.BlockSpec((tm, MLA_OUT), tok), cspec(MLA_KV_RANK), cspec(MLA_ROPE_DIM)]
        if caches is not None:
            for arr, out_idx in zip(caches, (1, 2, 6, 7)):
                aliases[len(args)] = out_idx
                in_specs.append(pl.BlockSpec(memory_space=pl.ANY))
                args.append(arr)
    return pl.pallas_call(
        functools.partial(_even_proj_kernel, latent=latent, n_alias=len(aliases)),
        grid=(m // tm,),
        in_specs=in_specs,
        out_specs=out_specs,
        out_shape=out_shape,
        input_output_aliases=aliases,
        compiler_params=_cparams(1),
        name="even_proj_latent" if latent else "even_proj_context",
    )(*args)


def _mla_cache_kernel(ckv_ref, kr_ref, wuk_ref, wuv_ref, km_ref, vm_ref):
    ckvb = ckv_ref[0, 0].astype(BF16)
    km = _dot(ckvb, wuk_ref[0])
    kr = kr_ref[...]
    for hd in range(MLA_HEADS):
        sl = slice(hd * LANES, (hd + 1) * LANES)
        km_ref[:, sl] = (km[:, sl] + kr).astype(BF16)
    vm_ref[...] = _dot(ckvb, wuv_ref[0]).astype(BF16)


def _mla_cache_keys(cache_ckv, kr_padded, w_uk, w_uv, i_even):
    nb, _, past, _ = cache_ckv.shape
    return pl.pallas_call(
        _mla_cache_kernel,
        grid=(nb,),
        in_specs=[pl.BlockSpec((1, 1, past, MLA_KV_RANK), lambda b: (b, i_even, 0, 0)),
                  pl.BlockSpec((past, LANES), lambda b: (b, 0)),
                  _resident((1, MLA_KV_RANK, MLA_QK_W), lambda b: (i_even, 0, 0)),
                  _resident((1, MLA_KV_RANK, MLA_OUT), lambda b: (i_even, 0, 0))],
        out_specs=[pl.BlockSpec((past, MLA_QK_W), lambda b: (b, 0)),
                   pl.BlockSpec((past, MLA_OUT), lambda b: (b, 0))],
        out_shape=[jax.ShapeDtypeStruct((nb * past, MLA_QK_W), BF16),
                   jax.ShapeDtypeStruct((nb * past, MLA_OUT), BF16)],
        compiler_params=_cparams(1),
        name="mla_cache_keys",
    )(cache_ckv, kr_padded, w_uk, w_uv)


def _half_masks():
    lane = lax.broadcasted_iota(jnp.int32, (1, LANES), 1)
    lo = lane < (LANES // 2)
    return lo, jnp.logical_not(lo)


def _scores_pass(q, key_blocks, bias_blocks, s_ref, kb):
    mx = None
    for j, (k_blk, bias) in enumerate(zip(key_blocks, bias_blocks)):
        sc = _dot_nt(q, k_blk())
        if bias is not None:
            sc = sc + bias()
        s_ref[:, j * kb:(j + 1) * kb] = sc
        for g in range(kb // LANES):
            part = sc[:, g * LANES:(g + 1) * LANES]
            mx = part if mx is None else jnp.maximum(mx, part)
    return jnp.max(mx, axis=-1, keepdims=True)


def _values_pass(s_ref, m, value_blocks, kb):
    ls = None
    acc = None
    for j, v_blk in enumerate(value_blocks):
        p = jnp.exp2(s_ref[:, j * kb:(j + 1) * kb] - m)
        for g in range(kb // LANES):
            part = p[:, g * LANES:(g + 1) * LANES]
            ls = part if ls is None else ls + part
        o = _dot(p.astype(BF16), v_blk())
        acc = o if acc is None else acc + o
    return acc / jnp.sum(ls, axis=-1, keepdims=True)


def _attention_chains(chains, s_scr, kb=None):
    kb = ATT_KB if kb is None else kb
    depth = s_scr.shape[0]
    outs = []
    maxes = {}
    for i in range(len(chains) + depth - 1):
        if i < len(chains):
            q, key_blocks, _, bias_blocks = chains[i]
            maxes[i] = _scores_pass(q(), key_blocks, bias_blocks, s_scr.at[i % depth], kb)
        j = i - (depth - 1)
        if j >= 0:
            outs.append(_values_pass(s_scr.at[j % depth], maxes.pop(j), chains[j][2], kb))
    return outs


def _rows(ref, start, size, lanes):
    return lambda: ref[start:start + size, lanes]


def _ctx_attn_kernel(qa_ref, ka_ref, va_ref, qm_ref, km_ref, vm_ref, o_ref, s_scr, kv_scr):
    lo, hi = _half_masks()
    zero = jnp.zeros((), BF16)
    seq = qa_ref.shape[0]
    kb_sz = min(ATT_KB, seq)
    qs_sz = min(ATT_QS, seq)
    kv_scr[0] = ka_ref[0, 0].astype(BF16)
    kv_scr[1] = va_ref[0, 0].astype(BF16)
    chains = []
    for g in range((NA_WIDTH + MLA_OUT) // LANES):
        mla = g >= NA_WIDTH // LANES
        gl = g - NA_WIDTH // LANES if mla else g
        sl = slice(gl * LANES, (gl + 1) * LANES)
        for qs in range(seq // qs_sz):
            rows = slice(qs * qs_sz, (qs + 1) * qs_sz)
            for t, msk in enumerate((lo, hi)):
                if mla:
                    hsl = slice((2 * gl + t) * LANES, (2 * gl + t + 1) * LANES)
                    q = lambda rows=rows, hsl=hsl: qm_ref[rows, hsl]
                    kb = [_rows(km_ref, j * kb_sz, kb_sz, hsl) for j in range(seq // kb_sz)]
                    vb = [_rows(vm_ref, j * kb_sz, kb_sz, sl) for j in range(seq // kb_sz)]
                else:
                    q = lambda rows=rows, sl=sl, msk=msk: jnp.where(msk, qa_ref[rows, sl], zero)
                    kb = [_rows(kv_scr.at[0], j * kb_sz, kb_sz, sl) for j in range(seq // kb_sz)]
                    vb = [_rows(kv_scr.at[1], j * kb_sz, kb_sz, sl) for j in range(seq // kb_sz)]
                chains.append((q, kb, vb, [None] * len(kb)))
    outs = _attention_chains(chains, s_scr, kb_sz)
    n = 0
    for g in range((NA_WIDTH + MLA_OUT) // LANES):
        for qs in range(seq // qs_sz):
            o_ref[qs * qs_sz:(qs + 1) * qs_sz, g * LANES:(g + 1) * LANES] = (
                jnp.where(lo, outs[n], outs[n + 1]).astype(BF16))
            n += 2


def _ctx_attention(qa, ka, va, qm, km, vm, seq, i_even):
    m = qa.shape[0]
    nb = m // seq
    tok = lambda b: (b, 0)
    cache = lambda b: (b, i_even, 0, 0)
    return pl.pallas_call(
        _ctx_attn_kernel,
        grid=(nb,),
        in_specs=[pl.BlockSpec((seq, NA_WIDTH), tok),
                  pl.BlockSpec((1, 1, seq, NA_WIDTH), cache),
                  pl.BlockSpec((1, 1, seq, NA_WIDTH), cache),
                  pl.BlockSpec((seq, MLA_QK_W), tok),
                  pl.BlockSpec((seq, MLA_QK_W), tok),
                  pl.BlockSpec((seq, MLA_OUT), tok)],
        out_specs=pl.BlockSpec((seq, NA_WIDTH + MLA_OUT), tok),
        out_shape=jax.ShapeDtypeStruct((m, NA_WIDTH + MLA_OUT), BF16),
        scratch_shapes=[pltpu.VMEM((2, min(ATT_QS, seq), seq), F32), pltpu.VMEM((2, seq, NA_WIDTH), BF16)],
        compiler_params=_cparams(1),
        name="context_attention",
    )(qa, ka, va, qm, km, vm)


def _lat_mla_kernel(q_ref, kc_ref, kl_ref, vc_ref, vl_ref, o_ref, s_scr):
    lo, _ = _half_masks()
    past, seq = kc_ref.shape[0], kl_ref.shape[0]
    all_lanes = slice(0, LANES)
    kbs = []
    for t in range(2):
        sl = slice(t * LANES, (t + 1) * LANES)
        kbs.append([_rows(kc_ref, j * ATT_KB, ATT_KB, sl) for j in range(past // ATT_KB)]
                   + [_rows(kl_ref, j * ATT_KB, ATT_KB, sl) for j in range(seq // ATT_KB)])
    vb = ([_rows(vc_ref, j * ATT_KB, ATT_KB, all_lanes) for j in range(past // ATT_KB)]
          + [_rows(vl_ref, j * ATT_KB, ATT_KB, all_lanes) for j in range(seq // ATT_KB)])
    tq = min(ATT_TQ, seq)
    n_qs = tq // ATT_QS

    def tile(i, carry):
        base = pl.multiple_of(i * tq, tq)
        chains = []
        for qs in range(n_qs):
            rows = pl.ds(base + qs * ATT_QS, ATT_QS)
            for t in range(2):
                sl = slice(t * LANES, (t + 1) * LANES)
                chains.append((lambda rows=rows, sl=sl: q_ref[rows, sl], kbs[t], vb, [None] * len(vb)))
        outs = _attention_chains(chains, s_scr)
        for qs in range(n_qs):
            o_ref[pl.ds(base + qs * ATT_QS, ATT_QS), :] = (
                jnp.where(lo, outs[2 * qs], outs[2 * qs + 1]).astype(BF16))
        return carry

    lax.fori_loop(0, seq // tq, tile, 0)


def _latent_mla_attention(qm, km_ctx, km_lat, vm_ctx, vm_lat, seq, past):
    m = qm.shape[0]
    nb = m // seq
    pair = 2 * LANES
    return pl.pallas_call(
        _lat_mla_kernel,
        grid=(nb, MLA_HEADS // 2),
        in_specs=[pl.BlockSpec((seq, pair), lambda b, g: (b, g)),
                  pl.BlockSpec((past, pair), lambda b, g: (b, g)),
                  pl.BlockSpec((seq, pair), lambda b, g: (b, g)),
                  pl.BlockSpec((past, LANES), lambda b, g: (b, g)),
                  pl.BlockSpec((seq, LANES), lambda b, g: (b, g))],
        out_specs=pl.BlockSpec((seq, LANES), lambda b, g: (b, g)),
        out_shape=jax.ShapeDtypeStruct((m, MLA_OUT), BF16),
        scratch_shapes=[pltpu.VMEM((2, ATT_QS, past + seq), F32)],
        compiler_params=_cparams(2),
        name="latent_mla_attention",
    )(qm, km_ctx, km_lat, vm_ctx, vm_lat)


def _na_window_start(first_row, rows, clip=np.clip):
    r0 = clip(first_row - NA_ROWS // 2, 0, rows - NA_ROWS)
    return clip(r0 - (NA_WROWS - NA_ROWS) // 2, 0, rows - NA_WROWS)


def _na_bias_kernel(rpb_ref, o_ref, *, rows):
    hd = pl.program_id(0)
    kr = min(NA_ROWS, rows)
    n_dc = 2 * NA_COLS - 1
    band = ATT_QS // GRID_W
    c = lax.broadcasted_iota(jnp.int32, (GRID_W, LANES), 0)
    lane = lax.broadcasted_iota(jnp.int32, (GRID_W, LANES), 1)
    n = lane % GRID_W
    c0 = jnp.clip(c - NA_COLS // 2, 0, GRID_W - NA_COLS)
    in_win = (n >= c0) & (n < c0 + NA_COLS)
    dc = n - c + NA_COLS - 1
    left = lane < GRID_W
    neg = jnp.full((GRID_W, LANES), NEG_BIG, F32)
    col_tables = []
    for dr in range(2 * NA_ROWS - 1):
        t = neg
        for j in range(n_dc):
            t = jnp.where(dc == j, rpb_ref[hd * (2 * NA_ROWS - 1) * n_dc + dr * n_dc + j] * LOG2E, t)
        col_tables.append(jnp.where(in_win, t, neg))
    for rb in range(rows // NA_QROWS):
        for rl in range(NA_QROWS):
            r = rb * NA_QROWS + rl
            r0 = min(max(r - kr // 2, 0), rows - kr)
            start = int(_na_window_start(r - rl % band, rows))
            assert start <= r0 and r0 + kr <= start + NA_WROWS
            for pr in range(NA_WROWS // 2):
                halves = []
                for key_row in (start + 2 * pr, start + 2 * pr + 1):
                    ok = r0 <= key_row < r0 + kr
                    halves.append(col_tables[key_row - r + NA_ROWS - 1] if ok else neg)
                o_ref[rb, 0, rl * GRID_W:(rl + 1) * GRID_W, pr * LANES:(pr + 1) * LANES] = (
                    jnp.where(left, halves[0], halves[1]))


def _na_bias(rpb, rows):
    n_rb = rows // NA_QROWS
    flat = rpb.reshape(-1)
    return pl.pallas_call(
        functools.partial(_na_bias_kernel, rows=rows),
        grid=(NA_HEADS,),
        in_specs=[pl.BlockSpec(memory_space=pltpu.SMEM)],
        out_specs=pl.BlockSpec((n_rb, 1, NA_QROWS * GRID_W, NA_WROWS * GRID_W), lambda h: (0, h, 0, 0)),
        out_shape=jax.ShapeDtypeStruct((n_rb, NA_HEADS, NA_QROWS * GRID_W, NA_WROWS * GRID_W), F32),
        compiler_params=_cparams(1),
        name="na_bias",
    )(flat)


def _lat_na_kernel(q_ref, k_ref, v_ref, kc_ref, vc_ref, bias_ref, o_ref, s_scr, kv_scr, *, rows):
    lo, hi = _half_masks()
    zero = jnp.zeros((), BF16)
    rb = pl.program_id(0)
    nk = NA_WROWS * GRID_W
    past = kc_ref.shape[2]
    all_lanes = slice(0, LANES)
    n_qs = q_ref.shape[1] // ATT_QS
    starts = [pl.multiple_of(_na_window_start(rb * NA_QROWS + qs * (ATT_QS // GRID_W), rows, jnp.clip) * GRID_W,
                             GRID_W) for qs in range(n_qs)]

    def batch(b, carry):
        kv_scr[0] = kc_ref[b, 0].astype(BF16)
        kv_scr[1] = vc_ref[b, 0].astype(BF16)

        def win(ref, start, j):
            return lambda: ref[b, pl.ds(start + j * ATT_KB, ATT_KB), :]

        ctx_k = [_rows(kv_scr.at[0], j * ATT_KB, ATT_KB, all_lanes) for j in range(past // ATT_KB)]
        ctx_v = [_rows(kv_scr.at[1], j * ATT_KB, ATT_KB, all_lanes) for j in range(past // ATT_KB)]
        chains = []
        for qs in range(n_qs):
            qrows = slice(qs * ATT_QS, (qs + 1) * ATT_QS)
            kb = [win(k_ref, starts[qs], j) for j in range(nk // ATT_KB)] + ctx_k
            vb = [win(v_ref, starts[qs], j) for j in range(nk // ATT_KB)] + ctx_v
            for t, msk in enumerate((lo, hi)):
                bias = ([(lambda t=t, j=j, qrows=qrows: bias_ref[0, t, qrows, j * ATT_KB:(j + 1) * ATT_KB])
                         for j in range(nk // ATT_KB)] + [None] * (past // ATT_KB))
                chains.append((lambda qrows=qrows, msk=msk: jnp.where(msk, q_ref[b, qrows, :], zero),
                               kb, vb, bias))
        outs = _attention_chains(chains, s_scr)
        for qs in range(n_qs):
            o_ref[b, qs * ATT_QS:(qs + 1) * ATT_QS, :] = (
                jnp.where(lo, outs[2 * qs], outs[2 * qs + 1]).astype(BF16))
        return carry

    lax.fori_loop(0, q_ref.shape[0], batch, 0, unroll=True)


def _latent_na_attention(qa, ka, va, cache_k, cache_v, bias, i_even, seq):
    m = qa.shape[0]
    nb = m // seq
    rows = seq // GRID_W
    n_rb = rows // NA_QROWS
    tq = NA_QROWS * GRID_W
    past = cache_k.shape[2]
    per_batch = lambda a: a.reshape(nb, seq, NA_WIDTH)
    out = pl.pallas_call(
        functools.partial(_lat_na_kernel, rows=rows),
        grid=(n_rb, NA_HEADS // 2),
        in_specs=[pl.BlockSpec((nb, tq, LANES), lambda r, g: (0, r, g)),
                  pl.BlockSpec((nb, seq, LANES), lambda r, g: (0, 0, g)),
                  pl.BlockSpec((nb, seq, LANES), lambda r, g: (0, 0, g)),
                  pl.BlockSpec((nb, 1, past, LANES), lambda r, g: (0, i_even, 0, g)),
                  pl.BlockSpec((nb, 1, past, LANES), lambda r, g: (0, i_even, 0, g)),
                  pl.BlockSpec((1, 2, tq, NA_WROWS * GRID_W), lambda r, g: (r, g, 0, 0))],
        out_specs=pl.BlockSpec((nb, tq, LANES), lambda r, g: (0, r, g)),
        out_shape=jax.ShapeDtypeStruct((nb, seq, NA_WIDTH), BF16),
        scratch_shapes=[pltpu.VMEM((2, ATT_QS, NA_WROWS * GRID_W + past), F32),
                        pltpu.VMEM((2, past, LANES), BF16)],
        compiler_params=_cparams(2),
        name="latent_na_attention",
    )(per_batch(qa), per_batch(ka), per_batch(va), cache_k, cache_v, bias)
    return out.reshape(m, NA_WIDTH)


_RET_NK = RET_HEADS * RET_DK
_RET_NV = RET_HEADS * RET_DV


def _ret_proj_kernel(x_ref, mod_ref, g_ref, w_ref, cos_ref, sin_ref, q_ref, k_ref, v_ref, gate_ref):
    h = _modulated(x_ref[...], mod_ref, g_ref, 1).astype(BF16)

    def rope(y):
        parts = [pltpu.roll(y[:, g * LANES:(g + 1) * LANES], LANES // 2, 1) for g in range(RET_DK // LANES)]
        return y * cos_ref[...] + jnp.concatenate(parts, axis=1) * sin_ref[...]

    for hd in range(RET_HEADS):
        sl = slice(hd * RET_DK, (hd + 1) * RET_DK)
        q_ref[:, sl] = rope(_dot(h, w_ref[0, :, sl])).astype(BF16)
        k = _dot(h, w_ref[0, :, _RET_NK + hd * RET_DK:_RET_NK + (hd + 1) * RET_DK]) * (RET_DK ** -0.5)
        k_ref[:, sl] = rope(k)
    step = 512
    for j in range(_RET_NV // step):
        o = 2 * _RET_NK + j * step
        v_ref[:, j * step:(j + 1) * step] = _dot(h, w_ref[0, :, o:o + step]).astype(BF16)
    for j in range(2 * _RET_NV // step):
        o = 2 * _RET_NK + _RET_NV + j * step
        gate_ref[:, j * step:(j + 1) * step] = _dot(h, w_ref[0, :, o:o + step])


def _ret_proj_latent(x, mods, norm_g, w_in, layer, seq, rope):
    m = x.shape[0]
    tm = RET_PROJ_TM
    assert m % tm == 0 and seq % tm == 0
    i_odd = layer // 2
    tok = lambda i: (i, 0)
    n_in = 2 * _RET_NK + 3 * _RET_NV
    nt = seq // tm
    return pl.pallas_call(
        _ret_proj_kernel,
        grid=(m // tm,),
        in_specs=[pl.BlockSpec((tm, D_MODEL), tok),
                  pl.BlockSpec((1, 1, N_MOD * D_MODEL), _mod_index(layer, True, tm, seq)),
                  pl.BlockSpec((1, 1, D_MODEL), lambda i: (layer * 3 + 1, 0, 0)),
                  _resident((1, D_MODEL, n_in), lambda i: (i_odd, 0, 0)),
                  pl.BlockSpec((tm, RET_DK), lambda i: (i % nt, 0)),
                  pl.BlockSpec((tm, RET_DK), lambda i: (i % nt, 0))],
        out_specs=[pl.BlockSpec((tm, _RET_NK), tok),
                   pl.BlockSpec((tm, _RET_NK), tok),
                   pl.BlockSpec((tm, _RET_NV), tok),
                   pl.BlockSpec((tm, 2 * _RET_NV), tok)],
        out_shape=[jax.ShapeDtypeStruct((m, _RET_NK), BF16),
                   jax.ShapeDtypeStruct((m, _RET_NK), F32),
                   jax.ShapeDtypeStruct((m, _RET_NV), BF16),
                   jax.ShapeDtypeStruct((m, 2 * _RET_NV), F32)],
        compiler_params=_cparams(1),
        name="ret_proj_latent",
    )(x, mods, norm_g, w_in, *rope)


def _ret_rope_tables(seq):
    ang, sign = _rope_angles(seq, RET_DK, RET_DK // 2)
    return jnp.asarray(np.cos(ang)), jnp.asarray(np.sin(ang) * sign[None, :])


def _gated_head_norm(g, o):
    mu = jnp.mean(o, axis=-1, keepdims=True)
    d = o - mu
    half_rs = 0.5 * lax.rsqrt(jnp.mean(d * d, axis=-1, keepdims=True) + EPS)
    u = g * d
    return (u + u * jnp.tanh(0.5 * g)) * half_rs


def _ret_decays(lg, dirn, c):
    ii = lax.broadcasted_iota(jnp.int32, (c, c), 0)
    jj = lax.broadcasted_iota(jnp.int32, (c, c), 1)
    row = lax.broadcasted_iota(jnp.int32, (c, 1), 0).astype(F32)
    diff = (ii - jj if dirn == 0 else jj - ii).astype(F32)
    intra = jnp.where(diff >= 0, jnp.exp(lg * jnp.maximum(diff, 0.0)), 0.0)
    if dirn == 0:
        q_dec = jnp.exp(lg * (row + 1.0))
        k_dec = jnp.exp(lg * (c - 1.0 - row))
    else:
        q_dec = jnp.exp(lg * (c - row))
        k_dec = jnp.exp(lg * row)
    return intra, q_dec, k_dec, jnp.exp(lg * float(c))


def _ret_ctx_kernel(*refs, n_alias):
    ld_ref, x_ref, mod_ref, g_ref, w_ref = refs[:5]
    y_ref, st_ref = refs[5 + n_alias:]
    c = x_ref.shape[0]
    h = _modulated(x_ref[...], mod_ref, g_ref, 1).astype(BF16)
    for hd in range(RET_HEADS):
        ksl = slice(hd * RET_DK, (hd + 1) * RET_DK)
        qb = _dot(h, w_ref[0, :, ksl]).astype(BF16)
        k = _dot(h, w_ref[0, :, _RET_NK + hd * RET_DK:_RET_NK + (hd + 1) * RET_DK]) * (RET_DK ** -0.5)
        vo = 2 * _RET_NK + hd * RET_DV
        v = _dot(h, w_ref[0, :, vo:vo + RET_DV]).astype(BF16)
        qk = _dot_nt(qb, k.astype(BF16))
        y = None
        for dirn in range(2):
            intra, _, k_dec, _ = _ret_decays(-jnp.abs(ld_ref[dirn, hd]), dirn, c)
            o = _dot((qk * intra).astype(BF16), v)
            st_ref[0, 0, dirn, hd] = _dot((k * k_dec).T.astype(BF16), v)
            go = 2 * _RET_NK + (1 + dirn) * _RET_NV + hd * RET_DV
            yd = _gated_head_norm(_dot(h, w_ref[0, :, go:go + RET_DV]), o)
            y = yd if y is None else y + yd
        y_ref[:, hd * RET_DV:(hd + 1) * RET_DV] = y.astype(BF16)


def _retention_context(x, mods, norm_g, w_in, log_decay, layer, seq, n_odd, prev_states=None):
    assert seq == RET_CHUNK
    m = x.shape[0]
    nb = m // seq
    i_odd = layer // 2
    n_in = 2 * _RET_NK + 3 * _RET_NV
    in_specs = [pl.BlockSpec(memory_space=pltpu.SMEM),
                pl.BlockSpec((seq, D_MODEL), lambda b: (b, 0)),
                pl.BlockSpec((1, 1, N_MOD * D_MODEL), _mod_index(layer, False, seq, seq)),
                pl.BlockSpec((1, 1, D_MODEL), lambda b: (layer * 3 + 1, 0, 0)),
                _resident((1, D_MODEL, n_in), lambda b: (i_odd, 0, 0))]
    args = [log_decay[i_odd], x, mods, norm_g, w_in]
    aliases = {}
    if prev_states is not None:
        aliases[len(args)] = 1
        in_specs.append(pl.BlockSpec(memory_space=pl.ANY))
        args.append(prev_states)
    return pl.pallas_call(
        functools.partial(_ret_ctx_kernel, n_alias=len(aliases)),
        grid=(nb,),
        in_specs=in_specs,
        out_specs=[pl.BlockSpec((seq, _RET_NV), lambda b: (b, 0)),
                   pl.BlockSpec((1, 1, 2, RET_HEADS, RET_DK, RET_DV), lambda b: (b, i_odd, 0, 0, 0, 0))],
        out_shape=[jax.ShapeDtypeStruct((m, _RET_NV), BF16),
                   jax.ShapeDtypeStruct((nb, n_odd, 2, RET_HEADS, RET_DK, RET_DV), F32)],
        input_output_aliases=aliases,
        compiler_params=_cparams(1),
        name="retention_context",
    )(*args)


def _ret_lat_kernel(ld_ref, q_ref, k_ref, v_ref, gf_ref, gb_ref, s0_ref, y_ref, s_scr, o_scr):
    hd = pl.program_id(1)
    c = RET_CHUNK
    n_chunks = q_ref.shape[0] // c
    decays = [_ret_decays(-jnp.abs(ld_ref[dirn, hd]), dirn, c) for dirn in range(2)]
    gates = (gf_ref, gb_ref)
    for dirn in range(2):
        s_scr[dirn] = s0_ref[0, 0, dirn, 0]

    def scan(first_visit):
        def body(t, carry):
            for dirn in range(2):
                intra, q_dec, k_dec, c_dec = decays[dirn]
                ci = t if dirn == 0 else n_chunks - 1 - t
                rows = pl.ds(pl.multiple_of(ci * c, c), c)
                qb = q_ref[rows, :]
                k = k_ref[rows, :]
                v = v_ref[rows, :]
                a = _dot_nt(qb, k.astype(BF16)) * intra
                s_prev = s_scr[dirn]
                o = _dot(a.astype(BF16), v) + _dot(qb, s_prev.astype(BF16)) * q_dec
                s_scr[dirn] = s_prev * c_dec + _dot((k * k_dec).T.astype(BF16), v)
                y = _gated_head_norm(gates[dirn][rows, :], o)
                if first_visit:
                    o_scr[rows, :] = y
                else:
                    y_ref[rows, :] = (o_scr[rows, :] + y).astype(BF16)
            return carry
        return body

    lax.fori_loop(0, n_chunks // 2, scan(True), 0, unroll=True)
    lax.fori_loop(n_chunks // 2, n_chunks, scan(False), 0, unroll=True)


def _retention_latent(q, k, v, gates, log_decay, i_odd, seq, state):
    m = q.shape[0]
    nb = m // seq
    assert (seq // RET_CHUNK) % 2 == 0
    return pl.pallas_call(
        _ret_lat_kernel,
        grid=(nb, RET_HEADS),
        in_specs=[pl.BlockSpec(memory_space=pltpu.SMEM),
                  pl.BlockSpec((seq, RET_DK), lambda b, h: (b, h)),
                  pl.BlockSpec((seq, RET_DK), lambda b, h: (b, h)),
                  pl.BlockSpec((seq, RET_DV), lambda b, h: (b, h)),
                  pl.BlockSpec((seq, RET_DV), lambda b, h: (b, h)),
                  pl.BlockSpec((seq, RET_DV), lambda b, h: (b, RET_HEADS + h)),
                  pl.BlockSpec((1, 1, 2, 1, RET_DK, RET_DV), lambda b, h: (b, i_odd, 0, h, 0, 0))],
        out_specs=pl.BlockSpec((seq, RET_DV), lambda b, h: (b, h)),
        out_shape=jax.ShapeDtypeStruct((m, _RET_NV), BF16),
        scratch_shapes=[pltpu.VMEM((2, RET_DK, RET_DV), F32), pltpu.VMEM((seq, RET_DV), F32)],
        compiler_params=_cparams(2),
        name="retention_latent",
    )(log_decay[i_odd], q, k, v, gates, gates, state)


def kernel(x_prompt, x_sample, c, cache_na_k, cache_na_v, cache_mla_ckv, cache_mla_krope, state_ret,
           c_ctx, norm_g, ada_w, ada_b, ffn_w13, ffn_w2, mix_w_in, mla_q_norm, mla_kv_norm, mla_w_uq,
           mla_w_ukv, na_rpb, mix_w_out, ret_w_in, ret_log_decay, ret_w_out, final_norm_g):
    batch, seq, _ = x_prompt.shape
    dec_batch, dec_seq, _ = x_sample.shape
    past = cache_na_k.shape[2]
    n_even = mix_w_in.shape[0]
    assert dec_batch + 1 <= MOD_ROWS and seq == RET_CHUNK and dec_seq % RET_CHUNK == 0
    assert (dec_seq // GRID_W) % NA_QROWS == 0 and dec_seq // GRID_W >= NA_WROWS

    w_out_even = mix_w_out.astype(BF16)
    w_in_ret = ret_w_in.astype(BF16)
    w_out_ret = ret_w_out.astype(BF16)
    packed = _pack_even_weights(mix_w_in, mla_w_uq, mla_w_ukv)
    q_norm = mla_q_norm.reshape(n_even, 1, MLA_Q_RANK)
    kv_norm = mla_kv_norm.reshape(n_even, 1, MLA_KV_RANK)
    norm_g3 = norm_g.reshape(DEPTH * 3, 1, D_MODEL)
    mla_rope = _mla_rope_tables(dec_seq)
    ret_rope = _ret_rope_tables(dec_seq)
    cache_k = cache_na_k.reshape(dec_batch, n_even, past, NA_WIDTH)
    cache_v = cache_na_v.reshape(dec_batch, n_even, past, NA_WIDTH)

    cvec = jnp.concatenate([c_ctx[None, :], c, jnp.zeros((MOD_ROWS - 1 - dec_batch, D_MODEL), F32)], axis=0)
    mods = _ada_all(cvec, ada_w, ada_b).reshape(DEPTH * MOD_ROWS, 1, N_MOD * D_MODEL)

    xp = x_prompt.reshape(batch * seq, D_MODEL)
    xs = x_sample.reshape(dec_batch * dec_seq, D_MODEL)
    n_odd = ret_w_in.shape[0]
    caches = None
    states = None
    for layer in range(DEPTH):
        i = layer // 2
        xp = _ffn(xp, mods, norm_g3, ffn_w13, ffn_w2, layer, 0, False, seq)
        xs = _ffn(xs, mods, norm_g3, ffn_w13, ffn_w2, layer, 0, True, dec_seq)
        if layer % 2 == 0:
            qa, ka, va, qm, km, vm, ckv, kr = _even_proj(
                xp, mods, norm_g3, packed, q_norm, kv_norm, layer, False, seq, n_even, caches=caches)
            caches = (ka, va, ckv, kr)
            op = _ctx_attention(qa, ka, va, qm, km, vm, seq, i)
            pre_p = ([op], [(w_out_even, NA_WIDTH + MLA_OUT, 0)], i)

            qa, ka, va, qm, km, vm = _even_proj(
                xs, mods, norm_g3, packed, q_norm, kv_norm, layer, True, dec_seq, n_even, rope=mla_rope)
            kr_pad = jnp.pad(cache_mla_krope[:, i].reshape(dec_batch * past, MLA_ROPE_DIM),
                             ((0, 0), (MLA_NOPE_DIM, LANES - MLA_NOPE_DIM - MLA_ROPE_DIM)))
            km_ctx, vm_ctx = _mla_cache_keys(cache_mla_ckv, kr_pad, packed[4], packed[5], i)
            bias = _na_bias(na_rpb[i], dec_seq // GRID_W)
            oa = _latent_na_attention(qa, ka, va, cache_k, cache_v, bias, i, dec_seq)
            ob = _latent_mla_attention(qm, km_ctx, km, vm_ctx, vm, dec_seq, past)
            pre_s = ([oa, ob], [(w_out_even, NA_WIDTH, 0), (w_out_even, MLA_OUT, NA_WIDTH // MLA_OUT)], i)
        else:
            yp, states = _retention_context(xp, mods, norm_g3, w_in_ret, ret_log_decay, layer, seq, n_odd,
                                            prev_states=states)
            pre_p = ([yp], [(w_out_ret, _RET_NV, 0)], i)

            q, k, v, gates = _ret_proj_latent(xs, mods, norm_g3, w_in_ret, layer, dec_seq, ret_rope)
            ys = _retention_latent(q, k, v, gates, ret_log_decay, i, dec_seq, state_ret)
            pre_s = ([ys], [(w_out_ret, _RET_NV, 0)], i)
        last = final_norm_g if layer == DEPTH - 1 else None
        xp = _ffn(xp, mods, norm_g3, ffn_w13, ffn_w2, layer, 1, False, seq, pre=pre_p, final_g=last)
        xs = _ffn(xs, mods, norm_g3, ffn_w13, ffn_w2, layer, 1, True, dec_seq, pre=pre_s, final_g=last)

    y_prompt = xp.reshape(batch, seq, D_MODEL)
    y_sample = xs.reshape(dec_batch, dec_seq, D_MODEL)
    ka, va, ckv, kr = caches
    new_na_k = ka.reshape(batch, n_even, seq, NA_HEADS, NA_HEAD_DIM)
    new_na_v = va.reshape(batch, n_even, seq, NA_HEADS, NA_HEAD_DIM)
    return (y_prompt, y_sample, new_na_k, new_na_v, ckv, kr, states)
```

```python
import functools

import jax
import jax.numpy as jnp
import numpy as np
from jax import lax
from jax.experimental import pallas as pl
from jax.experimental.pallas import tpu as pltpu

F32 = jnp.float32
BF16 = jnp.bfloat16

D_MODEL = 1024
DEPTH = 4
GRID_W = 64
NA_HEADS = 8
NA_HEAD_DIM = 64
NA_ROWS = 8
NA_COLS = 16
MLA_HEADS = 8
MLA_Q_RANK = 384
MLA_KV_RANK = 256
MLA_NOPE_DIM = 64
MLA_ROPE_DIM = 32
MLA_V_DIM = 64
RET_HEADS = 4
RET_DK = D_MODEL // RET_HEADS
RET_DV = 2 * D_MODEL // RET_HEADS
FFN_DIM = ((8 * D_MODEL // 3 + 127) // 128) * 128
N_MOD = 9
ROPE_BASE = 10000.0
EPS = 1e-6
NA_WIDTH = NA_HEADS * NA_HEAD_DIM
MLA_OUT = MLA_HEADS * MLA_V_DIM

LANES = 128
MOD_ROWS = 8
NEG_BIG = -1e30
VMEM_LIMIT = 56 * 1024 * 1024

ADA_TK = 256
FFN_TM = 512
RET_PROJ_TM = 256
FFN_TF = 256
FFN_AHEAD = 1
RET_CHUNK = 256
NA_QROWS = 8
NA_WROWS = 12
ATT_QS = 128
ATT_KB = 256
CTX_BATCHES = 4
ATT_TQ = 2048


def _cparams(n_axes):
    return pltpu.CompilerParams(dimension_semantics=("arbitrary",) * n_axes,
                                vmem_limit_bytes=VMEM_LIMIT)


def _resident(block, index_map):
    return pl.BlockSpec(block, index_map, pipeline_mode=pl.Buffered(1))


def _dot(a, b):
    return jnp.dot(a, b, preferred_element_type=F32)


def _dot_nt(a, b):
    return lax.dot_general(a, b, (((1,), (1,)), ((), ())), preferred_element_type=F32)


def _silu(x):
    return x * (0.5 * jnp.tanh(0.5 * x) + 0.5)


def _rms(x, g):
    return x * lax.rsqrt(jnp.mean(x * x, axis=-1, keepdims=True) + EPS) * g


def _modulated(x, mod_ref, g_ref, s):
    shift = mod_ref[0, :, 3 * s * D_MODEL:(3 * s + 1) * D_MODEL]
    scale = mod_ref[0, :, (3 * s + 1) * D_MODEL:(3 * s + 2) * D_MODEL]
    return _rms(x, g_ref[0]) * (1.0 + scale) + shift


def _gate(mod_ref, s):
    return mod_ref[0, :, (3 * s + 2) * D_MODEL:(3 * s + 3) * D_MODEL]


def _mod_index(layer, latent, tile_rows, seq):
    if not latent:
        return lambda i: (layer * MOD_ROWS, 0, 0)
    return lambda i: (layer * MOD_ROWS + 1 + (i * tile_rows) // seq, 0, 0)


def _ada_kernel(c_ref, w_ref, b_ref, o_ref):
    @pl.when(pl.program_id(1) == 0)
    def _():
        o_ref[0] = jnp.broadcast_to(b_ref[0], o_ref.shape[1:])

    s = _silu(c_ref[...]).astype(BF16)
    o_ref[0] += _dot(s, w_ref[0].astype(BF16))


def _ada_all(cvec, ada_w, ada_b):
    tk = ADA_TK
    n = N_MOD * D_MODEL
    return pl.pallas_call(
        _ada_kernel,
        grid=(DEPTH, D_MODEL // tk),
        in_specs=[pl.BlockSpec((MOD_ROWS, tk), lambda l, k: (0, k)),
                  pl.BlockSpec((1, tk, n), lambda l, k: (l, k, 0)),
                  pl.BlockSpec((1, 1, n), lambda l, k: (l, 0, 0))],
        out_specs=pl.BlockSpec((1, MOD_ROWS, n), lambda l, k: (l, 0, 0)),
        out_shape=jax.ShapeDtypeStruct((DEPTH, MOD_ROWS, n), F32),
        compiler_params=_cparams(2),
        name="adaln",
    )(cvec, ada_w, ada_b.reshape(DEPTH, 1, n))


def _ffn_kernel(*refs, s, n_pre, final, layer, which):
    x_ref, mod_ref, g_ref, w13_hbm, w2_hbm = refs[:5]
    pre_y = refs[5:5 + n_pre]
    pre_w = refs[5 + n_pre:5 + 2 * n_pre]
    fg_ref = refs[5 + 2 * n_pre] if final else None
    o_ref, w13_ref, w2_ref, stage13, stage2, sem = refs[-6:]
    n_chunks = FFN_DIM // FFN_TF

    def chunk_copies(j):
        slot = j % (FFN_AHEAD + 1)
        cols = pl.ds(j * FFN_TF, FFN_TF)
        gate_cols = pl.ds(FFN_DIM + j * FFN_TF, FFN_TF)
        return (pltpu.make_async_copy(w13_hbm.at[layer, which, :, cols], stage13.at[slot, 0], sem.at[slot, 0]),
                pltpu.make_async_copy(w13_hbm.at[layer, which, :, gate_cols], stage13.at[slot, 1], sem.at[slot, 1]),
                pltpu.make_async_copy(w2_hbm.at[layer, which, cols, :], stage2.at[slot], sem.at[slot, 2]))

    def body(stream_weights):
        if stream_weights:
            for jj in range(FFN_AHEAD):
                for cp in chunk_copies(jj):
                    cp.start()
        x = x_ref[...]
        if n_pre:
            mix = None
            for y_ref, w_ref in zip(pre_y, pre_w):
                y = _dot(y_ref[...], w_ref[0])
                mix = y if mix is None else mix + y
            x = x + _gate(mod_ref, 1) * mix
        h = _modulated(x, mod_ref, g_ref, s).astype(BF16)
        acc = jnp.zeros(x.shape, F32)
        for j in range(n_chunks):
            up = slice(j * FFN_TF, (j + 1) * FFN_TF)
            gate = slice(FFN_DIM + j * FFN_TF, FFN_DIM + (j + 1) * FFN_TF)
            if stream_weights:
                if j + FFN_AHEAD < n_chunks:
                    for cp in chunk_copies(j + FFN_AHEAD):
                        cp.start()
                for cp in chunk_copies(j):
                    cp.wait()
                w13_ref[:, up] = stage13[j % (FFN_AHEAD + 1), 0].astype(BF16)
                w13_ref[:, gate] = stage13[j % (FFN_AHEAD + 1), 1].astype(BF16)
                w2_ref[up, :] = stage2[j % (FFN_AHEAD + 1)].astype(BF16)
            a = _dot(h, w13_ref[:, up])
            b = _dot(h, w13_ref[:, gate])
            u = (_silu(a) * b).astype(BF16)
            acc = acc + _dot(u, w2_ref[up, :])
        out = x + 0.5 * _gate(mod_ref, s) * acc
        if final:
            out = _rms(out, fg_ref[...])
        o_ref[...] = out

    first = pl.program_id(0) == 0
    pl.when(first)(functools.partial(body, True))
    pl.when(jnp.logical_not(first))(functools.partial(body, False))


def _ffn(x, mods, norm_g, w13, w2, layer, which, latent, seq, pre=None, final_g=None):
    tm = FFN_TM
    m = x.shape[0]
    assert m % tm == 0
    s = 0 if which == 0 else 2
    in_specs = [pl.BlockSpec((tm, D_MODEL), lambda i: (i, 0)),
                pl.BlockSpec((1, 1, N_MOD * D_MODEL), _mod_index(layer, latent, tm, seq)),
                pl.BlockSpec((1, 1, D_MODEL), lambda i: (layer * 3 + s, 0, 0)),
                pl.BlockSpec(memory_space=pl.ANY),
                pl.BlockSpec(memory_space=pl.ANY)]
    args = [x, mods, norm_g, w13, w2]
    n_pre = 0
    if pre is not None:
        ys, ws, w_layer = pre
        n_pre = len(ys)
        for y in ys:
            in_specs.append(pl.BlockSpec((tm, y.shape[1]), lambda i: (i, 0)))
        for w, rows, blk in ws:
            in_specs.append(_resident((1, rows, D_MODEL), lambda i, blk=blk: (w_layer, blk, 0)))
        args += list(ys) + [w for w, _, _ in ws]
    if final_g is not None:
        in_specs.append(pl.BlockSpec((1, D_MODEL), lambda i: (0, 0)))
        args.append(final_g.reshape(1, D_MODEL))
    return pl.pallas_call(
        functools.partial(_ffn_kernel, s=s, n_pre=n_pre, final=final_g is not None, layer=layer, which=which),
        grid=(m // tm,),
        in_specs=in_specs,
        out_specs=pl.BlockSpec((tm, D_MODEL), lambda i: (i, 0)),
        out_shape=jax.ShapeDtypeStruct((m, D_MODEL), F32),
        scratch_shapes=[pltpu.VMEM((D_MODEL, 2 * FFN_DIM), BF16),
                        pltpu.VMEM((FFN_DIM, D_MODEL), BF16),
                        pltpu.VMEM((FFN_AHEAD + 1, 2, D_MODEL, FFN_TF), F32),
                        pltpu.VMEM((FFN_AHEAD + 1, FFN_TF, D_MODEL), F32),
                        pltpu.SemaphoreType.DMA((FFN_AHEAD + 1, 3))],
        compiler_params=_cparams(1),
        name="ffn",
    )(*args)


_EV_QA = 0
_EV_KA = NA_WIDTH
_EV_VA = 2 * NA_WIDTH
_EV_CQ = 3 * NA_WIDTH
_EV_CKV = _EV_CQ + MLA_Q_RANK
_EV_COLS = _EV_CKV + MLA_KV_RANK + MLA_ROPE_DIM
MLA_QK_W = MLA_HEADS * LANES
LOG2E = 1.4426950408889634
NA_QSCALE = NA_HEAD_DIM ** -0.5 * LOG2E
MLA_QSCALE = (MLA_NOPE_DIM + MLA_ROPE_DIM) ** -0.5 * LOG2E


def _rope_swap_perm(width, half):
    j = np.arange(width)
    return np.where((j % (2 * half)) < half, j + half, j - half)


def _pack_even_weights(mix_w_in, mla_w_uq, mla_w_ukv):
    n_even = mix_w_in.shape[0]
    o = 3 * NA_WIDTH + MLA_Q_RANK + MLA_KV_RANK
    w_kr = mix_w_in[:, :, o:o + MLA_ROPE_DIM]
    w_krs = w_kr[:, :, _rope_swap_perm(MLA_ROPE_DIM, MLA_ROPE_DIM // 4)]
    pad = ((0, 0), (0, 0), (MLA_NOPE_DIM, LANES - MLA_NOPE_DIM - MLA_ROPE_DIM))
    w_kr2 = jnp.concatenate([jnp.pad(w_kr, pad), jnp.pad(w_krs, pad)], axis=-1)

    uq = mla_w_uq.reshape(n_even, MLA_Q_RANK, MLA_HEADS, MLA_NOPE_DIM + MLA_ROPE_DIM)
    uq_rope = uq[..., MLA_NOPE_DIM:]
    uq_rope_s = uq_rope[..., _rope_swap_perm(MLA_ROPE_DIM, MLA_ROPE_DIM // 4)]
    zpad = LANES - MLA_NOPE_DIM - MLA_ROPE_DIM
    w_uq = jnp.pad(uq, ((0, 0), (0, 0), (0, 0), (0, zpad))).reshape(n_even, MLA_Q_RANK, MLA_QK_W)
    w_uqs = jnp.pad(uq_rope_s, ((0, 0), (0, 0), (0, 0), (MLA_NOPE_DIM, zpad)))
    w_uqs = w_uqs.reshape(n_even, MLA_Q_RANK, MLA_QK_W)

    ukv = mla_w_ukv.reshape(n_even, MLA_KV_RANK, MLA_HEADS, MLA_NOPE_DIM + MLA_V_DIM)
    w_uk = jnp.pad(ukv[..., :MLA_NOPE_DIM], ((0, 0), (0, 0), (0, 0), (0, LANES - MLA_NOPE_DIM)))
    w_uk = w_uk.reshape(n_even, MLA_KV_RANK, MLA_QK_W)
    w_uv = ukv[..., MLA_NOPE_DIM:].reshape(n_even, MLA_KV_RANK, MLA_OUT)
    return (mix_w_in.astype(BF16), w_kr2.astype(BF16), w_uq.astype(BF16), w_uqs.astype(BF16),
            w_uk.astype(BF16), w_uv.astype(BF16))


def _rope_angles(seq, width, group):
    half = group // 2
    t = np.arange(seq)
    j = np.arange(width)
    pos = np.where((j // group)[None, :] % 2 == 0, (t // GRID_W)[:, None], (t % GRID_W)[:, None])
    inv = np.float32(ROPE_BASE) ** (-np.arange(half, dtype=np.float32) / np.float32(half))
    ang = pos.astype(np.float32) * inv[j % half][None, :]
    sign = np.where((j % group) < half, -1.0, 1.0).astype(np.float32)
    return ang, sign


def _mla_rope_tables(seq):
    ang, sign = _rope_angles(seq, MLA_ROPE_DIM, MLA_ROPE_DIM // 2)
    tail = LANES - MLA_NOPE_DIM - MLA_ROPE_DIM
    cos = np.concatenate([np.ones((seq, MLA_NOPE_DIM), np.float32), np.cos(ang),
                          np.ones((seq, tail), np.float32)], axis=1)
    sin = np.concatenate([np.zeros((seq, MLA_NOPE_DIM), np.float32), np.sin(ang) * sign[None, :],
                          np.zeros((seq, tail), np.float32)], axis=1)
    return jnp.asarray(cos), jnp.asarray(sin)


def _even_proj_kernel(*refs, latent, n_alias):
    (x_ref, mod_ref, g_ref, win_ref, wkr_ref, qn_ref, kvn_ref, wuq_ref, wuqs_ref, wuk_ref,
     wuv_ref) = refs[:11]
    if latent:
        cos_ref, sin_ref = refs[11:13]
        qa_ref, ka_ref, va_ref, qm_ref, km_ref, vm_ref = refs[13:]
    else:
        qa_ref, ka_ref, va_ref, qm_ref, km_ref, vm_ref, ckv_ref, kr_ref = refs[11 + n_alias:]

    def w_in(lo, width):
        return win_ref[0, :, lo:lo + width]

    h = _modulated(x_ref[...], mod_ref, g_ref, 1).astype(BF16)
    qa_ref[...] = (_dot(h, w_in(_EV_QA, NA_WIDTH)) * NA_QSCALE).astype(qa_ref.dtype)
    ka = _dot(h, w_in(_EV_KA, NA_WIDTH))
    va = _dot(h, w_in(_EV_VA, NA_WIDTH))
    if latent:
        ka_ref[...] = ka.astype(BF16)
        va_ref[...] = va.astype(BF16)
    else:
        ka_ref[:, 0] = ka.reshape(ka_ref.shape[0], ka_ref.shape[2], NA_WIDTH)
        va_ref[:, 0] = va.reshape(va_ref.shape[0], va_ref.shape[2], NA_WIDTH)

    cqn = _rms(_dot(h, w_in(_EV_CQ, MLA_Q_RANK)), qn_ref[0]).astype(BF16)
    ckvn = _rms(_dot(h, w_in(_EV_CKV, MLA_KV_RANK)), kvn_ref[0])
    kr = _dot(h, wkr_ref[0, :, :LANES])
    if latent:
        cos = cos_ref[...]
        sin = sin_ref[...]
        kr = kr * cos + _dot(h, wkr_ref[0, :, LANES:]) * sin
    else:
        ckv_ref[:, 0] = ckvn.reshape(ckv_ref.shape[0], ckv_ref.shape[2], MLA_KV_RANK)
        kr_ref[:, 0] = kr[:, MLA_NOPE_DIM:MLA_NOPE_DIM + MLA_ROPE_DIM].reshape(
            kr_ref.shape[0], kr_ref.shape[2], MLA_ROPE_DIM)
    ckvb = ckvn.astype(BF16)

    qm = _dot(cqn, wuq_ref[0])
    km = _dot(ckvb, wuk_ref[0])
    if latent:
        qms = _dot(cqn, wuqs_ref[0])
    for hd in range(MLA_HEADS):
        sl = slice(hd * LANES, (hd + 1) * LANES)
        q_h = qm[:, sl]
        if latent:
            q_h = q_h * cos + qms[:, sl] * sin
        qm_ref[:, sl] = (q_h * MLA_QSCALE).astype(BF16)
        km_ref[:, sl] = (km[:, sl] + kr).astype(BF16)
    vm_ref[...] = _dot(ckvb, wuv_ref[0]).astype(BF16)


def _even_proj(x, mods, norm_g, packed, q_norm, kv_norm, layer, latent, seq, n_even, rope=None, caches=None):
    w_in, w_kr2, w_uq, w_uqs, w_uk, w_uv = packed
    i_even = layer // 2
    m = x.shape[0]
    tm = FFN_TM
    assert m % tm == 0 and (latent or tm % seq == 0)
    bt = tm // seq if not latent else 0
    tok = lambda i: (i, 0)
    in_specs = [pl.BlockSpec((tm, D_MODEL), tok),
                pl.BlockSpec((1, 1, N_MOD * D_MODEL), _mod_index(layer, latent, tm, seq)),
                pl.BlockSpec((1, 1, D_MODEL), lambda i: (layer * 3 + 1, 0, 0)),
                _resident((1, D_MODEL, _EV_COLS), lambda i: (i_even, 0, 0)),
                _resident((1, D_MODEL, 2 * LANES), lambda i: (i_even, 0, 0)),
                pl.BlockSpec((1, 1, MLA_Q_RANK), lambda i: (i_even, 0, 0)),
                pl.BlockSpec((1, 1, MLA_KV_RANK), lambda i: (i_even, 0, 0)),
                _resident((1, MLA_Q_RANK, MLA_QK_W), lambda i: (i_even, 0, 0)),
                _resident((1, MLA_Q_RANK, MLA_QK_W), lambda i: (i_even, 0, 0)),
                _resident((1, MLA_KV_RANK, MLA_QK_W), lambda i: (i_even, 0, 0)),
                _resident((1, MLA_KV_RANK, MLA_OUT), lambda i: (i_even, 0, 0))]
    args = [x, mods, norm_g, w_in, w_kr2, q_norm, kv_norm, w_uq, w_uqs, w_uk, w_uv]
    bf = lambda w: jax.ShapeDtypeStruct((m, w), BF16)
    aliases = {}
    if latent:
        nt = seq // tm
        in_specs += [pl.BlockSpec((tm, LANES), lambda i: (i % nt, 0))] * 2
        args += list(rope)
        out_shape = [bf(NA_WIDTH), bf(NA_WIDTH), bf(NA_WIDTH), bf(MLA_QK_W), bf(MLA_QK_W), bf(MLA_OUT)]
        out_specs = [pl.BlockSpec((tm, s.shape[1]), tok) for s in out_shape]
    else:
        nb = m // seq
        cache = lambda w: jax.ShapeDtypeStruct((nb, n_even, seq, w), F32)
        cspec = lambda w: pl.BlockSpec((bt, 1, seq, w), lambda i: (i, i_even, 0, 0))
        out_shape = [bf(NA_WIDTH), cache(NA_WIDTH), cache(NA_WIDTH), bf(MLA_QK_W), bf(MLA_QK_W),
                     bf(MLA_OUT), cache(MLA_KV_RANK), cache(MLA_ROPE_DIM)]
        out_specs = [pl.BlockSpec((tm, NA_WIDTH), tok), cspec(NA_WIDTH), cspec(NA_WIDTH),
                     pl.BlockSpec((tm, MLA_QK_W), tok), pl.BlockSpec((tm, MLA_QK_W), tok),
                     pl.BlockSpec((tm, MLA_OUT), tok), cspec(MLA_KV_RANK), cspec(MLA_ROPE_DIM)]
        if caches is not None:
            for arr, out_idx in zip(caches, (1, 2, 6, 7)):
                aliases[len(args)] = out_idx
                in_specs.append(pl.BlockSpec(memory_space=pl.ANY))
                args.append(arr)
    return pl.pallas_call(
        functools.partial(_even_proj_kernel, latent=latent, n_alias=len(aliases)),
        grid=(m // tm,),
        in_specs=in_specs,
        out_specs=out_specs,
        out_shape=out_shape,
        input_output_aliases=aliases,
        compiler_params=_cparams(1),
        name="even_proj_latent" if latent else "even_proj_context",
    )(*args)


def _mla_cache_kernel(ckv_ref, kr_ref, wuk_ref, wuv_ref, km_ref, vm_ref):
    ckvb = ckv_ref[0, 0].astype(BF16)
    km = _dot(ckvb, wuk_ref[0])
    kr = kr_ref[...]
    for hd in range(MLA_HEADS):
        sl = slice(hd * LANES, (hd + 1) * LANES)
        km_ref[:, sl] = (km[:, sl] + kr).astype(BF16)
    vm_ref[...] = _dot(ckvb, wuv_ref[0]).astype(BF16)


def _mla_cache_keys(cache_ckv, kr_padded, w_uk, w_uv, i_even):
    nb, _, past, _ = cache_ckv.shape
    return pl.pallas_call(
        _mla_cache_kernel,
        grid=(nb,),
        in_specs=[pl.BlockSpec((1, 1, past, MLA_KV_RANK), lambda b: (b, i_even, 0, 0)),
                  pl.BlockSpec((past, LANES), lambda b: (b, 0)),
                  _resident((1, MLA_KV_RANK, MLA_QK_W), lambda b: (i_even, 0, 0)),
                  _resident((1, MLA_KV_RANK, MLA_OUT), lambda b: (i_even, 0, 0))],
        out_specs=[pl.BlockSpec((past, MLA_QK_W), lambda b: (b, 0)),
                   pl.BlockSpec((past, MLA_OUT), lambda b: (b, 0))],
        out_shape=[jax.ShapeDtypeStruct((nb * past, MLA_QK_W), BF16),
                   jax.ShapeDtypeStruct((nb * past, MLA_OUT), BF16)],
        compiler_params=_cparams(1),
        name="mla_cache_keys",
    )(cache_ckv, kr_padded, w_uk, w_uv)


def _half_masks():
    lane = lax.broadcasted_iota(jnp.int32, (1, LANES), 1)
    lo = lane < (LANES // 2)
    return lo, jnp.logical_not(lo)


def _scores_pass(q, key_blocks, bias_blocks, s_ref, kb):
    mx = None
    for j, (k_blk, bias) in enumerate(zip(key_blocks, bias_blocks)):
        sc = _dot_nt(q, k_blk())
        if bias is not None:
            sc = sc + bias()
        s_ref[:, j * kb:(j + 1) * kb] = sc
        for g in range(kb // LANES):
            part = sc[:, g * LANES:(g + 1) * LANES]
            mx = part if mx is None else jnp.maximum(mx, part)
    return jnp.max(mx, axis=-1, keepdims=True)


def _values_pass(s_ref, m, value_blocks, kb):
    ls = None
    acc = None
    for j, v_blk in enumerate(value_blocks):
        p = jnp.exp2(s_ref[:, j * kb:(j + 1) * kb] - m)
        for g in range(kb // LANES):
            part = p[:, g * LANES:(g + 1) * LANES]
            ls = part if ls is None else ls + part
        o = _dot(p.astype(BF16), v_blk())
        acc = o if acc is None else acc + o
    return acc / jnp.sum(ls, axis=-1, keepdims=True)


def _attention_chains(chains, s_scr, kb=None):
    kb = ATT_KB if kb is None else kb
    depth = s_scr.shape[0]
    outs = []
    maxes = {}
    for i in range(len(chains) + depth - 1):
        if i < len(chains):
            q, key_blocks, _, bias_blocks = chains[i]
            maxes[i] = _scores_pass(q(), key_blocks, bias_blocks, s_scr.at[i % depth], kb)
        j = i - (depth - 1)
        if j >= 0:
            outs.append(_values_pass(s_scr.at[j % depth], maxes.pop(j), chains[j][2], kb))
    return outs


def _rows(ref, start, size, lanes):
    return lambda: ref[start:start + size, lanes]


def _ctx_attn_kernel(qa_ref, ka_ref, va_ref, qm_ref, km_ref, vm_ref, o_ref, s_scr, kv_scr):
    lo, hi = _half_masks()
    zero = jnp.zeros((), BF16)
    nbt, seq = qa_ref.shape[0], qa_ref.shape[1]
    kb_sz = min(ATT_KB, seq)
    qs_sz = min(ATT_QS, seq)
    chains = []
    for bb in range(nbt):
        kv_scr[bb, 0] = ka_ref[bb, 0].astype(BF16)
        kv_scr[bb, 1] = va_ref[bb, 0].astype(BF16)
        for g in range((NA_WIDTH + MLA_OUT) // LANES):
            mla = g >= NA_WIDTH // LANES
            gl = g - NA_WIDTH // LANES if mla else g
            sl = slice(gl * LANES, (gl + 1) * LANES)
            for qs in range(seq // qs_sz):
                rows = slice(qs * qs_sz, (qs + 1) * qs_sz)
                for t, msk in enumerate((lo, hi)):
                    if mla:
                        hsl = slice((2 * gl + t) * LANES, (2 * gl + t + 1) * LANES)
                        q = lambda bb=bb, rows=rows, hsl=hsl: qm_ref[bb, rows, hsl]
                        kb = [_rows(km_ref.at[bb], j * kb_sz, kb_sz, hsl) for j in range(seq // kb_sz)]
                        vb = [_rows(vm_ref.at[bb], j * kb_sz, kb_sz, sl) for j in range(seq // kb_sz)]
                    else:
                        q = lambda bb=bb, rows=rows, sl=sl, msk=msk: jnp.where(msk, qa_ref[bb, rows, sl], zero)
                        kb = [_rows(kv_scr.at[bb, 0], j * kb_sz, kb_sz, sl) for j in range(seq // kb_sz)]
                        vb = [_rows(kv_scr.at[bb, 1], j * kb_sz, kb_sz, sl) for j in range(seq // kb_sz)]
                    chains.append((q, kb, vb, [None] * len(kb)))
    outs = _attention_chains(chains, s_scr, kb_sz)
    n = 0
    for bb in range(nbt):
        for g in range((NA_WIDTH + MLA_OUT) // LANES):
            for qs in range(seq // qs_sz):
                o_ref[bb, qs * qs_sz:(qs + 1) * qs_sz, g * LANES:(g + 1) * LANES] = (
                    jnp.where(lo, outs[n], outs[n + 1]).astype(BF16))
                n += 2


def _ctx_attention(qa, ka, va, qm, km, vm, seq, i_even):
    m = qa.shape[0]
    nb = m // seq
    nbt = CTX_BATCHES
    assert nb % nbt == 0
    tok = lambda b: (b, 0, 0)
    cache = lambda b: (b, i_even, 0, 0)
    r3 = lambda a: a.reshape(nb, seq, a.shape[1])
    out = pl.pallas_call(
        _ctx_attn_kernel,
        grid=(nb // nbt,),
        in_specs=[pl.BlockSpec((nbt, seq, NA_WIDTH), tok),
                  pl.BlockSpec((nbt, 1, seq, NA_WIDTH), cache),
                  pl.BlockSpec((nbt, 1, seq, NA_WIDTH), cache),
                  pl.BlockSpec((nbt, seq, MLA_QK_W), tok),
                  pl.BlockSpec((nbt, seq, MLA_QK_W), tok),
                  pl.BlockSpec((nbt, seq, MLA_OUT), tok)],
        out_specs=pl.BlockSpec((nbt, seq, NA_WIDTH + MLA_OUT), tok),
        out_shape=jax.ShapeDtypeStruct((nb, seq, NA_WIDTH + MLA_OUT), BF16),
        scratch_shapes=[pltpu.VMEM((2, min(ATT_QS, seq), seq), F32),
                        pltpu.VMEM((nbt, 2, seq, NA_WIDTH), BF16)],
        compiler_params=_cparams(1),
        name="context_attention",
    )(r3(qa), ka, va, r3(qm), r3(km), r3(vm))
    return out.reshape(m, NA_WIDTH + MLA_OUT)


def _lat_mla_kernel(q_ref, kc_ref, kl_ref, vc_ref, vl_ref, o_ref, s_scr):
    lo, _ = _half_masks()
    past, seq = kc_ref.shape[0], kl_ref.shape[0]
    all_lanes = slice(0, LANES)
    kbs = []
    for t in range(2):
        sl = slice(t * LANES, (t + 1) * LANES)
        kbs.append([_rows(kc_ref, j * ATT_KB, ATT_KB, sl) for j in range(past // ATT_KB)]
                   + [_rows(kl_ref, j * ATT_KB, ATT_KB, sl) for j in range(seq // ATT_KB)])
    vb = ([_rows(vc_ref, j * ATT_KB, ATT_KB, all_lanes) for j in range(past // ATT_KB)]
          + [_rows(vl_ref, j * ATT_KB, ATT_KB, all_lanes) for j in range(seq // ATT_KB)])
    tq = min(ATT_TQ, seq)
    n_qs = tq // ATT_QS

    def tile(i, carry):
        base = pl.multiple_of(i * tq, tq)
        chains = []
        for qs in range(n_qs):
            rows = pl.ds(base + qs * ATT_QS, ATT_QS)
            for t in range(2):
                sl = slice(t * LANES, (t + 1) * LANES)
                chains.append((lambda rows=rows, sl=sl: q_ref[rows, sl], kbs[t], vb, [None] * len(vb)))
        outs = _attention_chains(chains, s_scr)
        for qs in range(n_qs):
            o_ref[pl.ds(base + qs * ATT_QS, ATT_QS), :] = (
                jnp.where(lo, outs[2 * qs], outs[2 * qs + 1]).astype(BF16))
        return carry

    lax.fori_loop(0, seq // tq, tile, 0)


def _latent_mla_attention(qm, km_ctx, km_lat, vm_ctx, vm_lat, seq, past):
    m = qm.shape[0]
    nb = m // seq
    pair = 2 * LANES
    return pl.pallas_call(
        _lat_mla_kernel,
        grid=(nb, MLA_HEADS // 2),
        in_specs=[pl.BlockSpec((seq, pair), lambda b, g: (b, g)),
                  pl.BlockSpec((past, pair), lambda b, g: (b, g)),
                  pl.BlockSpec((seq, pair), lambda b, g: (b, g)),
                  pl.BlockSpec((past, LANES), lambda b, g: (b, g)),
                  pl.BlockSpec((seq, LANES), lambda b, g: (b, g))],
        out_specs=pl.BlockSpec((seq, LANES), lambda b, g: (b, g)),
        out_shape=jax.ShapeDtypeStruct((m, MLA_OUT), BF16),
        scratch_shapes=[pltpu.VMEM((2, ATT_QS, past + seq), F32)],
        compiler_params=_cparams(2),
        name="latent_mla_attention",
    )(qm, km_ctx, km_lat, vm_ctx, vm_lat)


def _na_window_start(first_row, rows, clip=np.clip):
    r0 = clip(first_row - NA_ROWS // 2, 0, rows - NA_ROWS)
    return clip(r0 - (NA_WROWS - NA_ROWS) // 2, 0, rows - NA_WROWS)


def _na_bias_kernel(rpb_ref, o_ref, *, rows):
    hd = pl.program_id(0)
    kr = min(NA_ROWS, rows)
    n_dc = 2 * NA_COLS - 1
    band = ATT_QS // GRID_W
    c = lax.broadcasted_iota(jnp.int32, (GRID_W, LANES), 0)
    lane = lax.broadcasted_iota(jnp.int32, (GRID_W, LANES), 1)
    n = lane % GRID_W
    c0 = jnp.clip(c - NA_COLS // 2, 0, GRID_W - NA_COLS)
    in_win = (n >= c0) & (n < c0 + NA_COLS)
    dc = n - c + NA_COLS - 1
    left = lane < GRID_W
    neg = jnp.full((GRID_W, LANES), NEG_BIG, F32)
    col_tables = []
    for dr in range(2 * NA_ROWS - 1):
        t = neg
        for j in range(n_dc):
            t = jnp.where(dc == j, rpb_ref[hd * (2 * NA_ROWS - 1) * n_dc + dr * n_dc + j] * LOG2E, t)
        col_tables.append(jnp.where(in_win, t, neg))
    for rb in range(rows // NA_QROWS):
        for rl in range(NA_QROWS):
            r = rb * NA_QROWS + rl
            r0 = min(max(r - kr // 2, 0), rows - kr)
            start = int(_na_window_start(r - rl % band, rows))
            assert start <= r0 and r0 + kr <= start + NA_WROWS
            for pr in range(NA_WROWS // 2):
                halves = []
                for key_row in (start + 2 * pr, start + 2 * pr + 1):
                    ok = r0 <= key_row < r0 + kr
                    halves.append(col_tables[key_row - r + NA_ROWS - 1] if ok else neg)
                o_ref[rb, 0, rl * GRID_W:(rl + 1) * GRID_W, pr * LANES:(pr + 1) * LANES] = (
                    jnp.where(left, halves[0], halves[1]))


def _na_bias(rpb, rows):
    n_rb = rows // NA_QROWS
    flat = rpb.reshape(-1)
    return pl.pallas_call(
        functools.partial(_na_bias_kernel, rows=rows),
        grid=(NA_HEADS,),
        in_specs=[pl.BlockSpec(memory_space=pltpu.SMEM)],
        out_specs=pl.BlockSpec((n_rb, 1, NA_QROWS * GRID_W, NA_WROWS * GRID_W), lambda h: (0, h, 0, 0)),
        out_shape=jax.ShapeDtypeStruct((n_rb, NA_HEADS, NA_QROWS * GRID_W, NA_WROWS * GRID_W), F32),
        compiler_params=_cparams(1),
        name="na_bias",
    )(flat)


def _lat_na_kernel(q_ref, k_ref, v_ref, kc_ref, vc_ref, bias_ref, o_ref, s_scr, kv_scr, *, rows):
    lo, hi = _half_masks()
    zero = jnp.zeros((), BF16)
    rb = pl.program_id(0)
    nk = NA_WROWS * GRID_W
    past = kc_ref.shape[2]
    all_lanes = slice(0, LANES)
    n_qs = q_ref.shape[1] // ATT_QS
    starts = [pl.multiple_of(_na_window_start(rb * NA_QROWS + qs * (ATT_QS // GRID_W), rows, jnp.clip) * GRID_W,
                             GRID_W) for qs in range(n_qs)]

    def batch(b, carry):
        kv_scr[0] = kc_ref[b, 0].astype(BF16)
        kv_scr[1] = vc_ref[b, 0].astype(BF16)

        def win(ref, start, j):
            return lambda: ref[b, pl.ds(start + j * ATT_KB, ATT_KB), :]

        ctx_k = [_rows(kv_scr.at[0], j * ATT_KB, ATT_KB, all_lanes) for j in range(past // ATT_KB)]
        ctx_v = [_rows(kv_scr.at[1], j * ATT_KB, ATT_KB, all_lanes) for j in range(past // ATT_KB)]
        chains = []
        for qs in range(n_qs):
            qrows = slice(qs * ATT_QS, (qs + 1) * ATT_QS)
            kb = [win(k_ref, starts[qs], j) for j in range(nk // ATT_KB)] + ctx_k
            vb = [win(v_ref, starts[qs], j) for j in range(nk // ATT_KB)] + ctx_v
            for t, msk in enumerate((lo, hi)):
                bias = ([(lambda t=t, j=j, qrows=qrows: bias_ref[0, t, qrows, j * ATT_KB:(j + 1) * ATT_KB])
                         for j in range(nk // ATT_KB)] + [None] * (past // ATT_KB))
                chains.append((lambda qrows=qrows, msk=msk: jnp.where(msk, q_ref[b, qrows, :], zero),
                               kb, vb, bias))
        outs = _attention_chains(chains, s_scr)
        for qs in range(n_qs):
            o_ref[b, qs * ATT_QS:(qs + 1) * ATT_QS, :] = (
                jnp.where(lo, outs[2 * qs], outs[2 * qs + 1]).astype(BF16))
        return carry

    lax.fori_loop(0, q_ref.shape[0], batch, 0, unroll=True)


def _latent_na_attention(qa, ka, va, cache_k, cache_v, bias, i_even, seq):
    m = qa.shape[0]
    nb = m // seq
    rows = seq // GRID_W
    n_rb = rows // NA_QROWS
    tq = NA_QROWS * GRID_W
    past = cache_k.shape[2]
    per_batch = lambda a: a.reshape(nb, seq, NA_WIDTH)
    out = pl.pallas_call(
        functools.partial(_lat_na_kernel, rows=rows),
        grid=(n_rb, NA_HEADS // 2),
        in_specs=[pl.BlockSpec((nb, tq, LANES), lambda r, g: (0, r, g)),
                  pl.BlockSpec((nb, seq, LANES), lambda r, g: (0, 0, g)),
                  pl.BlockSpec((nb, seq, LANES), lambda r, g: (0, 0, g)),
                  pl.BlockSpec((nb, 1, past, LANES), lambda r, g: (0, i_even, 0, g)),
                  pl.BlockSpec((nb, 1, past, LANES), lambda r, g: (0, i_even, 0, g)),
                  pl.BlockSpec((1, 2, tq, NA_WROWS * GRID_W), lambda r, g: (r, g, 0, 0))],
        out_specs=pl.BlockSpec((nb, tq, LANES), lambda r, g: (0, r, g)),
        out_shape=jax.ShapeDtypeStruct((nb, seq, NA_WIDTH), BF16),
        scratch_shapes=[pltpu.VMEM((2, ATT_QS, NA_WROWS * GRID_W + past), F32),
                        pltpu.VMEM((2, past, LANES), BF16)],
        compiler_params=_cparams(2),
        name="latent_na_attention",
    )(per_batch(qa), per_batch(ka), per_batch(va), cache_k, cache_v, bias)
    return out.reshape(m, NA_WIDTH)


_RET_NK = RET_HEADS * RET_DK
_RET_NV = RET_HEADS * RET_DV


def _ret_proj_kernel(x_ref, mod_ref, g_ref, w_ref, cos_ref, sin_ref, q_ref, k_ref, v_ref, gate_ref):
    h = _modulated(x_ref[...], mod_ref, g_ref, 1).astype(BF16)

    def rope(y):
        parts = [pltpu.roll(y[:, g * LANES:(g + 1) * LANES], LANES // 2, 1) for g in range(RET_DK // LANES)]
        return y * cos_ref[...] + jnp.concatenate(parts, axis=1) * sin_ref[...]

    for hd in range(RET_HEADS):
        sl = slice(hd * RET_DK, (hd + 1) * RET_DK)
        q_ref[:, sl] = rope(_dot(h, w_ref[0, :, sl])).astype(BF16)
        k = _dot(h, w_ref[0, :, _RET_NK + hd * RET_DK:_RET_NK + (hd + 1) * RET_DK]) * (RET_DK ** -0.5)
        k_ref[:, sl] = rope(k)
    step = 512
    for j in range(_RET_NV // step):
        o = 2 * _RET_NK + j * step
        v_ref[:, j * step:(j + 1) * step] = _dot(h, w_ref[0, :, o:o + step]).astype(BF16)
    for j in range(2 * _RET_NV // step):
        o = 2 * _RET_NK + _RET_NV + j * step
        gate_ref[:, j * step:(j + 1) * step] = _dot(h, w_ref[0, :, o:o + step])


def _ret_proj_latent(x, mods, norm_g, w_in, layer, seq, rope):
    m = x.shape[0]
    tm = RET_PROJ_TM
    assert m % tm == 0 and seq % tm == 0
    i_odd = layer // 2
    tok = lambda i: (i, 0)
    n_in = 2 * _RET_NK + 3 * _RET_NV
    nt = seq // tm
    return pl.pallas_call(
        _ret_proj_kernel,
        grid=(m // tm,),
        in_specs=[pl.BlockSpec((tm, D_MODEL), tok),
                  pl.BlockSpec((1, 1, N_MOD * D_MODEL), _mod_index(layer, True, tm, seq)),
                  pl.BlockSpec((1, 1, D_MODEL), lambda i: (layer * 3 + 1, 0, 0)),
                  _resident((1, D_MODEL, n_in), lambda i: (i_odd, 0, 0)),
                  pl.BlockSpec((tm, RET_DK), lambda i: (i % nt, 0)),
                  pl.BlockSpec((tm, RET_DK), lambda i: (i % nt, 0))],
        out_specs=[pl.BlockSpec((tm, _RET_NK), tok),
                   pl.BlockSpec((tm, _RET_NK), tok),
                   pl.BlockSpec((tm, _RET_NV), tok),
                   pl.BlockSpec((tm, 2 * _RET_NV), tok)],
        out_shape=[jax.ShapeDtypeStruct((m, _RET_NK), BF16),
                   jax.ShapeDtypeStruct((m, _RET_NK), F32),
                   jax.ShapeDtypeStruct((m, _RET_NV), BF16),
                   jax.ShapeDtypeStruct((m, 2 * _RET_NV), F32)],
        compiler_params=_cparams(1),
        name="ret_proj_latent",
    )(x, mods, norm_g, w_in, *rope)


def _ret_rope_tables(seq):
    ang, sign = _rope_angles(seq, RET_DK, RET_DK // 2)
    return jnp.asarray(np.cos(ang)), jnp.asarray(np.sin(ang) * sign[None, :])


def _gated_head_norm(g, o):
    mu = jnp.mean(o, axis=-1, keepdims=True)
    d = o - mu
    half_rs = 0.5 * lax.rsqrt(jnp.mean(d * d, axis=-1, keepdims=True) + EPS)
    u = g * d
    return (u + u * jnp.tanh(0.5 * g)) * half_rs


def _ret_decays(lg, dirn, c):
    ii = lax.broadcasted_iota(jnp.int32, (c, c), 0)
    jj = lax.broadcasted_iota(jnp.int32, (c, c), 1)
    row = lax.broadcasted_iota(jnp.int32, (c, 1), 0).astype(F32)
    diff = (ii - jj if dirn == 0 else jj - ii).astype(F32)
    intra = jnp.where(diff >= 0, jnp.exp(lg * jnp.maximum(diff, 0.0)), 0.0)
    if dirn == 0:
        q_dec = jnp.exp(lg * (row + 1.0))
        k_dec = jnp.exp(lg * (c - 1.0 - row))
    else:
        q_dec = jnp.exp(lg * (c - row))
        k_dec = jnp.exp(lg * row)
    return intra, q_dec, k_dec, jnp.exp(lg * float(c))


def _ret_ctx_kernel(*refs, n_alias):
    ld_ref, x_ref, mod_ref, g_ref, w_ref = refs[:5]
    y_ref, st_ref = refs[5 + n_alias:]
    c = x_ref.shape[0]
    h = _modulated(x_ref[...], mod_ref, g_ref, 1).astype(BF16)
    for hd in range(RET_HEADS):
        ksl = slice(hd * RET_DK, (hd + 1) * RET_DK)
        qb = _dot(h, w_ref[0, :, ksl]).astype(BF16)
        k = _dot(h, w_ref[0, :, _RET_NK + hd * RET_DK:_RET_NK + (hd + 1) * RET_DK]) * (RET_DK ** -0.5)
        vo = 2 * _RET_NK + hd * RET_DV
        v = _dot(h, w_ref[0, :, vo:vo + RET_DV]).astype(BF16)
        qk = _dot_nt(qb, k.astype(BF16))
        y = None
        for dirn in range(2):
            intra, _, k_dec, _ = _ret_decays(-jnp.abs(ld_ref[dirn, hd]), dirn, c)
            o = _dot((qk * intra).astype(BF16), v)
            st_ref[0, 0, dirn, hd] = _dot((k * k_dec).T.astype(BF16), v)
            go = 2 * _RET_NK + (1 + dirn) * _RET_NV + hd * RET_DV
            yd = _gated_head_norm(_dot(h, w_ref[0, :, go:go + RET_DV]), o)
            y = yd if y is None else y + yd
        y_ref[:, hd * RET_DV:(hd + 1) * RET_DV] = y.astype(BF16)


def _retention_context(x, mods, norm_g, w_in, log_decay, layer, seq, n_odd, prev_states=None):
    assert seq == RET_CHUNK
    m = x.shape[0]
    nb = m // seq
    i_odd = layer // 2
    n_in = 2 * _RET_NK + 3 * _RET_NV
    in_specs = [pl.BlockSpec(memory_space=pltpu.SMEM),
                pl.BlockSpec((seq, D_MODEL), lambda b: (b, 0)),
                pl.BlockSpec((1, 1, N_MOD * D_MODEL), _mod_index(layer, False, seq, seq)),
                pl.BlockSpec((1, 1, D_MODEL), lambda b: (layer * 3 + 1, 0, 0)),
                _resident((1, D_MODEL, n_in), lambda b: (i_odd, 0, 0))]
    args = [log_decay[i_odd], x, mods, norm_g, w_in]
    aliases = {}
    if prev_states is not None:
        aliases[len(args)] = 1
        in_specs.append(pl.BlockSpec(memory_space=pl.ANY))
        args.append(prev_states)
    return pl.pallas_call(
        functools.partial(_ret_ctx_kernel, n_alias=len(aliases)),
        grid=(nb,),
        in_specs=in_specs,
        out_specs=[pl.BlockSpec((seq, _RET_NV), lambda b: (b, 0)),
                   pl.BlockSpec((1, 1, 2, RET_HEADS, RET_DK, RET_DV), lambda b: (b, i_odd, 0, 0, 0, 0))],
        out_shape=[jax.ShapeDtypeStruct((m, _RET_NV), BF16),
                   jax.ShapeDtypeStruct((nb, n_odd, 2, RET_HEADS, RET_DK, RET_DV), F32)],
        input_output_aliases=aliases,
        compiler_params=_cparams(1),
        name="retention_context",
    )(*args)


def _ret_lat_kernel(ld_ref, q_ref, k_ref, v_ref, gf_ref, gb_ref, s0_ref, y_ref, s_scr, o_scr):
    hd = pl.program_id(1)
    c = RET_CHUNK
    n_chunks = q_ref.shape[0] // c
    decays = [_ret_decays(-jnp.abs(ld_ref[dirn, hd]), dirn, c) for dirn in range(2)]
    gates = (gf_ref, gb_ref)
    for dirn in range(2):
        s_scr[dirn] = s0_ref[0, 0, dirn, 0]

    def scan(first_visit):
        def body(t, carry):
            for dirn in range(2):
                intra, q_dec, k_dec, c_dec = decays[dirn]
                ci = t if dirn == 0 else n_chunks - 1 - t
                rows = pl.ds(pl.multiple_of(ci * c, c), c)
                qb = q_ref[rows, :]
                k = k_ref[rows, :]
                v = v_ref[rows, :]
                a = _dot_nt(qb, k.astype(BF16)) * intra
                s_prev = s_scr[dirn]
                o = _dot(a.astype(BF16), v) + _dot(qb, s_prev.astype(BF16)) * q_dec
                s_scr[dirn] = s_prev * c_dec + _dot((k * k_dec).T.astype(BF16), v)
                y = _gated_head_norm(gates[dirn][rows, :], o)
                if first_visit:
                    o_scr[rows, :] = y
                else:
                    y_ref[rows, :] = (o_scr[rows, :] + y).astype(BF16)
            return carry
        return body

    lax.fori_loop(0, n_chunks // 2, scan(True), 0, unroll=True)
    lax.fori_loop(n_chunks // 2, n_chunks, scan(False), 0, unroll=True)


def _retention_latent(q, k, v, gates, log_decay, i_odd, seq, state):
    m = q.shape[0]
    nb = m // seq
    assert (seq // RET_CHUNK) % 2 == 0
    return pl.pallas_call(
        _ret_lat_kernel,
        grid=(nb, RET_HEADS),
        in_specs=[pl.BlockSpec(memory_space=pltpu.SMEM),
                  pl.BlockSpec((seq, RET_DK), lambda b, h: (b, h)),
                  pl.BlockSpec((seq, RET_DK), lambda b, h: (b, h)),
                  pl.BlockSpec((seq, RET_DV), lambda b, h: (b, h)),
                  pl.BlockSpec((seq, RET_DV), lambda b, h: (b, h)),
                  pl.BlockSpec((seq, RET_DV), lambda b, h: (b, RET_HEADS + h)),
                  pl.BlockSpec((1, 1, 2, 1, RET_DK, RET_DV), lambda b, h: (b, i_odd, 0, h, 0, 0))],
        out_specs=pl.BlockSpec((seq, RET_DV), lambda b, h: (b, h)),
        out_shape=jax.ShapeDtypeStruct((m, _RET_NV), BF16),
        scratch_shapes=[pltpu.VMEM((2, RET_DK, RET_DV), F32), pltpu.VMEM((seq, RET_DV), F32)],
        compiler_params=_cparams(2),
        name="retention_latent",
    )(log_decay[i_odd], q, k, v, gates, gates, state)


def kernel(x_prompt, x_sample, c, cache_na_k, cache_na_v, cache_mla_ckv, cache_mla_krope, state_ret,
           c_ctx, norm_g, ada_w, ada_b, ffn_w13, ffn_w2, mix_w_in, mla_q_norm, mla_kv_norm, mla_w_uq,
           mla_w_ukv, na_rpb, mix_w_out, ret_w_in, ret_log_decay, ret_w_out, final_norm_g):
    batch, seq, _ = x_prompt.shape
    dec_batch, dec_seq, _ = x_sample.shape
    past = cache_na_k.shape[2]
    n_even = mix_w_in.shape[0]
    assert dec_batch + 1 <= MOD_ROWS and seq == RET_CHUNK and dec_seq % RET_CHUNK == 0
    assert (dec_seq // GRID_W) % NA_QROWS == 0 and dec_seq // GRID_W >= NA_WROWS

    w_out_even = mix_w_out.astype(BF16)
    w_in_ret = ret_w_in.astype(BF16)
    w_out_ret = ret_w_out.astype(BF16)
    packed = _pack_even_weights(mix_w_in, mla_w_uq, mla_w_ukv)
    q_norm = mla_q_norm.reshape(n_even, 1, MLA_Q_RANK)
    kv_norm = mla_kv_norm.reshape(n_even, 1, MLA_KV_RANK)
    norm_g3 = norm_g.reshape(DEPTH * 3, 1, D_MODEL)
    mla_rope = _mla_rope_tables(dec_seq)
    ret_rope = _ret_rope_tables(dec_seq)
    cache_k = cache_na_k.reshape(dec_batch, n_even, past, NA_WIDTH)
    cache_v = cache_na_v.reshape(dec_batch, n_even, past, NA_WIDTH)

    cvec = jnp.concatenate([c_ctx[None, :], c, jnp.zeros((MOD_ROWS - 1 - dec_batch, D_MODEL), F32)], axis=0)
    mods = _ada_all(cvec, ada_w, ada_b).reshape(DEPTH * MOD_ROWS, 1, N_MOD * D_MODEL)

    xp = x_prompt.reshape(batch * seq, D_MODEL)
    xs = x_sample.reshape(dec_batch * dec_seq, D_MODEL)
    n_odd = ret_w_in.shape[0]
    caches = None
    states = None
    for layer in range(DEPTH):
        i = layer // 2
        xp = _ffn(xp, mods, norm_g3, ffn_w13, ffn_w2, layer, 0, False, seq)
        xs = _ffn(xs, mods, norm_g3, ffn_w13, ffn_w2, layer, 0, True, dec_seq)
        if layer % 2 == 0:
            qa, ka, va, qm, km, vm, ckv, kr = _even_proj(
                xp, mods, norm_g3, packed, q_norm, kv_norm, layer, False, seq, n_even, caches=caches)
            caches = (ka, va, ckv, kr)
            op = _ctx_attention(qa, ka, va, qm, km, vm, seq, i)
            pre_p = ([op], [(w_out_even, NA_WIDTH + MLA_OUT, 0)], i)

            qa, ka, va, qm, km, vm = _even_proj(
                xs, mods, norm_g3, packed, q_norm, kv_norm, layer, True, dec_seq, n_even, rope=mla_rope)
            kr_pad = jnp.pad(cache_mla_krope[:, i].reshape(dec_batch * past, MLA_ROPE_DIM),
                             ((0, 0), (MLA_NOPE_DIM, LANES - MLA_NOPE_DIM - MLA_ROPE_DIM)))
            km_ctx, vm_ctx = _mla_cache_keys(cache_mla_ckv, kr_pad, packed[4], packed[5], i)
            bias = _na_bias(na_rpb[i], dec_seq // GRID_W)
            oa = _latent_na_attention(qa, ka, va, cache_k, cache_v, bias, i, dec_seq)
            ob = _latent_mla_attention(qm, km_ctx, km, vm_ctx, vm, dec_seq, past)
            pre_s = ([oa, ob], [(w_out_even, NA_WIDTH, 0), (w_out_even, MLA_OUT, NA_WIDTH // MLA_OUT)], i)
        else:
            yp, states = _retention_context(xp, mods, norm_g3, w_in_ret, ret_log_decay, layer, seq, n_odd,
                                            prev_states=states)
            pre_p = ([yp], [(w_out_ret, _RET_NV, 0)], i)

            q, k, v, gates = _ret_proj_latent(xs, mods, norm_g3, w_in_ret, layer, dec_seq, ret_rope)
            ys = _retention_latent(q, k, v, gates, ret_log_decay, i, dec_seq, state_ret)
            pre_s = ([ys], [(w_out_ret, _RET_NV, 0)], i)
        last = final_norm_g if layer == DEPTH - 1 else None
        xp = _ffn(xp, mods, norm_g3, ffn_w13, ffn_w2, layer, 1, False, seq, pre=pre_p, final_g=last)
        xs = _ffn(xs, mods, norm_g3, ffn_w13, ffn_w2, layer, 1, True, dec_seq, pre=pre_s, final_g=last)

    y_prompt = xp.reshape(batch, seq, D_MODEL)
    y_sample = xs.reshape(dec_batch, dec_seq, D_MODEL)
    ka, va, ckv, kr = caches
    new_na_k = ka.reshape(batch, n_even, seq, NA_HEADS, NA_HEAD_DIM)
    new_na_v = va.reshape(batch, n_even, seq, NA_HEADS, NA_HEAD_DIM)
    return (y_prompt, y_sample, new_na_k, new_na_v, ckv, kr, states)
```

```python
import functools

import jax
import jax.numpy as jnp
import numpy as np
from jax import lax
from jax.experimental import pallas as pl
from jax.experimental.pallas import tpu as pltpu

F32 = jnp.float32
BF16 = jnp.bfloat16

D_MODEL = 1024
DEPTH = 4
GRID_W = 64
NA_HEADS = 8
NA_HEAD_DIM = 64
NA_ROWS = 8
NA_COLS = 16
MLA_HEADS = 8
MLA_Q_RANK = 384
MLA_KV_RANK = 256
MLA_NOPE_DIM = 64
MLA_ROPE_DIM = 32
MLA_V_DIM = 64
RET_HEADS = 4
RET_DK = D_MODEL // RET_HEADS
RET_DV = 2 * D_MODEL // RET_HEADS
FFN_DIM = ((8 * D_MODEL // 3 + 127) // 128) * 128
N_MOD = 9
ROPE_BASE = 10000.0
EPS = 1e-6
NA_WIDTH = NA_HEADS * NA_HEAD_DIM
MLA_OUT = MLA_HEADS * MLA_V_DIM

LANES = 128
MOD_ROWS = 8
NEG_BIG = -1e30
VMEM_LIMIT = 56 * 1024 * 1024

ADA_TK = 256
FFN_TM = 512
EVEN_PROJ_TM = 1024
RET_PROJ_TM = 512
FFN_TF = 256
FFN_AHEAD = 1
RET_CHUNK = 256
NA_QROWS = 8
NA_WROWS = 12
ATT_QS = 128
ATT_KB = 256
CTX_BATCHES = 4
ATT_TQ = 2048


def _cparams(n_axes):
    return pltpu.CompilerParams(dimension_semantics=("arbitrary",) * n_axes,
                                vmem_limit_bytes=VMEM_LIMIT)


def _resident(block, index_map):
    return pl.BlockSpec(block, index_map, pipeline_mode=pl.Buffered(1))


def _dot(a, b):
    return jnp.dot(a, b, preferred_element_type=F32)


def _dot_nt(a, b):
    return lax.dot_general(a, b, (((1,), (1,)), ((), ())), preferred_element_type=F32)


def _silu(x):
    return x * (0.5 * jnp.tanh(0.5 * x) + 0.5)


def _rms(x, g):
    return x * lax.rsqrt(jnp.mean(x * x, axis=-1, keepdims=True) + EPS) * g


def _modulated(x, mod_ref, g_ref, s):
    shift = mod_ref[0, :, 3 * s * D_MODEL:(3 * s + 1) * D_MODEL]
    scale = mod_ref[0, :, (3 * s + 1) * D_MODEL:(3 * s + 2) * D_MODEL]
    return _rms(x, g_ref[0]) * (1.0 + scale) + shift


def _gate(mod_ref, s):
    return mod_ref[0, :, (3 * s + 2) * D_MODEL:(3 * s + 3) * D_MODEL]


def _mod_index(layer, latent, tile_rows, seq):
    if not latent:
        return lambda i: (layer * MOD_ROWS, 0, 0)
    return lambda i: (layer * MOD_ROWS + 1 + (i * tile_rows) // seq, 0, 0)


def _ada_kernel(c_ref, w_ref, b_ref, o_ref):
    @pl.when(pl.program_id(1) == 0)
    def _():
        o_ref[0] = jnp.broadcast_to(b_ref[0], o_ref.shape[1:])

    s = _silu(c_ref[...]).astype(BF16)
    o_ref[0] += _dot(s, w_ref[0].astype(BF16))


def _ada_all(cvec, ada_w, ada_b):
    tk = ADA_TK
    n = N_MOD * D_MODEL
    return pl.pallas_call(
        _ada_kernel,
        grid=(DEPTH, D_MODEL // tk),
        in_specs=[pl.BlockSpec((MOD_ROWS, tk), lambda l, k: (0, k)),
                  pl.BlockSpec((1, tk, n), lambda l, k: (l, k, 0)),
                  pl.BlockSpec((1, 1, n), lambda l, k: (l, 0, 0))],
        out_specs=pl.BlockSpec((1, MOD_ROWS, n), lambda l, k: (l, 0, 0)),
        out_shape=jax.ShapeDtypeStruct((DEPTH, MOD_ROWS, n), F32),
        compiler_params=_cparams(2),
        name="adaln",
    )(cvec, ada_w, ada_b.reshape(DEPTH, 1, n))


def _ffn_kernel(*refs, s, n_pre, final, layer, which):
    x_ref, mod_ref, g_ref, w13_hbm, w2_hbm = refs[:5]
    pre_y = refs[5:5 + n_pre]
    pre_w = refs[5 + n_pre:5 + 2 * n_pre]
    fg_ref = refs[5 + 2 * n_pre] if final else None
    o_ref, w13_ref, w2_ref, stage13, stage2, sem = refs[-6:]
    n_chunks = FFN_DIM // FFN_TF

    def chunk_copies(j):
        slot = j % (FFN_AHEAD + 1)
        cols = pl.ds(j * FFN_TF, FFN_TF)
        gate_cols = pl.ds(FFN_DIM + j * FFN_TF, FFN_TF)
        return (pltpu.make_async_copy(w13_hbm.at[layer, which, :, cols], stage13.at[slot, 0], sem.at[slot, 0]),
                pltpu.make_async_copy(w13_hbm.at[layer, which, :, gate_cols], stage13.at[slot, 1], sem.at[slot, 1]),
                pltpu.make_async_copy(w2_hbm.at[layer, which, cols, :], stage2.at[slot], sem.at[slot, 2]))

    def body(stream_weights):
        if stream_weights:
            for jj in range(FFN_AHEAD):
                for cp in chunk_copies(jj):
                    cp.start()
        x = x_ref[...]
        if n_pre:
            mix = None
            for y_ref, w_ref in zip(pre_y, pre_w):
                y = _dot(y_ref[...], w_ref[0])
                mix = y if mix is None else mix + y
            x = x + _gate(mod_ref, 1) * mix
        h = _modulated(x, mod_ref, g_ref, s).astype(BF16)
        acc = jnp.zeros(x.shape, F32)
        for j in range(n_chunks):
            up = slice(j * FFN_TF, (j + 1) * FFN_TF)
            gate = slice(FFN_DIM + j * FFN_TF, FFN_DIM + (j + 1) * FFN_TF)
            if stream_weights:
                if j + FFN_AHEAD < n_chunks:
                    for cp in chunk_copies(j + FFN_AHEAD):
                        cp.start()
                for cp in chunk_copies(j):
                    cp.wait()
                w13_ref[:, up] = stage13[j % (FFN_AHEAD + 1), 0].astype(BF16)
                w13_ref[:, gate] = stage13[j % (FFN_AHEAD + 1), 1].astype(BF16)
                w2_ref[up, :] = stage2[j % (FFN_AHEAD + 1)].astype(BF16)
            a = _dot(h, w13_ref[:, up])
            b = _dot(h, w13_ref[:, gate])
            u = (_silu(a) * b).astype(BF16)
            acc = acc + _dot(u, w2_ref[up, :])
        out = x + 0.5 * _gate(mod_ref, s) * acc
        if final:
            out = _rms(out, fg_ref[...])
        o_ref[...] = out

    first = pl.program_id(0) == 0
    pl.when(first)(functools.partial(body, True))
    pl.when(jnp.logical_not(first))(functools.partial(body, False))


def _ffn(x, mods, norm_g, w13, w2, layer, which, latent, seq, pre=None, final_g=None):
    tm = FFN_TM
    m = x.shape[0]
    assert m % tm == 0
    s = 0 if which == 0 else 2
    in_specs = [pl.BlockSpec((tm, D_MODEL), lambda i: (i, 0)),
                pl.BlockSpec((1, 1, N_MOD * D_MODEL), _mod_index(layer, latent, tm, seq)),
                pl.BlockSpec((1, 1, D_MODEL), lambda i: (layer * 3 + s, 0, 0)),
                pl.BlockSpec(memory_space=pl.ANY),
                pl.BlockSpec(memory_space=pl.ANY)]
    args = [x, mods, norm_g, w13, w2]
    n_pre = 0
    if pre is not None:
        ys, ws, w_layer = pre
        n_pre = len(ys)
        for y in ys:
            in_specs.append(pl.BlockSpec((tm, y.shape[1]), lambda i: (i, 0)))
        for w, rows, blk in ws:
            in_specs.append(_resident((1, rows, D_MODEL), lambda i, blk=blk: (w_layer, blk, 0)))
        args += list(ys) + [w for w, _, _ in ws]
    if final_g is not None:
        in_specs.append(pl.BlockSpec((1, D_MODEL), lambda i: (0, 0)))
        args.append(final_g.reshape(1, D_MODEL))
    return pl.pallas_call(
        functools.partial(_ffn_kernel, s=s, n_pre=n_pre, final=final_g is not None, layer=layer, which=which),
        grid=(m // tm,),
        in_specs=in_specs,
        out_specs=pl.BlockSpec((tm, D_MODEL), lambda i: (i, 0)),
        out_shape=jax.ShapeDtypeStruct((m, D_MODEL), F32),
        scratch_shapes=[pltpu.VMEM((D_MODEL, 2 * FFN_DIM), BF16),
                        pltpu.VMEM((FFN_DIM, D_MODEL), BF16),
                        pltpu.VMEM((FFN_AHEAD + 1, 2, D_MODEL, FFN_TF), F32),
                        pltpu.VMEM((FFN_AHEAD + 1, FFN_TF, D_MODEL), F32),
                        pltpu.SemaphoreType.DMA((FFN_AHEAD + 1, 3))],
        compiler_params=_cparams(1),
        name="ffn",
    )(*args)


_EV_QA = 0
_EV_KA = NA_WIDTH
_EV_VA = 2 * NA_WIDTH
_EV_CQ = 3 * NA_WIDTH
_EV_CKV = _EV_CQ + MLA_Q_RANK
_EV_COLS = _EV_CKV + MLA_KV_RANK + MLA_ROPE_DIM
MLA_QK_W = MLA_HEADS * LANES
LOG2E = 1.4426950408889634
NA_QSCALE = NA_HEAD_DIM ** -0.5 * LOG2E
MLA_QSCALE = (MLA_NOPE_DIM + MLA_ROPE_DIM) ** -0.5 * LOG2E


def _rope_swap_perm(width, half):
    j = np.arange(width)
    return np.where((j % (2 * half)) < half, j + half, j - half)


def _pack_even_weights(mix_w_in, mla_w_uq, mla_w_ukv):
    n_even = mix_w_in.shape[0]
    o = 3 * NA_WIDTH + MLA_Q_RANK + MLA_KV_RANK
    w_kr = mix_w_in[:, :, o:o + MLA_ROPE_DIM]
    w_krs = w_kr[:, :, _rope_swap_perm(MLA_ROPE_DIM, MLA_ROPE_DIM // 4)]
    pad = ((0, 0), (0, 0), (MLA_NOPE_DIM, LANES - MLA_NOPE_DIM - MLA_ROPE_DIM))
    w_kr2 = jnp.concatenate([jnp.pad(w_kr, pad), jnp.pad(w_krs, pad)], axis=-1)

    uq = mla_w_uq.reshape(n_even, MLA_Q_RANK, MLA_HEADS, MLA_NOPE_DIM + MLA_ROPE_DIM)
    uq_rope = uq[..., MLA_NOPE_DIM:]
    uq_rope_s = uq_rope[..., _rope_swap_perm(MLA_ROPE_DIM, MLA_ROPE_DIM // 4)]
    zpad = LANES - MLA_NOPE_DIM - MLA_ROPE_DIM
    w_uq = jnp.pad(uq, ((0, 0), (0, 0), (0, 0), (0, zpad))).reshape(n_even, MLA_Q_RANK, MLA_QK_W)
    w_uqs = jnp.pad(uq_rope_s, ((0, 0), (0, 0), (0, 0), (MLA_NOPE_DIM, zpad)))
    w_uqs = w_uqs.reshape(n_even, MLA_Q_RANK, MLA_QK_W)

    ukv = mla_w_ukv.reshape(n_even, MLA_KV_RANK, MLA_HEADS, MLA_NOPE_DIM + MLA_V_DIM)
    w_uk = jnp.pad(ukv[..., :MLA_NOPE_DIM], ((0, 0), (0, 0), (0, 0), (0, LANES - MLA_NOPE_DIM)))
    w_uk = w_uk.reshape(n_even, MLA_KV_RANK, MLA_QK_W)
    w_uv = ukv[..., MLA_NOPE_DIM:].reshape(n_even, MLA_KV_RANK, MLA_OUT)
    return (mix_w_in.astype(BF16), w_kr2.astype(BF16), w_uq.astype(BF16), w_uqs.astype(BF16),
            w_uk.astype(BF16), w_uv.astype(BF16))


def _rope_angles(seq, width, group):
    half = group // 2
    t = np.arange(seq)
    j = np.arange(width)
    pos = np.where((j // group)[None, :] % 2 == 0, (t // GRID_W)[:, None], (t % GRID_W)[:, None])
    inv = np.float32(ROPE_BASE) ** (-np.arange(half, dtype=np.float32) / np.float32(half))
    ang = pos.astype(np.float32) * inv[j % half][None, :]
    sign = np.where((j % group) < half, -1.0, 1.0).astype(np.float32)
    return ang, sign


def _mla_rope_tables(seq):
    ang, sign = _rope_angles(seq, MLA_ROPE_DIM, MLA_ROPE_DIM // 2)
    tail = LANES - MLA_NOPE_DIM - MLA_ROPE_DIM
    cos = np.concatenate([np.ones((seq, MLA_NOPE_DIM), np.float32), np.cos(ang),
                          np.ones((seq, tail), np.float32)], axis=1)
    sin = np.concatenate([np.zeros((seq, MLA_NOPE_DIM), np.float32), np.sin(ang) * sign[None, :],
                          np.zeros((seq, tail), np.float32)], axis=1)
    return jnp.asarray(cos), jnp.asarray(sin)


def _even_proj_kernel(*refs, latent, n_alias):
    (x_ref, mod_ref, g_ref, win_ref, wkr_ref, qn_ref, kvn_ref, wuq_ref, wuqs_ref, wuk_ref,
     wuv_ref) = refs[:11]
    if latent:
        cos_ref, sin_ref = refs[11:13]
        qa_ref, ka_ref, va_ref, qm_ref, km_ref, vm_ref = refs[13:]
    else:
        qa_ref, ka_ref, va_ref, qm_ref, km_ref, vm_ref, ckv_ref, kr_ref = refs[11 + n_alias:]

    def w_in(lo, width):
        return win_ref[0, :, lo:lo + width]

    h = _modulated(x_ref[...], mod_ref, g_ref, 1).astype(BF16)
    qa_ref[...] = (_dot(h, w_in(_EV_QA, NA_WIDTH)) * NA_QSCALE).astype(qa_ref.dtype)
    ka = _dot(h, w_in(_EV_KA, NA_WIDTH))
    va = _dot(h, w_in(_EV_VA, NA_WIDTH))
    if latent:
        ka_ref[...] = ka.astype(BF16)
        va_ref[...] = va.astype(BF16)
    else:
        ka_ref[:, 0] = ka.reshape(ka_ref.shape[0], ka_ref.shape[2], NA_WIDTH)
        va_ref[:, 0] = va.reshape(va_ref.shape[0], va_ref.shape[2], NA_WIDTH)

    cqn = _rms(_dot(h, w_in(_EV_CQ, MLA_Q_RANK)), qn_ref[0]).astype(BF16)
    ckvn = _rms(_dot(h, w_in(_EV_CKV, MLA_KV_RANK)), kvn_ref[0])
    kr = _dot(h, wkr_ref[0, :, :LANES])
    if latent:
        cos = cos_ref[...]
        sin = sin_ref[...]
        kr = kr * cos + _dot(h, wkr_ref[0, :, LANES:]) * sin
    else:
        ckv_ref[:, 0] = ckvn.reshape(ckv_ref.shape[0], ckv_ref.shape[2], MLA_KV_RANK)
        kr_ref[:, 0] = kr[:, MLA_NOPE_DIM:MLA_NOPE_DIM + MLA_ROPE_DIM].reshape(
            kr_ref.shape[0], kr_ref.shape[2], MLA_ROPE_DIM)
    ckvb = ckvn.astype(BF16)

    qm = _dot(cqn, wuq_ref[0])
    km = _dot(ckvb, wuk_ref[0])
    if latent:
        qms = _dot(cqn, wuqs_ref[0])
    for hd in range(MLA_HEADS):
        sl = slice(hd * LANES, (hd + 1) * LANES)
        q_h = qm[:, sl]
        if latent:
            q_h = q_h * cos + qms[:, sl] * sin
        qm_ref[:, sl] = (q_h * MLA_QSCALE).astype(BF16)
        km_ref[:, sl] = (km[:, sl] + kr).astype(BF16)
    vm_ref[...] = _dot(ckvb, wuv_ref[0]).astype(BF16)


def _even_proj(x, mods, norm_g, packed, q_norm, kv_norm, layer, latent, seq, n_even, rope=None, caches=None):
    w_in, w_kr2, w_uq, w_uqs, w_uk, w_uv = packed
    i_even = layer // 2
    m = x.shape[0]
    tm = EVEN_PROJ_TM
    assert m % tm == 0 and (tm % seq == 0 or seq % tm == 0)
    bt = tm // seq if not latent else 0
    tok = lambda i: (i, 0)
    in_specs = [pl.BlockSpec((tm, D_MODEL), tok),
                pl.BlockSpec((1, 1, N_MOD * D_MODEL), _mod_index(layer, latent, tm, seq)),
                pl.BlockSpec((1, 1, D_MODEL), lambda i: (layer * 3 + 1, 0, 0)),
                _resident((1, D_MODEL, _EV_COLS), lambda i: (i_even, 0, 0)),
                _resident((1, D_MODEL, 2 * LANES), lambda i: (i_even, 0, 0)),
                pl.BlockSpec((1, 1, MLA_Q_RANK), lambda i: (i_even, 0, 0)),
                pl.BlockSpec((1, 1, MLA_KV_RANK), lambda i: (i_even, 0, 0)),
                _resident((1, MLA_Q_RANK, MLA_QK_W), lambda i: (i_even, 0, 0)),
                _resident((1, MLA_Q_RANK, MLA_QK_W), lambda i: (i_even, 0, 0)),
                _resident((1, MLA_KV_RANK, MLA_QK_W), lambda i: (i_even, 0, 0)),
                _resident((1, MLA_KV_RANK, MLA_OUT), lambda i: (i_even, 0, 0))]
    args = [x, mods, norm_g, w_in, w_kr2, q_norm, kv_norm, w_uq, w_uqs, w_uk, w_uv]
    bf = lambda w: jax.ShapeDtypeStruct((m, w), BF16)
    aliases = {}
    if latent:
        nt = seq // tm
        in_specs += [pl.BlockSpec((tm, LANES), lambda i: (i % nt, 0))] * 2
        args += list(rope)
        out_shape = [bf(NA_WIDTH), bf(NA_WIDTH), bf(NA_WIDTH), bf(MLA_QK_W), bf(MLA_QK_W), bf(MLA_OUT)]
        out_specs = [pl.BlockSpec((tm, s.shape[1]), tok) for s in out_shape]
    else:
        nb = m // seq
        cache = lambda w: jax.ShapeDtypeStruct((nb, n_even, seq, w), F32)
        cspec = lambda w: pl.BlockSpec((bt, 1, seq, w), lambda i: (i, i_even, 0, 0))
        out_shape = [bf(NA_WIDTH), cache(NA_WIDTH), cache(NA_WIDTH), bf(MLA_QK_W), bf(MLA_QK_W),
                     bf(MLA_OUT), cache(MLA_KV_RANK), cache(MLA_ROPE_DIM)]
        out_specs = [pl.BlockSpec((tm, NA_WIDTH), tok), cspec(NA_WIDTH), cspec(NA_WIDTH),
                     pl.BlockSpec((tm, MLA_QK_W), tok), pl.BlockSpec((tm, MLA_QK_W), tok),
                     pl.BlockSpec((tm, MLA_OUT), tok), cspec(MLA_KV_RANK), cspec(MLA_ROPE_DIM)]
        if caches is not None:
            for arr, out_idx in zip(caches, (1, 2, 6, 7)):
                aliases[len(args)] = out_idx
                in_specs.append(pl.BlockSpec(memory_space=pl.ANY))
                args.append(arr)
    return pl.pallas_call(
        functools.partial(_even_proj_kernel, latent=latent, n_alias=len(aliases)),
        grid=(m // tm,),
        in_specs=in_specs,
        out_specs=out_specs,
        out_shape=out_shape,
        input_output_aliases=aliases,
        compiler_params=_cparams(1),
        name="even_proj_latent" if latent else "even_proj_context",
    )(*args)


def _mla_cache_kernel(ckv_ref, kr_ref, wuk_ref, wuv_ref, km_ref, vm_ref):
    ckvb = ckv_ref[0, 0].astype(BF16)
    km = _dot(ckvb, wuk_ref[0])
    kr = kr_ref[...]
    for hd in range(MLA_HEADS):
        sl = slice(hd * LANES, (hd + 1) * LANES)
        km_ref[:, sl] = (km[:, sl] + kr).astype(BF16)
    vm_ref[...] = _dot(ckvb, wuv_ref[0]).astype(BF16)


def _mla_cache_keys(cache_ckv, kr_padded, w_uk, w_uv, i_even):
    nb, _, past, _ = cache_ckv.shape
    return pl.pallas_call(
        _mla_cache_kernel,
        grid=(nb,),
        in_specs=[pl.BlockSpec((1, 1, past, MLA_KV_RANK), lambda b: (b, i_even, 0, 0)),
                  pl.BlockSpec((past, LANES), lambda b: (b, 0)),
                  _resident((1, MLA_KV_RANK, MLA_QK_W), lambda b: (i_even, 0, 0)),
                  _resident((1, MLA_KV_RANK, MLA_OUT), lambda b: (i_even, 0, 0))],
        out_specs=[pl.BlockSpec((past, MLA_QK_W), lambda b: (b, 0)),
                   pl.BlockSpec((past, MLA_OUT), lambda b: (b, 0))],
        out_shape=[jax.ShapeDtypeStruct((nb * past, MLA_QK_W), BF16),
                   jax.ShapeDtypeStruct((nb * past, MLA_OUT), BF16)],
        compiler_params=_cparams(1),
        name="mla_cache_keys",
    )(cache_ckv, kr_padded, w_uk, w_uv)


def _half_masks():
    lane = lax.broadcasted_iota(jnp.int32, (1, LANES), 1)
    lo = lane < (LANES // 2)
    return lo, jnp.logical_not(lo)


def _scores_pass(q, key_blocks, bias_blocks, s_ref, kb):
    mx = None
    for j, (k_blk, bias) in enumerate(zip(key_blocks, bias_blocks)):
        sc = _dot_nt(q, k_blk())
        if bias is not None:
            sc = sc + bias()
        s_ref[:, j * kb:(j + 1) * kb] = sc
        for g in range(kb // LANES):
            part = sc[:, g * LANES:(g + 1) * LANES]
            mx = part if mx is None else jnp.maximum(mx, part)
    return jnp.max(mx, axis=-1, keepdims=True)


def _values_pass(s_ref, m, value_blocks, kb):
    ls = None
    acc = None
    for j, v_blk in enumerate(value_blocks):
        p = jnp.exp2(s_ref[:, j * kb:(j + 1) * kb] - m)
        for g in range(kb // LANES):
            part = p[:, g * LANES:(g + 1) * LANES]
            ls = part if ls is None else ls + part
        o = _dot(p.astype(BF16), v_blk())
        acc = o if acc is None else acc + o
    return acc / jnp.sum(ls, axis=-1, keepdims=True)


def _attention_chains(chains, s_scr, kb=None):
    kb = ATT_KB if kb is None else kb
    depth = s_scr.shape[0]
    outs = []
    maxes = {}
    for i in range(len(chains) + depth - 1):
        if i < len(chains):
            q, key_blocks, _, bias_blocks = chains[i]
            maxes[i] = _scores_pass(q(), key_blocks, bias_blocks, s_scr.at[i % depth], kb)
        j = i - (depth - 1)
        if j >= 0:
            outs.append(_values_pass(s_scr.at[j % depth], maxes.pop(j), chains[j][2], kb))
    return outs


def _rows(ref, start, size, lanes):
    return lambda: ref[start:start + size, lanes]


def _ctx_attn_kernel(qa_ref, ka_ref, va_ref, qm_ref, km_ref, vm_ref, o_ref, s_scr, kv_scr):
    lo, hi = _half_masks()
    zero = jnp.zeros((), BF16)
    nbt, seq = qa_ref.shape[0], qa_ref.shape[1]
    kb_sz = min(ATT_KB, seq)
    qs_sz = min(ATT_QS, seq)
    chains = []
    for bb in range(nbt):
        kv_scr[bb, 0] = ka_ref[bb, 0].astype(BF16)
        kv_scr[bb, 1] = va_ref[bb, 0].astype(BF16)
        for g in range((NA_WIDTH + MLA_OUT) // LANES):
            mla = g >= NA_WIDTH // LANES
            gl = g - NA_WIDTH // LANES if mla else g
            sl = slice(gl * LANES, (gl + 1) * LANES)
            for qs in range(seq // qs_sz):
                rows = slice(qs * qs_sz, (qs + 1) * qs_sz)
                for t, msk in enumerate((lo, hi)):
                    if mla:
                        hsl = slice((2 * gl + t) * LANES, (2 * gl + t + 1) * LANES)
                        q = lambda bb=bb, rows=rows, hsl=hsl: qm_ref[bb, rows, hsl]
                        kb = [_rows(km_ref.at[bb], j * kb_sz, kb_sz, hsl) for j in range(seq // kb_sz)]
                        vb = [_rows(vm_ref.at[bb], j * kb_sz, kb_sz, sl) for j in range(seq // kb_sz)]
                    else:
                        q = lambda bb=bb, rows=rows, sl=sl, msk=msk: jnp.where(msk, qa_ref[bb, rows, sl], zero)
                        kb = [_rows(kv_scr.at[bb, 0], j * kb_sz, kb_sz, sl) for j in range(seq // kb_sz)]
                        vb = [_rows(kv_scr.at[bb, 1], j * kb_sz, kb_sz, sl) for j in range(seq // kb_sz)]
                    chains.append((q, kb, vb, [None] * len(kb)))
    outs = _attention_chains(chains, s_scr, kb_sz)
    n = 0
    for bb in range(nbt):
        for g in range((NA_WIDTH + MLA_OUT) // LANES):
            for qs in range(seq // qs_sz):
                o_ref[bb, qs * qs_sz:(qs + 1) * qs_sz, g * LANES:(g + 1) * LANES] = (
                    jnp.where(lo, outs[n], outs[n + 1]).astype(BF16))
                n += 2


def _ctx_attention(qa, ka, va, qm, km, vm, seq, i_even):
    m = qa.shape[0]
    nb = m // seq
    nbt = CTX_BATCHES
    assert nb % nbt == 0
    tok = lambda b: (b, 0, 0)
    cache = lambda b: (b, i_even, 0, 0)
    r3 = lambda a: a.reshape(nb, seq, a.shape[1])
    out = pl.pallas_call(
        _ctx_attn_kernel,
        grid=(nb // nbt,),
        in_specs=[pl.BlockSpec((nbt, seq, NA_WIDTH), tok),
                  pl.BlockSpec((nbt, 1, seq, NA_WIDTH), cache),
                  pl.BlockSpec((nbt, 1, seq, NA_WIDTH), cache),
                  pl.BlockSpec((nbt, seq, MLA_QK_W), tok),
                  pl.BlockSpec((nbt, seq, MLA_QK_W), tok),
                  pl.BlockSpec((nbt, seq, MLA_OUT), tok)],
        out_specs=pl.BlockSpec((nbt, seq, NA_WIDTH + MLA_OUT), tok),
        out_shape=jax.ShapeDtypeStruct((nb, seq, NA_WIDTH + MLA_OUT), BF16),
        scratch_shapes=[pltpu.VMEM((2, min(ATT_QS, seq), seq), F32),
                        pltpu.VMEM((nbt, 2, seq, NA_WIDTH), BF16)],
        compiler_params=_cparams(1),
        name="context_attention",
    )(r3(qa), ka, va, r3(qm), r3(km), r3(vm))
    return out.reshape(m, NA_WIDTH + MLA_OUT)


def _lat_mla_kernel(q_ref, kc_ref, kl_ref, vc_ref, vl_ref, o_ref, s_scr):
    lo, _ = _half_masks()
    past, seq = kc_ref.shape[0], kl_ref.shape[0]
    all_lanes = slice(0, LANES)
    kbs = []
    for t in range(2):
        sl = slice(t * LANES, (t + 1) * LANES)
        kbs.append([_rows(kc_ref, j * ATT_KB, ATT_KB, sl) for j in range(past // ATT_KB)]
                   + [_rows(kl_ref, j * ATT_KB, ATT_KB, sl) for j in range(seq // ATT_KB)])
    vb = ([_rows(vc_ref, j * ATT_KB, ATT_KB, all_lanes) for j in range(past // ATT_KB)]
          + [_rows(vl_ref, j * ATT_KB, ATT_KB, all_lanes) for j in range(seq // ATT_KB)])
    tq = min(ATT_TQ, seq)
    n_qs = tq // ATT_QS

    def tile(i, carry):
        base = pl.multiple_of(i * tq, tq)
        chains = []
        for qs in range(n_qs):
            rows = pl.ds(base + qs * ATT_QS, ATT_QS)
            for t in range(2):
                sl = slice(t * LANES, (t + 1) * LANES)
                chains.append((lambda rows=rows, sl=sl: q_ref[rows, sl], kbs[t], vb, [None] * len(vb)))
        outs = _attention_chains(chains, s_scr)
        for qs in range(n_qs):
            o_ref[pl.ds(base + qs * ATT_QS, ATT_QS), :] = (
                jnp.where(lo, outs[2 * qs], outs[2 * qs + 1]).astype(BF16))
        return carry

    lax.fori_loop(0, seq // tq, tile, 0)


def _latent_mla_attention(qm, km_ctx, km_lat, vm_ctx, vm_lat, seq, past):
    m = qm.shape[0]
    nb = m // seq
    pair = 2 * LANES
    return pl.pallas_call(
        _lat_mla_kernel,
        grid=(nb, MLA_HEADS // 2),
        in_specs=[pl.BlockSpec((seq, pair), lambda b, g: (b, g)),
                  pl.BlockSpec((past, pair), lambda b, g: (b, g)),
                  pl.BlockSpec((seq, pair), lambda b, g: (b, g)),
                  pl.BlockSpec((past, LANES), lambda b, g: (b, g)),
                  pl.BlockSpec((seq, LANES), lambda b, g: (b, g))],
        out_specs=pl.BlockSpec((seq, LANES), lambda b, g: (b, g)),
        out_shape=jax.ShapeDtypeStruct((m, MLA_OUT), BF16),
        scratch_shapes=[pltpu.VMEM((2, ATT_QS, past + seq), F32)],
        compiler_params=_cparams(2),
        name="latent_mla_attention",
    )(qm, km_ctx, km_lat, vm_ctx, vm_lat)


def _na_window_start(first_row, rows, clip=np.clip):
    r0 = clip(first_row - NA_ROWS // 2, 0, rows - NA_ROWS)
    return clip(r0 - (NA_WROWS - NA_ROWS) // 2, 0, rows - NA_WROWS)


def _na_bias_kernel(rpb_ref, o_ref, *, rows):
    hd = pl.program_id(0)
    kr = min(NA_ROWS, rows)
    n_dc = 2 * NA_COLS - 1
    band = ATT_QS // GRID_W
    c = lax.broadcasted_iota(jnp.int32, (GRID_W, LANES), 0)
    lane = lax.broadcasted_iota(jnp.int32, (GRID_W, LANES), 1)
    n = lane % GRID_W
    c0 = jnp.clip(c - NA_COLS // 2, 0, GRID_W - NA_COLS)
    in_win = (n >= c0) & (n < c0 + NA_COLS)
    dc = n - c + NA_COLS - 1
    left = lane < GRID_W
    neg = jnp.full((GRID_W, LANES), NEG_BIG, F32)
    col_tables = []
    for dr in range(2 * NA_ROWS - 1):
        t = neg
        for j in range(n_dc):
            t = jnp.where(dc == j, rpb_ref[hd * (2 * NA_ROWS - 1) * n_dc + dr * n_dc + j] * LOG2E, t)
        col_tables.append(jnp.where(in_win, t, neg))
    for rb in range(rows // NA_QROWS):
        for rl in range(NA_QROWS):
            r = rb * NA_QROWS + rl
            r0 = min(max(r - kr // 2, 0), rows - kr)
            start = int(_na_window_start(r - rl % band, rows))
            assert start <= r0 and r0 + kr <= start + NA_WROWS
            for pr in range(NA_WROWS // 2):
                halves = []
                for key_row in (start + 2 * pr, start + 2 * pr + 1):
                    ok = r0 <= key_row < r0 + kr
                    halves.append(col_tables[key_row - r + NA_ROWS - 1] if ok else neg)
                o_ref[rb, 0, rl * GRID_W:(rl + 1) * GRID_W, pr * LANES:(pr + 1) * LANES] = (
                    jnp.where(left, halves[0], halves[1]))


def _na_bias(rpb, rows):
    n_rb = rows // NA_QROWS
    flat = rpb.reshape(-1)
    return pl.pallas_call(
        functools.partial(_na_bias_kernel, rows=rows),
        grid=(NA_HEADS,),
        in_specs=[pl.BlockSpec(memory_space=pltpu.SMEM)],
        out_specs=pl.BlockSpec((n_rb, 1, NA_QROWS * GRID_W, NA_WROWS * GRID_W), lambda h: (0, h, 0, 0)),
        out_shape=jax.ShapeDtypeStruct((n_rb, NA_HEADS, NA_QROWS * GRID_W, NA_WROWS * GRID_W), F32),
        compiler_params=_cparams(1),
        name="na_bias",
    )(flat)


def _lat_na_kernel(q_ref, k_ref, v_ref, kc_ref, vc_ref, bias_ref, o_ref, s_scr, kv_scr, *, rows):
    lo, hi = _half_masks()
    zero = jnp.zeros((), BF16)
    rb = pl.program_id(0)
    nk = NA_WROWS * GRID_W
    past = kc_ref.shape[2]
    all_lanes = slice(0, LANES)
    n_qs = q_ref.shape[1] // ATT_QS
    starts = [pl.multiple_of(_na_window_start(rb * NA_QROWS + qs * (ATT_QS // GRID_W), rows, jnp.clip) * GRID_W,
                             GRID_W) for qs in range(n_qs)]

    def batch(b, carry):
        kv_scr[0] = kc_ref[b, 0].astype(BF16)
        kv_scr[1] = vc_ref[b, 0].astype(BF16)

        def win(ref, start, j):
            return lambda: ref[b, pl.ds(start + j * ATT_KB, ATT_KB), :]

        ctx_k = [_rows(kv_scr.at[0], j * ATT_KB, ATT_KB, all_lanes) for j in range(past // ATT_KB)]
        ctx_v = [_rows(kv_scr.at[1], j * ATT_KB, ATT_KB, all_lanes) for j in range(past // ATT_KB)]
        chains = []
        for qs in range(n_qs):
            qrows = slice(qs * ATT_QS, (qs + 1) * ATT_QS)
            kb = [win(k_ref, starts[qs], j) for j in range(nk // ATT_KB)] + ctx_k
            vb = [win(v_ref, starts[qs], j) for j in range(nk // ATT_KB)] + ctx_v
            for t, msk in enumerate((lo, hi)):
                bias = ([(lambda t=t, j=j, qrows=qrows: bias_ref[0, t, qrows, j * ATT_KB:(j + 1) * ATT_KB])
                         for j in range(nk // ATT_KB)] + [None] * (past // ATT_KB))
                chains.append((lambda qrows=qrows, msk=msk: jnp.where(msk, q_ref[b, qrows, :], zero),
                               kb, vb, bias))
        outs = _attention_chains(chains, s_scr)
        for qs in range(n_qs):
            o_ref[b, qs * ATT_QS:(qs + 1) * ATT_QS, :] = (
                jnp.where(lo, outs[2 * qs], outs[2 * qs + 1]).astype(BF16))
        return carry

    lax.fori_loop(0, q_ref.shape[0], batch, 0, unroll=True)


def _latent_na_attention(qa, ka, va, cache_k, cache_v, bias, i_even, seq):
    m = qa.shape[0]
    nb = m // seq
    rows = seq // GRID_W
    n_rb = rows // NA_QROWS
    tq = NA_QROWS * GRID_W
    past = cache_k.shape[2]
    per_batch = lambda a: a.reshape(nb, seq, NA_WIDTH)
    out = pl.pallas_call(
        functools.partial(_lat_na_kernel, rows=rows),
        grid=(n_rb, NA_HEADS // 2),
        in_specs=[pl.BlockSpec((nb, tq, LANES), lambda r, g: (0, r, g)),
                  pl.BlockSpec((nb, seq, LANES), lambda r, g: (0, 0, g)),
                  pl.BlockSpec((nb, seq, LANES), lambda r, g: (0, 0, g)),
                  pl.BlockSpec((nb, 1, past, LANES), lambda r, g: (0, i_even, 0, g)),
                  pl.BlockSpec((nb, 1, past, LANES), lambda r, g: (0, i_even, 0, g)),
                  pl.BlockSpec((1, 2, tq, NA_WROWS * GRID_W), lambda r, g: (r, g, 0, 0))],
        out_specs=pl.BlockSpec((nb, tq, LANES), lambda r, g: (0, r, g)),
        out_shape=jax.ShapeDtypeStruct((nb, seq, NA_WIDTH), BF16),
        scratch_shapes=[pltpu.VMEM((2, ATT_QS, NA_WROWS * GRID_W + past), F32),
                        pltpu.VMEM((2, past, LANES), BF16)],
        compiler_params=_cparams(2),
        name="latent_na_attention",
    )(per_batch(qa), per_batch(ka), per_batch(va), cache_k, cache_v, bias)
    return out.reshape(m, NA_WIDTH)


_RET_NK = RET_HEADS * RET_DK
_RET_NV = RET_HEADS * RET_DV


def _ret_proj_kernel(x_ref, mod_ref, g_ref, w_ref, cos_ref, sin_ref, q_ref, k_ref, v_ref, gate_ref):
    h = _modulated(x_ref[...], mod_ref, g_ref, 1).astype(BF16)

    def rope(y):
        parts = [pltpu.roll(y[:, g * LANES:(g + 1) * LANES], LANES // 2, 1) for g in range(RET_DK // LANES)]
        return y * cos_ref[...] + jnp.concatenate(parts, axis=1) * sin_ref[...]

    for hd in range(RET_HEADS):
        sl = slice(hd * RET_DK, (hd + 1) * RET_DK)
        q_ref[:, sl] = rope(_dot(h, w_ref[0, :, sl])).astype(BF16)
        k = _dot(h, w_ref[0, :, _RET_NK + hd * RET_DK:_RET_NK + (hd + 1) * RET_DK]) * (RET_DK ** -0.5)
        k_ref[:, sl] = rope(k)
    step = 512
    for j in range(_RET_NV // step):
        o = 2 * _RET_NK + j * step
        v_ref[:, j * step:(j + 1) * step] = _dot(h, w_ref[0, :, o:o + step]).astype(BF16)
    for j in range(2 * _RET_NV // step):
        o = 2 * _RET_NK + _RET_NV + j * step
        gate_ref[:, j * step:(j + 1) * step] = _dot(h, w_ref[0, :, o:o + step])


def _ret_proj_latent(x, mods, norm_g, w_in, layer, seq, rope):
    m = x.shape[0]
    tm = RET_PROJ_TM
    assert m % tm == 0 and seq % tm == 0
    i_odd = layer // 2
    tok = lambda i: (i, 0)
    n_in = 2 * _RET_NK + 3 * _RET_NV
    nt = seq // tm
    return pl.pallas_call(
        _ret_proj_kernel,
        grid=(m // tm,),
        in_specs=[pl.BlockSpec((tm, D_MODEL), tok),
                  pl.BlockSpec((1, 1, N_MOD * D_MODEL), _mod_index(layer, True, tm, seq)),
                  pl.BlockSpec((1, 1, D_MODEL), lambda i: (layer * 3 + 1, 0, 0)),
                  _resident((1, D_MODEL, n_in), lambda i: (i_odd, 0, 0)),
                  pl.BlockSpec((tm, RET_DK), lambda i: (i % nt, 0)),
                  pl.BlockSpec((tm, RET_DK), lambda i: (i % nt, 0))],
        out_specs=[pl.BlockSpec((tm, _RET_NK), tok),
                   pl.BlockSpec((tm, _RET_NK), tok),
                   pl.BlockSpec((tm, _RET_NV), tok),
                   pl.BlockSpec((tm, 2 * _RET_NV), tok)],
        out_shape=[jax.ShapeDtypeStruct((m, _RET_NK), BF16),
                   jax.ShapeDtypeStruct((m, _RET_NK), F32),
                   jax.ShapeDtypeStruct((m, _RET_NV), BF16),
                   jax.ShapeDtypeStruct((m, 2 * _RET_NV), F32)],
        compiler_params=_cparams(1),
        name="ret_proj_latent",
    )(x, mods, norm_g, w_in, *rope)


def _ret_rope_tables(seq):
    ang, sign = _rope_angles(seq, RET_DK, RET_DK // 2)
    return jnp.asarray(np.cos(ang)), jnp.asarray(np.sin(ang) * sign[None, :])


def _gated_head_norm(g, o):
    mu = jnp.mean(o, axis=-1, keepdims=True)
    d = o - mu
    half_rs = 0.5 * lax.rsqrt(jnp.mean(d * d, axis=-1, keepdims=True) + EPS)
    u = g * d
    return (u + u * jnp.tanh(0.5 * g)) * half_rs


def _ret_decays(lg, dirn, c):
    ii = lax.broadcasted_iota(jnp.int32, (c, c), 0)
    jj = lax.broadcasted_iota(jnp.int32, (c, c), 1)
    row = lax.broadcasted_iota(jnp.int32, (c, 1), 0).astype(F32)
    diff = (ii - jj if dirn == 0 else jj - ii).astype(F32)
    intra = jnp.where(diff >= 0, jnp.exp(lg * jnp.maximum(diff, 0.0)), 0.0)
    if dirn == 0:
        q_dec = jnp.exp(lg * (row + 1.0))
        k_dec = jnp.exp(lg * (c - 1.0 - row))
    else:
        q_dec = jnp.exp(lg * (c - row))
        k_dec = jnp.exp(lg * row)
    return intra, q_dec, k_dec, jnp.exp(lg * float(c))


def _ret_ctx_kernel(*refs, n_alias):
    ld_ref, x_ref, mod_ref, g_ref, w_ref = refs[:5]
    y_ref, st_ref = refs[5 + n_alias:]
    c = x_ref.shape[0]
    h = _modulated(x_ref[...], mod_ref, g_ref, 1).astype(BF16)
    for hd in range(RET_HEADS):
        ksl = slice(hd * RET_DK, (hd + 1) * RET_DK)
        qb = _dot(h, w_ref[0, :, ksl]).astype(BF16)
        k = _dot(h, w_ref[0, :, _RET_NK + hd * RET_DK:_RET_NK + (hd + 1) * RET_DK]) * (RET_DK ** -0.5)
        vo = 2 * _RET_NK + hd * RET_DV
        v = _dot(h, w_ref[0, :, vo:vo + RET_DV]).astype(BF16)
        qk = _dot_nt(qb, k.astype(BF16))
        y = None
        for dirn in range(2):
            intra, _, k_dec, _ = _ret_decays(-jnp.abs(ld_ref[dirn, hd]), dirn, c)
            o = _dot((qk * intra).astype(BF16), v)
            st_ref[0, 0, dirn, hd] = _dot((k * k_dec).T.astype(BF16), v)
            go = 2 * _RET_NK + (1 + dirn) * _RET_NV + hd * RET_DV
            yd = _gated_head_norm(_dot(h, w_ref[0, :, go:go + RET_DV]), o)
            y = yd if y is None else y + yd
        y_ref[:, hd * RET_DV:(hd + 1) * RET_DV] = y.astype(BF16)


def _retention_context(x, mods, norm_g, w_in, log_decay, layer, seq, n_odd, prev_states=None):
    assert seq == RET_CHUNK
    m = x.shape[0]
    nb = m // seq
    i_odd = layer // 2
    n_in = 2 * _RET_NK + 3 * _RET_NV
    in_specs = [pl.BlockSpec(memory_space=pltpu.SMEM),
                pl.BlockSpec((seq, D_MODEL), lambda b: (b, 0)),
                pl.BlockSpec((1, 1, N_MOD * D_MODEL), _mod_index(layer, False, seq, seq)),
                pl.BlockSpec((1, 1, D_MODEL), lambda b: (layer * 3 + 1, 0, 0)),
                _resident((1, D_MODEL, n_in), lambda b: (i_odd, 0, 0))]
    args = [log_decay[i_odd], x, mods, norm_g, w_in]
    aliases = {}
    if prev_states is not None:
        aliases[len(args)] = 1
        in_specs.append(pl.BlockSpec(memory_space=pl.ANY))
        args.append(prev_states)
    return pl.pallas_call(
        functools.partial(_ret_ctx_kernel, n_alias=len(aliases)),
        grid=(nb,),
        in_specs=in_specs,
        out_specs=[pl.BlockSpec((seq, _RET_NV), lambda b: (b, 0)),
                   pl.BlockSpec((1, 1, 2, RET_HEADS, RET_DK, RET_DV), lambda b: (b, i_odd, 0, 0, 0, 0))],
        out_shape=[jax.ShapeDtypeStruct((m, _RET_NV), BF16),
                   jax.ShapeDtypeStruct((nb, n_odd, 2, RET_HEADS, RET_DK, RET_DV), F32)],
        input_output_aliases=aliases,
        compiler_params=_cparams(1),
        name="retention_context",
    )(*args)


def _ret_lat_kernel(ld_ref, q_ref, k_ref, v_ref, gf_ref, gb_ref, s0_ref, y_ref, s_scr, o_scr):
    hd = pl.program_id(1)
    c = RET_CHUNK
    n_chunks = q_ref.shape[0] // c
    decays = [_ret_decays(-jnp.abs(ld_ref[dirn, hd]), dirn, c) for dirn in range(2)]
    gates = (gf_ref, gb_ref)
    for dirn in range(2):
        s_scr[dirn] = s0_ref[0, 0, dirn, 0]

    def scan(first_visit):
        def body(t, carry):
            for dirn in range(2):
                intra, q_dec, k_dec, c_dec = decays[dirn]
                ci = t if dirn == 0 else n_chunks - 1 - t
                rows = pl.ds(pl.multiple_of(ci * c, c), c)
                qb = q_ref[rows, :]
                k = k_ref[rows, :]
                v = v_ref[rows, :]
                a = _dot_nt(qb, k.astype(BF16)) * intra
                s_prev = s_scr[dirn]
                o = _dot(a.astype(BF16), v) + _dot(qb, s_prev.astype(BF16)) * q_dec
                s_scr[dirn] = s_prev * c_dec + _dot((k * k_dec).T.astype(BF16), v)
                y = _gated_head_norm(gates[dirn][rows, :], o)
                if first_visit:
                    o_scr[rows, :] = y
                else:
                    y_ref[rows, :] = (o_scr[rows, :] + y).astype(BF16)
            return carry
        return body

    lax.fori_loop(0, n_chunks // 2, scan(True), 0, unroll=True)
    lax.fori_loop(n_chunks // 2, n_chunks, scan(False), 0, unroll=True)


def _retention_latent(q, k, v, gates, log_decay, i_odd, seq, state):
    m = q.shape[0]
    nb = m // seq
    assert (seq // RET_CHUNK) % 2 == 0
    return pl.pallas_call(
        _ret_lat_kernel,
        grid=(nb, RET_HEADS),
        in_specs=[pl.BlockSpec(memory_space=pltpu.SMEM),
                  pl.BlockSpec((seq, RET_DK), lambda b, h: (b, h)),
                  pl.BlockSpec((seq, RET_DK), lambda b, h: (b, h)),
                  pl.BlockSpec((seq, RET_DV), lambda b, h: (b, h)),
                  pl.BlockSpec((seq, RET_DV), lambda b, h: (b, h)),
                  pl.BlockSpec((seq, RET_DV), lambda b, h: (b, RET_HEADS + h)),
                  pl.BlockSpec((1, 1, 2, 1, RET_DK, RET_DV), lambda b, h: (b, i_odd, 0, h, 0, 0))],
        out_specs=pl.BlockSpec((seq, RET_DV), lambda b, h: (b, h)),
        out_shape=jax.ShapeDtypeStruct((m, _RET_NV), BF16),
        scratch_shapes=[pltpu.VMEM((2, RET_DK, RET_DV), F32), pltpu.VMEM((seq, RET_DV), F32)],
        compiler_params=_cparams(2),
        name="retention_latent",
    )(log_decay[i_odd], q, k, v, gates, gates, state)


def kernel(x_prompt, x_sample, c, cache_na_k, cache_na_v, cache_mla_ckv, cache_mla_krope, state_ret,
           c_ctx, norm_g, ada_w, ada_b, ffn_w13, ffn_w2, mix_w_in, mla_q_norm, mla_kv_norm, mla_w_uq,
           mla_w_ukv, na_rpb, mix_w_out, ret_w_in, ret_log_decay, ret_w_out, final_norm_g):
    batch, seq, _ = x_prompt.shape
    dec_batch, dec_seq, _ = x_sample.shape
    past = cache_na_k.shape[2]
    n_even = mix_w_in.shape[0]
    assert dec_batch + 1 <= MOD_ROWS and seq == RET_CHUNK and dec_seq % RET_CHUNK == 0
    assert (dec_seq // GRID_W) % NA_QROWS == 0 and dec_seq // GRID_W >= NA_WROWS

    w_out_even = mix_w_out.astype(BF16)
    w_in_ret = ret_w_in.astype(BF16)
    w_out_ret = ret_w_out.astype(BF16)
    packed = _pack_even_weights(mix_w_in, mla_w_uq, mla_w_ukv)
    q_norm = mla_q_norm.reshape(n_even, 1, MLA_Q_RANK)
    kv_norm = mla_kv_norm.reshape(n_even, 1, MLA_KV_RANK)
    norm_g3 = norm_g.reshape(DEPTH * 3, 1, D_MODEL)
    mla_rope = _mla_rope_tables(dec_seq)
    ret_rope = _ret_rope_tables(dec_seq)
    cache_k = cache_na_k.reshape(dec_batch, n_even, past, NA_WIDTH)
    cache_v = cache_na_v.reshape(dec_batch, n_even, past, NA_WIDTH)

    cvec = jnp.concatenate([c_ctx[None, :], c, jnp.zeros((MOD_ROWS - 1 - dec_batch, D_MODEL), F32)], axis=0)
    mods = _ada_all(cvec, ada_w, ada_b).reshape(DEPTH * MOD_ROWS, 1, N_MOD * D_MODEL)

    xp = x_prompt.reshape(batch * seq, D_MODEL)
    xs = x_sample.reshape(dec_batch * dec_seq, D_MODEL)
    n_odd = ret_w_in.shape[0]
    caches = None
    states = None
    for layer in range(DEPTH):
        i = layer // 2
        xp = _ffn(xp, mods, norm_g3, ffn_w13, ffn_w2, layer, 0, False, seq)
        xs = _ffn(xs, mods, norm_g3, ffn_w13, ffn_w2, layer, 0, True, dec_seq)
        if layer % 2 == 0:
            qa, ka, va, qm, km, vm, ckv, kr = _even_proj(
                xp, mods, norm_g3, packed, q_norm, kv_norm, layer, False, seq, n_even, caches=caches)
            caches = (ka, va, ckv, kr)
            op = _ctx_attention(qa, ka, va, qm, km, vm, seq, i)
            pre_p = ([op], [(w_out_even, NA_WIDTH + MLA_OUT, 0)], i)

            qa, ka, va, qm, km, vm = _even_proj(
                xs, mods, norm_g3, packed, q_norm, kv_norm, layer, True, dec_seq, n_even, rope=mla_rope)
            kr_pad = jnp.pad(cache_mla_krope[:, i].reshape(dec_batch * past, MLA_ROPE_DIM),
                             ((0, 0), (MLA_NOPE_DIM, LANES - MLA_NOPE_DIM - MLA_ROPE_DIM)))
            km_ctx, vm_ctx = _mla_cache_keys(cache_mla_ckv, kr_pad, packed[4], packed[5], i)
            bias = _na_bias(na_rpb[i], dec_seq // GRID_W)
            oa = _latent_na_attention(qa, ka, va, cache_k, cache_v, bias, i, dec_seq)
            ob = _latent_mla_attention(qm, km_ctx, km, vm_ctx, vm, dec_seq, past)
            pre_s = ([oa, ob], [(w_out_even, NA_WIDTH, 0), (w_out_even, MLA_OUT, NA_WIDTH // MLA_OUT)], i)
        else:
            yp, states = _retention_context(xp, mods, norm_g3, w_in_ret, ret_log_decay, layer, seq, n_odd,
                                            prev_states=states)
            pre_p = ([yp], [(w_out_ret, _RET_NV, 0)], i)

            q, k, v, gates = _ret_proj_latent(xs, mods, norm_g3, w_in_ret, layer, dec_seq, ret_rope)
            ys = _retention_latent(q, k, v, gates, ret_log_decay, i, dec_seq, state_ret)
            pre_s = ([ys], [(w_out_ret, _RET_NV, 0)], i)
        last = final_norm_g if layer == DEPTH - 1 else None
        xp = _ffn(xp, mods, norm_g3, ffn_w13, ffn_w2, layer, 1, False, seq, pre=pre_p, final_g=last)
        xs = _ffn(xs, mods, norm_g3, ffn_w13, ffn_w2, layer, 1, True, dec_seq, pre=pre_s, final_g=last)

    y_prompt = xp.reshape(batch, seq, D_MODEL)
    y_sample = xs.reshape(dec_batch, dec_seq, D_MODEL)
    ka, va, ckv, kr = caches
    new_na_k = ka.reshape(batch, n_even, seq, NA_HEADS, NA_HEAD_DIM)
    new_na_v = va.reshape(batch, n_even, seq, NA_HEADS, NA_HEAD_DIM)
    return (y_prompt, y_sample, new_na_k, new_na_v, ckv, kr, states)
```
